```python
import jax, jax.numpy as jnp
from jax import lax
import numpy as np

D_MODEL = 1024
BATCH = 4
SEQ = 4096
DEPTH = 1

D_PLE = 256
HEAD_DIM = 64
RET_WIDTH = D_MODEL // 2
RWKV_WIDTH = D_MODEL - RET_WIDTH
RET_HEADS = RET_WIDTH // HEAD_DIM
RWKV_HEADS = RWKV_WIDTH // HEAD_DIM
MIX_WIDTH = RET_WIDTH + RWKV_WIDTH
RET_CHUNK = 128
ROPE_BASE = 10000.0
DECAY_LORA = 64
AAA_LORA = 64
GATE_LORA = 160
D_FF = 2816
LN_EPS = 1e-5
RET_GN_EPS = 1e-5
RWKV_GN_EPS = 64e-5
DEEPNORM_ALPHA = (2.0 * DEPTH) ** 0.25
DEEPNORM_BETA = (8.0 * DEPTH) ** -0.25

RET_COLS = 4 * RET_WIDTH
RW_COLS = 3 * RWKV_WIDTH + DECAY_LORA + AAA_LORA + GATE_LORA
IN_COLS = RET_COLS + RW_COLS
RW_SPLITS = (RWKV_WIDTH, 2 * RWKV_WIDTH, 3 * RWKV_WIDTH,
             3 * RWKV_WIDTH + DECAY_LORA, 3 * RWKV_WIDTH + DECAY_LORA + AAA_LORA)

kernel_name = 'hymba_retnet_rwkv7_macaron_deepnorm'


def _layer_norm(x, g, b):
    xf = x.astype(jnp.float32)
    mu = jnp.mean(xf, axis=-1, keepdims=True)
    var = jnp.mean(jnp.square(xf - mu), axis=-1, keepdims=True)
    y = (xf - mu) * lax.rsqrt(var + LN_EPS) * g.astype(jnp.float32) + b.astype(jnp.float32)
    return y.astype(x.dtype)


def _head_group_norm(y, g, b, eps):
    mu = jnp.mean(y, axis=-1, keepdims=True)
    var = jnp.mean(jnp.square(y - mu), axis=-1, keepdims=True)
    bsz, s, h, d = y.shape
    yn = ((y - mu) * lax.rsqrt(var + eps)).reshape(bsz, s, h * d)
    return yn * g.astype(jnp.float32) + b.astype(jnp.float32)


def _swiglu(x, w_gu, w_down):
    hdn = jnp.einsum('bsd,df->bsf', x, w_gu)
    gate, up = jnp.split(hdn, 2, axis=-1)
    return jnp.einsum('bsf,fd->bsd', jax.nn.silu(gate) * up, w_down)


def _rotary(t):
    s, d = t.shape[1], t.shape[-1]
    pos = jnp.arange(s, dtype=jnp.float32)
    inv_freq = ROPE_BASE ** (-jnp.arange(0, d, 2, dtype=jnp.float32) / d)
    ang = pos[:, None] * inv_freq[None, :]
    cos = jnp.cos(ang)[None, :, None, :]
    sin = jnp.sin(ang)[None, :, None, :]
    t1, t2 = jnp.split(t, 2, axis=-1)
    return jnp.concatenate([t1 * cos - t2 * sin, t1 * sin + t2 * cos], axis=-1)


def _retention_chunkwise(q, k, v):
    bsz, s, h, dk = q.shape
    dv = v.shape[-1]
    c = RET_CHUNK
    n = s // c
    log_gamma = jnp.log1p(-jnp.exp2(-5.0 - jnp.arange(h, dtype=jnp.float32)))
    idx = jnp.arange(c, dtype=jnp.float32)
    rel = idx[:, None] - idx[None, :]
    causal = rel >= 0
    intra = jnp.where(causal[None], jnp.exp(jnp.where(causal, rel, 0.0)[None] * log_gamma[:, None, None]), 0.0)
    q_decay = jnp.exp((idx[None, :] + 1.0) * log_gamma[:, None])
    k_decay = jnp.exp((c - 1.0 - idx[None, :]) * log_gamma[:, None])
    chunk_decay = jnp.exp(c * log_gamma)
    qc = q.reshape(bsz, n, c, h, dk)
    kc = k.reshape(bsz, n, c, h, dk)
    vc = v.reshape(bsz, n, c, h, dv)
    scores = jnp.einsum('bnihd,bnjhd->bnhij', qc, kc) * intra
    inner = jnp.einsum('bnhij,bnjhe->bnihe', scores, vc)
    kv = jnp.einsum('bnjhd,bnjhe,hj->nbhde', kc, vc, k_decay)

    def step(state, kv_n):
        return chunk_decay[None, :, None, None] * state + kv_n, state

    _, prev_states = lax.scan(step, jnp.zeros((bsz, h, dk, dv), jnp.float32), kv)
    cross = jnp.einsum('bnihd,nbhde,hi->bnihe', qc, prev_states, q_decay)
    return (inner + cross).reshape(bsz, s, h, dv)


def _rwkv7_scan(r, w, k, v, kk, a):
    bsz, s, h, nd = r.shape
    b_vec = kk * a
    xs = tuple(jnp.moveaxis(t, 1, 0) for t in (r, w, k, v, kk, b_vec))

    def step(state, inp):
        r_t, w_t, k_t, v_t, kk_t, b_t = inp
        sa = jnp.einsum('bhvk,bhk->bhv', state, kk_t)
        state = (state * w_t[:, :, None, :] - sa[..., None] * b_t[:, :, None, :]
                 + v_t[..., None] * k_t[:, :, None, :])
        return state, jnp.einsum('bhvk,bhk->bhv', state, r_t)

    _, y = lax.scan(step, jnp.zeros((bsz, h, nd, nd), jnp.float32), xs)
    return jnp.moveaxis(y, 0, 1)


def _token_mixer(hin, w_in, ret_gn_g, ret_gn_b, rw_mu, rw_w0, rw_w_up, rw_a0, rw_a_up,
                 rw_g_up, rw_k_k, rw_k_a, rw_r_k, rw_gn_g, rw_gn_b, w_out):
    bsz, s, _ = hin.shape
    z = jnp.einsum('bsd,dc->bsc', hin, w_in).astype(jnp.float32)
    z_ret, z_rw = z[..., :RET_COLS], z[..., RET_COLS:]

    q, k, v, g = jnp.split(z_ret, 4, axis=-1)
    q = _rotary(q.reshape(bsz, s, RET_HEADS, HEAD_DIM))
    k = _rotary(k.reshape(bsz, s, RET_HEADS, HEAD_DIM)) * (HEAD_DIM ** -0.5)
    v = v.reshape(bsz, s, RET_HEADS, HEAD_DIM)
    ret = _retention_chunkwise(q, k, v)
    ret_out = jax.nn.silu(g) * _head_group_norm(ret, ret_gn_g, ret_gn_b, RET_GN_EPS)

    z_prev = jnp.pad(z_rw, ((0, 0), (1, 0), (0, 0)))[:, :-1]
    z_rw = z_rw + (z_prev - z_rw) * rw_mu.astype(jnp.float32)
    r, kr, vr, wd, ad, gd = jnp.split(z_rw, RW_SPLITS, axis=-1)
    w_log = -jax.nn.softplus(-(rw_w0 + jnp.tanh(wd) @ rw_w_up)) - 0.5
    decay = jnp.exp(-jnp.exp(w_log))
    a = jax.nn.sigmoid(rw_a0 + ad @ rw_a_up)
    gate = jax.nn.sigmoid(gd) @ rw_g_up
    heads = (bsz, s, RWKV_HEADS, HEAD_DIM)
    kk = (kr * rw_k_k).reshape(heads)
    kk = kk / jnp.maximum(jnp.sqrt(jnp.sum(jnp.square(kk), axis=-1, keepdims=True)), 1e-12)
    kr = kr * (1.0 + (a - 1.0) * rw_k_a)
    r_h, k_h, v_h = r.reshape(heads), kr.reshape(heads), vr.reshape(heads)
    y = _rwkv7_scan(r_h, decay.reshape(heads), k_h, v_h, kk, a.reshape(heads))
    bonus = jnp.sum(r_h * k_h * rw_r_k.astype(jnp.float32), axis=-1, keepdims=True) * v_h
    rw_out = (_head_group_norm(y, rw_gn_g, rw_gn_b, RWKV_GN_EPS) + bonus.reshape(bsz, s, RWKV_WIDTH)) * gate

    mixed = jnp.concatenate([ret_out, rw_out], axis=-1).astype(hin.dtype)
    return jnp.einsum('bsc,cd->bsd', mixed, w_out)


def setup_inputs(seed: int = 0) -> dict:
    key = jax.random.key(seed)
    ks = jax.random.split(key, 32)
    f32 = jnp.float32

    def nrm(k_, shape, scale):
        return jax.random.normal(k_, shape, f32) * scale

    def gain(k_, shape):
        return 1.0 + 0.02 * jax.random.normal(k_, shape, f32)

    w0_base = jnp.tile(jnp.linspace(-6.5, -1.5, HEAD_DIM, dtype=f32), RWKV_HEADS)
    return {
        'x': jax.random.normal(ks[0], (BATCH, SEQ, D_MODEL), f32),
        'p': jax.random.normal(ks[1], (DEPTH, BATCH, SEQ, D_PLE), f32),
        'ffn1_w_gu': nrm(ks[2], (DEPTH, D_MODEL, 2 * D_FF), D_MODEL ** -0.5),
        'ffn1_w_down': nrm(ks[3], (DEPTH, D_FF, D_MODEL), DEEPNORM_BETA * D_FF ** -0.5),
        'ln1_g': gain(ks[4], (DEPTH, D_MODEL)),
        'ln1_b': nrm(ks[5], (DEPTH, D_MODEL), 0.02),
        'w_in': nrm(ks[6], (DEPTH, D_MODEL, IN_COLS), D_MODEL ** -0.5),
        'ret_gn_g': gain(ks[7], (DEPTH, RET_WIDTH)),
        'ret_gn_b': nrm(ks[8], (DEPTH, RET_WIDTH), 0.02),
        'rw_mu': jax.random.uniform(ks[9], (DEPTH, RW_COLS), f32),
        'rw_w0': w0_base[None] + nrm(ks[10], (DEPTH, RWKV_WIDTH), 0.1),
        'rw_w_up': nrm(ks[11], (DEPTH, DECAY_LORA, RWKV_WIDTH), DECAY_LORA ** -0.5),
        'rw_a0': nrm(ks[12], (DEPTH, RWKV_WIDTH), 0.1),
        'rw_a_up': nrm(ks[13], (DEPTH, AAA_LORA, RWKV_WIDTH), AAA_LORA ** -0.5),
        'rw_g_up': nrm(ks[14], (DEPTH, GATE_LORA, RWKV_WIDTH), GATE_LORA ** -0.5),
        'rw_k_k': 0.85 + nrm(ks[15], (DEPTH, RWKV_WIDTH), 0.02),
        'rw_k_a': 1.0 + nrm(ks[16], (DEPTH, RWKV_WIDTH), 0.02),
        'rw_r_k': nrm(ks[17], (DEPTH, RWKV_HEADS, HEAD_DIM), 0.1),
        'rw_gn_g': gain(ks[18], (DEPTH, RWKV_WIDTH)),
        'rw_gn_b': nrm(ks[19], (DEPTH, RWKV_WIDTH), 0.02),
        'w_out': nrm(ks[20], (DEPTH, MIX_WIDTH, D_MODEL), DEEPNORM_BETA * MIX_WIDTH ** -0.5),
        'ln2_g': gain(ks[21], (DEPTH, D_MODEL)),
        'ln2_b': nrm(ks[22], (DEPTH, D_MODEL), 0.02),
        'ffn2_w_gu': nrm(ks[23], (DEPTH, D_MODEL, 2 * D_FF), D_MODEL ** -0.5),
        'ffn2_w_down': nrm(ks[24], (DEPTH, D_FF, D_MODEL), DEEPNORM_BETA * D_FF ** -0.5),
        'ln3_g': gain(ks[25], (DEPTH, D_MODEL)),
        'ln3_b': nrm(ks[26], (DEPTH, D_MODEL), 0.02),
        'ple_w_proj': nrm(ks[27], (DEPTH, D_PLE, D_MODEL), D_PLE ** -0.5),
        'ple_w_gate': nrm(ks[28], (DEPTH, D_MODEL, D_MODEL), D_MODEL ** -0.5),
        'ple_b_gate': nrm(ks[29], (DEPTH, D_MODEL), 0.02),
    }


def reference(x, p, ffn1_w_gu, ffn1_w_down, ln1_g, ln1_b, w_in, ret_gn_g, ret_gn_b, rw_mu,
              rw_w0, rw_w_up, rw_a0, rw_a_up, rw_g_up, rw_k_k, rw_k_a, rw_r_k, rw_gn_g, rw_gn_b,
              w_out, ln2_g, ln2_b, ffn2_w_gu, ffn2_w_down, ln3_g, ln3_b,
              ple_w_proj, ple_w_gate, ple_b_gate):
    for i in range(DEPTH):
        x = _layer_norm(DEEPNORM_ALPHA * x + 0.5 * _swiglu(x, ffn1_w_gu[i], ffn1_w_down[i]), ln1_g[i], ln1_b[i])
        mix = _token_mixer(x, w_in[i], ret_gn_g[i], ret_gn_b[i], rw_mu[i], rw_w0[i], rw_w_up[i],
                           rw_a0[i], rw_a_up[i], rw_g_up[i], rw_k_k[i], rw_k_a[i], rw_r_k[i],
                           rw_gn_g[i], rw_gn_b[i], w_out[i])
        x = _layer_norm(DEEPNORM_ALPHA * x + mix, ln2_g[i], ln2_b[i])
        x = _layer_norm(DEEPNORM_ALPHA * x + 0.5 * _swiglu(x, ffn2_w_gu[i], ffn2_w_down[i]), ln3_g[i], ln3_b[i])
        gate = jax.nn.sigmoid(jnp.einsum('bsd,de->bse', x, ple_w_gate[i]) + ple_b_gate[i])
        x = x + gate * jnp.einsum('bsp,pd->bsd', p[i], ple_w_proj[i])
    return x
```

```python
import functools
import math

import jax
import jax.numpy as jnp
from jax import lax
from jax.experimental import pallas as pl
from jax.experimental.pallas import tpu as pltpu

D_PLE = 256
HEAD_DIM = 64
ROPE_BASE = 10000.0
DECAY_LORA = 64
AAA_LORA = 64
GATE_LORA = 160
LN_EPS = 1e-5
RET_GN_EPS = 1e-5
RWKV_GN_EPS = 64e-5
DECAY_SCALE = math.exp(-0.5)
HEAD_SHIFT = HEAD_DIM.bit_length() - 1

LANES = 128
MXU_DIM = 256
HEADS_PER_TILE = MXU_DIM // HEAD_DIM
CHUNK = 64
VMEM_LIMIT_BYTES = 56 * 1024 * 1024

F32 = jnp.float32
BF16 = jnp.bfloat16


def _dot(a, b):
    return jnp.dot(a, b, preferred_element_type=F32)


def _dot_nt(a, b):
    return lax.dot_general(a, b, (((1,), (1,)), ((), ())), preferred_element_type=F32)


def _dot_tn(a, b):
    return lax.dot_general(a, b, (((0,), (0,)), ((), ())), preferred_element_type=F32)


def _layer_norm(y, g, b):
    mu = jnp.mean(y, axis=-1, keepdims=True)
    yc = y - mu
    var = jnp.mean(yc * yc, axis=-1, keepdims=True)
    return yc * lax.rsqrt(var + LN_EPS) * g + b


def _const_spec(shape):
    nd = len(shape)
    return pl.BlockSpec(shape, lambda *_: (0,) * nd, pipeline_mode=pl.Buffered(1))


def _ffn_kernel(*refs, alpha, n_chunks, with_ple):
    if with_ple:
        (x_ref, wg_ref, wu_ref, wd_ref, g_ref, b_ref, p_ref, wp_ref, wpg_ref, bpg_ref,
         o_ref, acc_ref) = refs
    else:
        x_ref, wg_ref, wu_ref, wd_ref, g_ref, b_ref, o_ref, acc_ref = refs
    xb = x_ref[...].astype(BF16)
    acc_ref[...] = jnp.zeros_like(acc_ref)

    def chunk(c, carry):
        hg = _dot(xb, wg_ref[c])
        hu = _dot(xb, wu_ref[c])
        act = (hg * jax.nn.sigmoid(hg) * hu).astype(BF16)
        acc_ref[...] += _dot(act, wd_ref[c])
        return carry

    lax.fori_loop(0, n_chunks, chunk, 0)
    y = _layer_norm(alpha * x_ref[...] + 0.5 * acc_ref[...], g_ref[...], b_ref[...])
    if with_ple:
        gate = jax.nn.sigmoid(_dot(y.astype(BF16), wpg_ref[...]) + bpg_ref[...])
        y = y + gate * _dot(p_ref[...].astype(BF16), wp_ref[...])
    o_ref[...] = y


def _ffn_call(x2d, wg3, wu3, wd3, ln_g, ln_b, alpha, ple=None, tm=512):
    n, d = x2d.shape
    n_chunks, _, fc = wg3.shape
    tm = min(tm, n)
    row = lambda i: (i, 0)
    in_specs = [pl.BlockSpec((tm, d), row), _const_spec(wg3.shape), _const_spec(wu3.shape),
                _const_spec(wd3.shape), _const_spec((1, d)), _const_spec((1, d))]
    args = [x2d, wg3, wu3, wd3, ln_g, ln_b]
    if ple is not None:
        p2d, wp, wpg, bpg = ple
        in_specs += [pl.BlockSpec((tm, p2d.shape[1]), row), _const_spec(wp.shape),
                     _const_spec(wpg.shape), _const_spec((1, d))]
        args += [p2d, wp, wpg, bpg]
    return pl.pallas_call(
        functools.partial(_ffn_kernel, alpha=alpha, n_chunks=n_chunks, with_ple=ple is not None),
        grid=(n // tm,),
        in_specs=in_specs,
        out_specs=pl.BlockSpec((tm, d), row),
        out_shape=jax.ShapeDtypeStruct((n, d), F32),
        scratch_shapes=[pltpu.VMEM((tm, d), F32)],
        compiler_params=pltpu.CompilerParams(dimension_semantics=("arbitrary",),
                                             vmem_limit_bytes=VMEM_LIMIT_BYTES),
        name="ffn_ln",
    )(*args)


def _inproj_kernel(x_ref, wret_ref, wrw_ref, wlora_ref, cos_ref, sin_ref, mu_ref, w0_ref, a0_ref,
                   kk_ref, ka_ref,
                   q_ref, k_ref, v_ref, g_ref, r_ref, km_ref, vr_ref, lw_ref, a_ref, kku_ref,
                   gate_ref, carry_ref, *, width):
    j = pl.program_id(1)
    tm = x_ref.shape[0]
    xb = x_ref[...].astype(BF16)

    lane = lax.broadcasted_iota(jnp.int32, (tm, width), 1)
    first_half = (lane & (HEAD_DIM // 2)) == 0
    reps = width // cos_ref.shape[1]
    cos = jnp.concatenate([cos_ref[...]] * reps, axis=1)
    sin = jnp.concatenate([sin_ref[...]] * reps, axis=1)

    def rotary(t):
        swapped = jnp.where(first_half, pltpu.roll(t, width - HEAD_DIM // 2, 1),
                            pltpu.roll(t, HEAD_DIM // 2, 1))
        return t * cos + swapped * sin

    q_ref[...] = rotary(_dot(xb, wret_ref[:, 0 * width:1 * width]))
    k_ref[...] = rotary(_dot(xb, wret_ref[:, 1 * width:2 * width])) * (HEAD_DIM ** -0.5)
    v_ref[...] = _dot(xb, wret_ref[:, 2 * width:3 * width])
    g_ref[...] = _dot(xb, wret_ref[:, 3 * width:4 * width])

    z = _dot(xb, wrw_ref[...])

    @pl.when(j == 0)
    def _():
        carry_ref[...] = jnp.zeros_like(carry_ref)

    rowid = lax.broadcasted_iota(jnp.int32, z.shape, 0)
    prev = jnp.where(rowid == 0, carry_ref[0:1, :], pltpu.roll(z, 1, 0))
    carry_ref[0:1, :] = z[tm - 1:tm, :]
    sh = z + (prev - z) * mu_ref[...]

    r = sh[:, 0 * width:1 * width]
    kr = sh[:, 1 * width:2 * width]
    vr = sh[:, 2 * width:3 * width]
    lo = sh[:, 3 * width:]
    ll = lax.broadcasted_iota(jnp.int32, lo.shape, 1)
    act = jnp.where(ll < DECAY_LORA, jnp.tanh(lo),
                    jnp.where(ll < DECAY_LORA + AAA_LORA, lo, jax.nn.sigmoid(lo)))
    up = _dot(act.astype(BF16), wlora_ref[...])
    lw_ref[...] = -DECAY_SCALE * jax.nn.sigmoid(w0_ref[...] + up[:, 0 * width:1 * width])
    a = jax.nn.sigmoid(a0_ref[...] + up[:, 1 * width:2 * width])
    gate_ref[...] = up[:, 2 * width:3 * width]
    a_ref[...] = a
    r_ref[...] = r
    vr_ref[...] = vr
    kku_ref[...] = kr * kk_ref[...]
    km_ref[...] = kr * (1.0 + (a - 1.0) * ka_ref[...])


def _inproj_call(x3d, wret, wrw, wlora, cos, sin, mu, w0, a0, k_k, k_a, tm=256):
    bsz, s, d = x3d.shape
    width = wret.shape[1] // 4
    tm = min(tm, s)
    blk = lambda b, j: (b, j, 0)
    out_spec = pl.BlockSpec((None, tm, width), blk)
    out_sds = jax.ShapeDtypeStruct((bsz, s, width), F32)
    n_out = 11
    return pl.pallas_call(
        functools.partial(_inproj_kernel, width=width),
        grid=(bsz, s // tm),
        in_specs=[pl.BlockSpec((None, tm, d), blk), _const_spec(wret.shape), _const_spec(wrw.shape),
                  _const_spec(wlora.shape),
                  pl.BlockSpec((tm, cos.shape[1]), lambda b, j: (j, 0)),
                  pl.BlockSpec((tm, sin.shape[1]), lambda b, j: (j, 0)),
                  _const_spec(mu.shape), _const_spec(w0.shape), _const_spec(a0.shape),
                  _const_spec(k_k.shape), _const_spec(k_a.shape)],
        out_specs=[out_spec] * n_out,
        out_shape=[out_sds] * n_out,
        scratch_shapes=[pltpu.VMEM((8, wrw.shape[1]), F32)],
        compiler_params=pltpu.CompilerParams(dimension_semantics=("arbitrary", "arbitrary"),
                                             vmem_limit_bytes=VMEM_LIMIT_BYTES),
        name="inproj_prep",
    )(x3d, wret, wrw, wlora, cos, sin, mu, w0, a0, k_k, k_a)


def _split_hi_lo(x):
    hi = x.astype(BF16)
    lo = (x - hi.astype(F32)).astype(BF16)
    return hi, lo


def _mix_kernel(q_ref, k_ref, v_ref, g_ref, r_ref, km_ref, vr_ref, lw_ref, a_ref, kku_ref, gate_ref,
                dmask_ref, qdec_ref, kdec_ref, cdec_ref, rgn_g_ref, rgn_b_ref, wgn_g_ref, wgn_b_ref,
                rk_ref, o_ref,
                sret_ref, srw_ref, kkn_ref, p_ref, yret_ref, yrw_ref, *, n_sub):
    width = q_ref.shape[1]
    n_tiles = width // MXU_DIM
    hpt = HEADS_PER_TILE

    @pl.when(pl.program_id(1) == 0)
    def _():
        sret_ref[...] = jnp.zeros_like(sret_ref)
        srw_ref[...] = jnp.zeros_like(srw_ref)

    ri = lax.broadcasted_iota(jnp.int32, (MXU_DIM, MXU_DIM), 0)
    ci = lax.broadcasted_iota(jnp.int32, (MXU_DIM, MXU_DIM), 1)
    same_head = (ri >> HEAD_SHIFT) == (ci >> HEAD_SHIFT)
    strict_lower = same_head & ((ri & (CHUNK - 1)) > (ci & (CHUNK - 1)))
    incl_lower = same_head & ((ri & (CHUNK - 1)) >= (ci & (CHUNK - 1)))
    eye = (ri == ci).astype(F32)
    ones_blk = same_head.astype(BF16)
    lane_head = lax.broadcasted_iota(jnp.int32, (CHUNK, MXU_DIM), 1) >> HEAD_SHIFT
    ti = lax.broadcasted_iota(jnp.int32, (CHUNK, CHUNK), 0)
    tj = lax.broadcasted_iota(jnp.int32, (CHUNK, CHUNK), 1)
    tri = (ti >= tj).astype(BF16)

    def tile_rows(x):
        return jnp.concatenate([x] * hpt, axis=0)

    def stack_own(x):
        return jnp.where(same_head, tile_rows(x), 0.0)

    def unstack(xs):
        out = jnp.where(lane_head == 0, xs[0:CHUNK], 0.0)
        for h in range(1, hpt):
            out = out + jnp.where(lane_head == h, xs[h * CHUNK:(h + 1) * CHUNK], 0.0)
        return out

    def group_sum(x):
        outs = []
        for t in range(n_tiles):
            hi, lo = _split_hi_lo(x[:, t * MXU_DIM:(t + 1) * MXU_DIM])
            outs.append(_dot(hi, ones_blk) + _dot(lo, ones_blk))
        return jnp.concatenate(outs, axis=1)

    kku = kku_ref[...]
    norm = jnp.maximum(jnp.sqrt(group_sum(kku * kku)), 1e-12)
    kkn = kku / norm
    kkn_ref[...] = kkn
    p_ref[...] = -(kkn * a_ref[...])

    def sub_chunk(sc, carry):
        rows = pl.ds(pl.multiple_of(sc * CHUNK, CHUNK), CHUNK)
        lw = lw_ref[rows, :]
        lw_hi, lw_lo = _split_hi_lo(lw)
        cum = _dot(tri, lw_hi) + _dot(tri, lw_lo)
        e_in = jnp.exp(cum)
        e_ex = jnp.exp(cum - lw)
        e_neg = jnp.exp(-cum)
        e_last = e_in[CHUNK - 1:CHUNK, :]
        qt_all = kkn_ref[rows, :] * e_ex
        rt_all = r_ref[rows, :] * e_in
        pt_all = p_ref[rows, :] * e_neg
        kt_all = km_ref[rows, :] * e_neg
        ph_all = pt_all * e_last
        kh_all = kt_all * e_last
        vr_all = vr_ref[rows, :]
        q_all = q_ref[rows, :]
        k_all = k_ref[rows, :]
        v_all = v_ref[rows, :]
        qd_all = q_all * qdec_ref[...]
        kd_all = k_all * kdec_ref[...]

        for t in range(n_tiles):
            sl = slice(t * MXU_DIM, (t + 1) * MXU_DIM)
            qs = stack_own(qt_all[:, sl]).astype(BF16)
            rs = stack_own(rt_all[:, sl]).astype(BF16)
            pt = tile_rows(pt_all[:, sl]).astype(BF16)
            kt = tile_rows(kt_all[:, sl]).astype(BF16)
            a_qp = jnp.where(strict_lower, _dot_nt(qs, pt), 0.0)
            a_qk = jnp.where(strict_lower, _dot_nt(qs, kt), 0.0)
            b_rp = jnp.where(incl_lower, _dot_nt(rs, pt), 0.0)
            b_rk = jnp.where(incl_lower, _dot_nt(rs, kt), 0.0)
            tinv = a_qp + eye
            pw = a_qp
            for _ in range(5):
                pwb = pw.astype(BF16)
                pw = _dot(pwb, pwb)
                tinv = tinv + _dot(tinv.astype(BF16), pw.astype(BF16))
            st = srw_ref[t]
            stb = st.astype(BF16)
            qr_flat = jnp.concatenate([qt_all[:, sl], rt_all[:, sl]], axis=0).astype(BF16)
            qrm = _dot_nt(qr_flat, stb)
            vt = tile_rows(vr_all[:, sl]).astype(BF16)
            rhs = tile_rows(qrm[0:CHUNK]) + _dot(a_qk.astype(BF16), vt)
            u = _dot(tinv.astype(BF16), rhs.astype(BF16))
            y = tile_rows(qrm[CHUNK:]) + _dot(b_rk.astype(BF16), vt) + _dot(b_rp.astype(BF16), u.astype(BF16))
            u_flat = unstack(u)
            yrw_ref[rows, sl] = unstack(y)
            uv = jnp.concatenate([u_flat, vr_all[:, sl]], axis=0).astype(BF16)
            pk = jnp.concatenate([ph_all[:, sl], kh_all[:, sl]], axis=0).astype(BF16)
            srw_ref[t] = st * e_last[:, sl] + jnp.where(same_head, _dot_tn(uv, pk), 0.0)

            qs_r = stack_own(q_all[:, sl]).astype(BF16)
            kt_r = tile_rows(k_all[:, sl]).astype(BF16)
            scores = _dot_nt(qs_r, kt_r) * dmask_ref[t]
            inner = _dot(scores.astype(BF16), tile_rows(v_all[:, sl]).astype(BF16))
            sr = sret_ref[t]
            cross = _dot_nt(qd_all[:, sl].astype(BF16), sr.astype(BF16))
            yret_ref[rows, sl] = unstack(inner) + cross
            kv = _dot_tn(v_all[:, sl].astype(BF16), kd_all[:, sl].astype(BF16))
            sret_ref[t] = sr * cdec_ref[:, sl] + jnp.where(same_head, kv, 0.0)
        return carry

    lax.fori_loop(0, n_sub, sub_chunk, 0)

    inv_hd = 1.0 / HEAD_DIM

    def group_norm(y, eps):
        mu = group_sum(y) * inv_hd
        yc = y - mu
        var = group_sum(yc * yc) * inv_hd
        return yc * lax.rsqrt(var + eps)

    g = g_ref[...]
    ret_out = g * jax.nn.sigmoid(g) * (group_norm(yret_ref[...], RET_GN_EPS) * rgn_g_ref[...] + rgn_b_ref[...])
    o_ref[:, 0:width] = ret_out.astype(o_ref.dtype)
    bonus = group_sum(r_ref[...] * km_ref[...] * rk_ref[...]) * vr_ref[...]
    rw_out = (group_norm(yrw_ref[...], RWKV_GN_EPS) * wgn_g_ref[...] + wgn_b_ref[...] + bonus) * gate_ref[...]
    o_ref[:, width:2 * width] = rw_out.astype(o_ref.dtype)


def _mix_call(streams, dmask, qdec, kdec, cdec, rgn_g, rgn_b, wgn_g, wgn_b, rk, tc=256):
    bsz, s, width = streams[0].shape
    tc = min(tc, s)
    n_tiles = width // MXU_DIM
    blk = lambda b, j: (b, j, 0)
    stream_spec = pl.BlockSpec((None, tc, width), blk)
    consts = [dmask, qdec, kdec, cdec, rgn_g, rgn_b, wgn_g, wgn_b, rk]
    return pl.pallas_call(
        functools.partial(_mix_kernel, n_sub=tc // CHUNK),
        grid=(bsz, s // tc),
        in_specs=[stream_spec] * len(streams) + [_const_spec(c.shape) for c in consts],
        out_specs=pl.BlockSpec((None, tc, 2 * width), blk),
        out_shape=jax.ShapeDtypeStruct((bsz, s, 2 * width), BF16),
        scratch_shapes=[pltpu.VMEM((n_tiles, MXU_DIM, MXU_DIM), F32),
                        pltpu.VMEM((n_tiles, MXU_DIM, MXU_DIM), F32),
                        pltpu.VMEM((tc, width), F32), pltpu.VMEM((tc, width), F32),
                        pltpu.VMEM((tc, width), F32), pltpu.VMEM((tc, width), F32)],
        compiler_params=pltpu.CompilerParams(dimension_semantics=("arbitrary", "arbitrary"),
                                             vmem_limit_bytes=VMEM_LIMIT_BYTES),
        name="mix_recurrences",
    )(*streams, *consts)


def _outproj_kernel(m_ref, x_ref, w_ref, g_ref, b_ref, o_ref, *, alpha):
    y = alpha * x_ref[...] + _dot(m_ref[...], w_ref[...])
    o_ref[...] = _layer_norm(y, g_ref[...], b_ref[...])


def _outproj_call(m2d, x2d, w, ln_g, ln_b, alpha, tm=512):
    n, d = x2d.shape
    tm = min(tm, n)
    row = lambda i: (i, 0)
    return pl.pallas_call(
        functools.partial(_outproj_kernel, alpha=alpha),
        grid=(n // tm,),
        in_specs=[pl.BlockSpec((tm, m2d.shape[1]), row), pl.BlockSpec((tm, d), row),
                  _const_spec(w.shape), _const_spec((1, d)), _const_spec((1, d))],
        out_specs=pl.BlockSpec((tm, d), row),
        out_shape=jax.ShapeDtypeStruct((n, d), F32),
        compiler_params=pltpu.CompilerParams(dimension_semantics=("arbitrary",),
                                             vmem_limit_bytes=VMEM_LIMIT_BYTES),
        name="outproj_ln",
    )(m2d, x2d, w, ln_g, ln_b)


def _ffn_weights(w_gu, w_down, fc=MXU_DIM):
    d, two_f = w_gu.shape
    d_ff = two_f // 2
    n_chunks = d_ff // fc
    wg3 = w_gu[:, :d_ff].astype(BF16).reshape(d, n_chunks, fc).transpose(1, 0, 2)
    wu3 = w_gu[:, d_ff:].astype(BF16).reshape(d, n_chunks, fc).transpose(1, 0, 2)
    wd3 = w_down.astype(BF16).reshape(n_chunks, fc, d)
    return wg3, wu3, wd3


def _retention_tables(n_heads):
    h = jnp.arange(n_heads, dtype=F32)
    log_gamma = jnp.log1p(-jnp.exp2(-5.0 - h))
    lg_lane = jnp.repeat(log_gamma, HEAD_DIM)[None, :]
    idx = jnp.arange(CHUNK, dtype=F32)[:, None]
    qdec = jnp.exp((idx + 1.0) * lg_lane)
    kdec = jnp.exp((CHUNK - 1.0 - idx) * lg_lane)
    cdec = jnp.exp(CHUNK * lg_lane)
    n_tiles = n_heads // HEADS_PER_TILE
    r = jnp.arange(MXU_DIM)
    rel = ((r % CHUNK)[:, None] - (r % CHUNK)[None, :]).astype(F32)
    same = (r // HEAD_DIM)[:, None] == (r // HEAD_DIM)[None, :]
    lg_rows = log_gamma.reshape(n_tiles, HEADS_PER_TILE)
    lg_rows = jnp.repeat(lg_rows, HEAD_DIM, axis=1)[:, :, None]
    dmask = jnp.where((same & (rel >= 0))[None], jnp.exp(jnp.where(rel >= 0, rel, 0.0)[None] * lg_rows), 0.0)
    return dmask, qdec, kdec, cdec


def _rotary_tables(s):
    pos = jnp.arange(s, dtype=F32)
    inv_freq = ROPE_BASE ** (-jnp.arange(0, HEAD_DIM, 2, dtype=F32) / HEAD_DIM)
    ang = pos[:, None] * inv_freq[None, :]
    cos, sin = jnp.cos(ang), jnp.sin(ang)
    reps = LANES // HEAD_DIM
    cos_t = jnp.tile(jnp.concatenate([cos, cos], axis=1), (1, reps))
    sin_t = jnp.tile(jnp.concatenate([-sin, sin], axis=1), (1, reps))
    return cos_t, sin_t


def kernel(x, p, ffn1_w_gu, ffn1_w_down, ln1_g, ln1_b, w_in, ret_gn_g, ret_gn_b, rw_mu, rw_w0, rw_w_up, rw_a0, rw_a_up, rw_g_up, rw_k_k, rw_k_a, rw_r_k, rw_gn_g, rw_gn_b, w_out, ln2_g, ln2_b, ffn2_w_gu, ffn2_w_down, ln3_g, ln3_b, ple_w_proj, ple_w_gate, ple_b_gate):
    bsz, s, d = x.shape
    depth = ffn1_w_gu.shape[0]
    alpha = (2.0 * depth) ** 0.25
    width = rw_w0.shape[1]
    n_heads = width // HEAD_DIM
    ret_cols = 4 * width
    lora = DECAY_LORA + AAA_LORA + GATE_LORA
    lora_pad = -(-lora // LANES) * LANES

    dmask, qdec, kdec, cdec = _retention_tables(n_heads)
    cos_t, sin_t = _rotary_tables(s)
    row = lambda v: v.reshape(1, -1)

    h = x.reshape(bsz * s, d)
    for i in range(depth):
        wg3, wu3, wd3 = _ffn_weights(ffn1_w_gu[i], ffn1_w_down[i])
        h = _ffn_call(h, wg3, wu3, wd3, row(ln1_g[i]), row(ln1_b[i]), alpha)

        wi = w_in[i].astype(BF16)
        wret = wi[:, :ret_cols]
        wrw = jnp.pad(wi[:, ret_cols:], ((0, 0), (0, lora_pad - lora)))
        mu = row(jnp.pad(rw_mu[i], (0, lora_pad - lora)))
        wlora = jnp.zeros((lora_pad, 3 * width), BF16)
        wlora = wlora.at[:DECAY_LORA, :width].set(rw_w_up[i].astype(BF16))
        wlora = wlora.at[DECAY_LORA:DECAY_LORA + AAA_LORA, width:2 * width].set(rw_a_up[i].astype(BF16))
        wlora = wlora.at[DECAY_LORA + AAA_LORA:lora, 2 * width:].set(rw_g_up[i].astype(BF16))
        streams = _inproj_call(h.reshape(bsz, s, d), wret, wrw, wlora, cos_t, sin_t, mu,
                               row(rw_w0[i]), row(rw_a0[i]), row(rw_k_k[i]), row(rw_k_a[i]))
        mixed = _mix_call(streams, dmask, qdec, kdec, cdec, row(ret_gn_g[i]), row(ret_gn_b[i]),
                          row(rw_gn_g[i]), row(rw_gn_b[i]), row(rw_r_k[i]))
        h = _outproj_call(mixed.reshape(bsz * s, 2 * width), h, w_out[i].astype(BF16),
                          row(ln2_g[i]), row(ln2_b[i]), alpha)

        wg3, wu3, wd3 = _ffn_weights(ffn2_w_gu[i], ffn2_w_down[i])
        ple = (p[i].reshape(bsz * s, -1), ple_w_proj[i].astype(BF16), ple_w_gate[i].astype(BF16),
               row(ple_b_gate[i]))
        h = _ffn_call(h, wg3, wu3, wd3, row(ln3_g[i]), row(ln3_b[i]), alpha, ple=ple)
    return h.reshape(bsz, s, d)
```

```python
import functools
import math

import jax
import jax.numpy as jnp
from jax import lax
from jax.experimental import pallas as pl
from jax.experimental.pallas import tpu as pltpu

D_PLE = 256
HEAD_DIM = 64
ROPE_BASE = 10000.0
DECAY_LORA = 64
AAA_LORA = 64
GATE_LORA = 160
LN_EPS = 1e-5
RET_GN_EPS = 1e-5
RWKV_GN_EPS = 64e-5
DECAY_SCALE = math.exp(-0.5)
HEAD_SHIFT = HEAD_DIM.bit_length() - 1

LANES = 128
MXU_DIM = 256
HEADS_PER_TILE = MXU_DIM // HEAD_DIM
CHUNK = 64
VMEM_LIMIT_BYTES = 56 * 1024 * 1024

F32 = jnp.float32
BF16 = jnp.bfloat16


def _dot(a, b):
    return jnp.dot(a, b, preferred_element_type=F32)


def _dot_nt(a, b):
    return lax.dot_general(a, b, (((1,), (1,)), ((), ())), preferred_element_type=F32)


def _dot_tn(a, b):
    return lax.dot_general(a, b, (((0,), (0,)), ((), ())), preferred_element_type=F32)


def _layer_norm(y, g, b):
    mu = jnp.mean(y, axis=-1, keepdims=True)
    yc = y - mu
    var = jnp.mean(yc * yc, axis=-1, keepdims=True)
    return yc * lax.rsqrt(var + LN_EPS) * g + b


def _const_spec(shape):
    nd = len(shape)
    return pl.BlockSpec(shape, lambda *_: (0,) * nd, pipeline_mode=pl.Buffered(1))


def _ffn_kernel(*refs, alpha, n_chunks, with_ple):
    if with_ple:
        (x_ref, wg_ref, wu_ref, wd_ref, g_ref, b_ref, p_ref, wp_ref, wpg_ref, bpg_ref,
         o_ref, acc_ref) = refs
    else:
        x_ref, wg_ref, wu_ref, wd_ref, g_ref, b_ref, o_ref, acc_ref = refs
    xb = x_ref[...].astype(BF16)
    acc_ref[...] = jnp.zeros_like(acc_ref)

    def chunk(c, carry):
        hg = _dot(xb, wg_ref[c])
        hu = _dot(xb, wu_ref[c])
        act = (hg * jax.nn.sigmoid(hg) * hu).astype(BF16)
        acc_ref[...] += _dot(act, wd_ref[c])
        return carry

    lax.fori_loop(0, n_chunks, chunk, 0)
    y = _layer_norm(alpha * x_ref[...] + 0.5 * acc_ref[...], g_ref[...], b_ref[...])
    if with_ple:
        gate = jax.nn.sigmoid(_dot(y.astype(BF16), wpg_ref[...]) + bpg_ref[...])
        y = y + gate * _dot(p_ref[...].astype(BF16), wp_ref[...])
    o_ref[...] = y


def _ffn_call(x2d, wg3, wu3, wd3, ln_g, ln_b, alpha, ple=None, tm=512):
    n, d = x2d.shape
    n_chunks, _, fc = wg3.shape
    tm = min(tm, n)
    row = lambda i: (i, 0)
    in_specs = [pl.BlockSpec((tm, d), row), _const_spec(wg3.shape), _const_spec(wu3.shape),
                _const_spec(wd3.shape), _const_spec((1, d)), _const_spec((1, d))]
    args = [x2d, wg3, wu3, wd3, ln_g, ln_b]
    if ple is not None:
        p2d, wp, wpg, bpg = ple
        in_specs += [pl.BlockSpec((tm, p2d.shape[1]), row), _const_spec(wp.shape),
                     _const_spec(wpg.shape), _const_spec((1, d))]
        args += [p2d, wp, wpg, bpg]
    return pl.pallas_call(
        functools.partial(_ffn_kernel, alpha=alpha, n_chunks=n_chunks, with_ple=ple is not None),
        grid=(n // tm,),
        in_specs=in_specs,
        out_specs=pl.BlockSpec((tm, d), row),
        out_shape=jax.ShapeDtypeStruct((n, d), F32),
        scratch_shapes=[pltpu.VMEM((tm, d), F32)],
        compiler_params=pltpu.CompilerParams(dimension_semantics=("arbitrary",),
                                             vmem_limit_bytes=VMEM_LIMIT_BYTES),
        name="ffn_ln",
    )(*args)


def _inproj_kernel(x_ref, wret_ref, wrw_ref, wlora_ref, cos_ref, sin_ref, mu_ref, w0_ref, a0_ref,
                   kk_ref, ka_ref,
                   q_ref, k_ref, v_ref, g_ref, r_ref, km_ref, vr_ref, lw_ref, a_ref, kku_ref,
                   gate_ref, carry_ref, *, width):
    j = pl.program_id(1)
    tm = x_ref.shape[0]
    xb = x_ref[...].astype(BF16)

    lane = lax.broadcasted_iota(jnp.int32, (tm, width), 1)
    first_half = (lane & (HEAD_DIM // 2)) == 0
    reps = width // cos_ref.shape[1]
    cos = jnp.concatenate([cos_ref[...]] * reps, axis=1)
    sin = jnp.concatenate([sin_ref[...]] * reps, axis=1)

    def rotary(t):
        swapped = jnp.where(first_half, pltpu.roll(t, width - HEAD_DIM // 2, 1),
                            pltpu.roll(t, HEAD_DIM // 2, 1))
        return t * cos + swapped * sin

    q_ref[...] = rotary(_dot(xb, wret_ref[:, 0 * width:1 * width]))
    k_ref[...] = rotary(_dot(xb, wret_ref[:, 1 * width:2 * width])) * (HEAD_DIM ** -0.5)
    v_ref[...] = _dot(xb, wret_ref[:, 2 * width:3 * width])
    g_ref[...] = _dot(xb, wret_ref[:, 3 * width:4 * width])

    z = _dot(xb, wrw_ref[...])

    @pl.when(j == 0)
    def _():
        carry_ref[...] = jnp.zeros_like(carry_ref)

    rowid = lax.broadcasted_iota(jnp.int32, z.shape, 0)
    prev = jnp.where(rowid == 0, carry_ref[0:1, :], pltpu.roll(z, 1, 0))
    carry_ref[0:1, :] = z[tm - 1:tm, :]
    sh = z + (prev - z) * mu_ref[...]

    r = sh[:, 0 * width:1 * width]
    kr = sh[:, 1 * width:2 * width]
    vr = sh[:, 2 * width:3 * width]
    lo = sh[:, 3 * width:]
    ll = lax.broadcasted_iota(jnp.int32, lo.shape, 1)
    act = jnp.where(ll < DECAY_LORA, jnp.tanh(lo),
                    jnp.where(ll < DECAY_LORA + AAA_LORA, lo, jax.nn.sigmoid(lo)))
    up = _dot(act.astype(BF16), wlora_ref[...])
    lw_ref[...] = -DECAY_SCALE * jax.nn.sigmoid(w0_ref[...] + up[:, 0 * width:1 * width])
    a = jax.nn.sigmoid(a0_ref[...] + up[:, 1 * width:2 * width])
    gate_ref[...] = up[:, 2 * width:3 * width]
    a_ref[...] = a
    r_ref[...] = r
    vr_ref[...] = vr
    kku_ref[...] = kr * kk_ref[...]
    km_ref[...] = kr * (1.0 + (a - 1.0) * ka_ref[...])


def _inproj_call(x3d, wret, wrw, wlora, cos, sin, mu, w0, a0, k_k, k_a, tm=256):
    bsz, s, d = x3d.shape
    width = wret.shape[1] // 4
    tm = min(tm, s)
    blk = lambda b, j: (b, j, 0)
    out_spec = pl.BlockSpec((None, tm, width), blk)
    out_sds = jax.ShapeDtypeStruct((bsz, s, width), F32)
    n_out = 11
    return pl.pallas_call(
        functools.partial(_inproj_kernel, width=width),
        grid=(bsz, s // tm),
        in_specs=[pl.BlockSpec((None, tm, d), blk), _const_spec(wret.shape), _const_spec(wrw.shape),
                  _const_spec(wlora.shape),
                  pl.BlockSpec((tm, cos.shape[1]), lambda b, j: (j, 0)),
                  pl.BlockSpec((tm, sin.shape[1]), lambda b, j: (j, 0)),
                  _const_spec(mu.shape), _const_spec(w0.shape), _const_spec(a0.shape),
                  _const_spec(k_k.shape), _const_spec(k_a.shape)],
        out_specs=[out_spec] * n_out,
        out_shape=[out_sds] * n_out,
        scratch_shapes=[pltpu.VMEM((8, wrw.shape[1]), F32)],
        compiler_params=pltpu.CompilerParams(dimension_semantics=("arbitrary", "arbitrary"),
                                             vmem_limit_bytes=VMEM_LIMIT_BYTES),
        name="inproj_prep",
    )(x3d, wret, wrw, wlora, cos, sin, mu, w0, a0, k_k, k_a)


def _split_hi_lo(x):
    hi = x.astype(BF16)
    lo = (x - hi.astype(F32)).astype(BF16)
    return hi, lo


def _mix_kernel(q_ref, k_ref, v_ref, g_ref, r_ref, km_ref, vr_ref, lw_ref, a_ref, kku_ref, gate_ref,
                dmask_ref, qdec_ref, kdec_ref, cdec_ref, rgn_g_ref, rgn_b_ref, wgn_g_ref, wgn_b_ref,
                rk_ref, o_ref,
                sret_ref, srw_ref, kkn_ref, p_ref, yret_ref, yrw_ref,
                pw_ref, tinv_ref, brp_ref, aqkv_ref, brkv_ref, pht_ref, psi_ref, dec_ref, qr_ref,
                inner_ref, qd_ref, kv_ref):
    bsz, _, width = q_ref.shape
    n_tiles = width // MXU_DIM
    n_chains = bsz * n_tiles
    hpt = HEADS_PER_TILE
    n_rows = bsz * CHUNK

    def full(ref):
        return ref[...].reshape(n_rows, ref.shape[-1])

    @pl.when(pl.program_id(0) == 0)
    def _():
        sret_ref[...] = jnp.zeros_like(sret_ref)
        srw_ref[...] = jnp.zeros_like(srw_ref)

    ri = lax.broadcasted_iota(jnp.int32, (MXU_DIM, MXU_DIM), 0)
    ci = lax.broadcasted_iota(jnp.int32, (MXU_DIM, MXU_DIM), 1)
    same_head = (ri >> HEAD_SHIFT) == (ci >> HEAD_SHIFT)
    strict_lower = same_head & ((ri & (CHUNK - 1)) > (ci & (CHUNK - 1)))
    incl_lower = same_head & ((ri & (CHUNK - 1)) >= (ci & (CHUNK - 1)))
    eye = ri == ci
    one_b = jnp.ones((MXU_DIM, MXU_DIM), BF16)
    zero_b = jnp.zeros((MXU_DIM, MXU_DIM), BF16)
    ones_blk = same_head.astype(BF16)
    lane_head = lax.broadcasted_iota(jnp.int32, (CHUNK, MXU_DIM), 1) >> HEAD_SHIFT
    ti = lax.broadcasted_iota(jnp.int32, (CHUNK, CHUNK), 0)
    tj = lax.broadcasted_iota(jnp.int32, (CHUNK, CHUNK), 1)
    tri = (ti >= tj).astype(BF16)

    def tile_rows(x):
        return jnp.concatenate([x] * hpt, axis=0)

    def stack_own(x):
        return jnp.where(same_head, tile_rows(x), jnp.zeros((), x.dtype))

    def unstack(xs):
        out = jnp.where(lane_head == 0, xs[0:CHUNK], 0.0)
        for h in range(1, hpt):
            out = out + jnp.where(lane_head == h, xs[h * CHUNK:(h + 1) * CHUNK], 0.0)
        return out

    def group_sum(x):
        outs = []
        for t in range(n_tiles):
            hi, lo = _split_hi_lo(x[:, t * MXU_DIM:(t + 1) * MXU_DIM])
            outs.append(_dot(hi, ones_blk) + _dot(lo, ones_blk))
        return jnp.concatenate(outs, axis=1)

    kku = full(kku_ref)
    norm = jnp.maximum(jnp.sqrt(group_sum(kku * kku)), 1e-12)
    kkn = kku / norm
    kkn_ref[...] = kkn
    p_ref[...] = -(kkn * full(a_ref))

    def prepare(b, carry):
        rows = pl.ds(pl.multiple_of(b * CHUNK, CHUNK), CHUNK)
        lw = lw_ref[b]
        lw_hi, lw_lo = _split_hi_lo(lw)
        cum = _dot(tri, lw_hi) + _dot(tri, lw_lo)
        e_in = jnp.exp(cum)
        e_ex = jnp.exp(cum - lw)
        e_neg = jnp.exp(-cum)
        e_last = e_in[CHUNK - 1:CHUNK, :]
        qt_all = kkn_ref[rows, :] * e_ex
        rt_all = r_ref[b] * e_in
        pt_all = p_ref[rows, :] * e_neg
        kt_all = km_ref[b] * e_neg
        ph_all = pt_all * e_last
        kh_all = (kt_all * e_last).astype(BF16)
        vr_all = vr_ref[b].astype(BF16)
        q_all = q_ref[b]
        k_all = k_ref[b]
        v_all = v_ref[b].astype(BF16)
        qd_all = (q_all * qdec_ref[...]).astype(BF16)
        kd_all = (k_all * kdec_ref[...]).astype(BF16)
        qt_b, rt_b = qt_all.astype(BF16), rt_all.astype(BF16)
        pt_b, kt_b = pt_all.astype(BF16), kt_all.astype(BF16)
        q_b, k_b = q_all.astype(BF16), k_all.astype(BF16)

        for t in range(n_tiles):
            c = b * n_tiles + t
            sl = slice(t * MXU_DIM, (t + 1) * MXU_DIM)
            qs = stack_own(qt_b[:, sl])
            rs = stack_own(rt_b[:, sl])
            pt = tile_rows(pt_b[:, sl])
            kt = tile_rows(kt_b[:, sl])
            vt = tile_rows(vr_all[:, sl])
            a_qp = jnp.where(strict_lower, _dot_nt(qs, pt), 0.0).astype(BF16)
            pw_ref[c] = a_qp
            tinv_ref[c] = jnp.where(eye, one_b, a_qp)
            a_qk = jnp.where(strict_lower, _dot_nt(qs, kt), 0.0).astype(BF16)
            aqkv_ref[c] = _dot(a_qk, vt)
            b_rk = jnp.where(incl_lower, _dot_nt(rs, kt), 0.0).astype(BF16)
            brkv_ref[c] = unstack(_dot(b_rk, vt))
            brp_ref[c] = jnp.where(incl_lower, _dot_nt(rs, pt), 0.0).astype(BF16)
            pht_ref[c] = jnp.transpose(stack_own(ph_all[:, sl])).astype(BF16)
            psi_ref[c] = jnp.where(same_head, _dot_tn(kh_all[:, sl], vr_all[:, sl]), 0.0)
            dec_ref[c] = jnp.transpose(jnp.broadcast_to(e_last[:, sl], (LANES, MXU_DIM)))
            qr_ref[c] = jnp.concatenate([qt_b[:, sl], rt_b[:, sl]], axis=0)
            scores = (_dot_nt(stack_own(q_b[:, sl]), tile_rows(k_b[:, sl])) * dmask_ref[t]).astype(BF16)
            inner_ref[c] = unstack(_dot(scores, tile_rows(v_all[:, sl])))
            qd_ref[c] = qd_all[:, sl]
            kv_ref[c] = jnp.where(same_head, _dot_tn(kd_all[:, sl], v_all[:, sl]), 0.0)
        return carry

    lax.fori_loop(0, bsz, prepare, 0)

    for _ in range(5):
        for c in range(n_chains):
            pwb = pw_ref[c]
            pw_ref[c] = _dot(pwb, pwb).astype(BF16)
        for c in range(n_chains):
            tinv_ref[c] = _dot(tinv_ref[c], jnp.where(eye, one_b, pw_ref[c])).astype(BF16)

    chains = [(c, pl.ds((c // n_tiles) * CHUNK, CHUNK), slice((c % n_tiles) * MXU_DIM, (c % n_tiles + 1) * MXU_DIM))
              for c in range(n_chains)]
    qrm = [_dot(qr_ref[c], srw_ref[c].astype(BF16)) for c, _, _ in chains]
    u = [_dot(tinv_ref[c], (tile_rows(qrm[c][0:CHUNK]) + aqkv_ref[c]).astype(BF16)).astype(BF16)
         for c, _, _ in chains]
    for c, rows, sl in chains:
        yrw_ref[rows, sl] = qrm[c][CHUNK:] + brkv_ref[c] + unstack(_dot(brp_ref[c], u[c]))
        dec = dec_ref[c]
        srw_ref[c] = (srw_ref[c] * jnp.concatenate([dec, dec], axis=1) + psi_ref[c]
                      + jnp.where(same_head, _dot(pht_ref[c], u[c]), 0.0))
        s = sret_ref[c]
        yret_ref[rows, sl] = inner_ref[c] + _dot(qd_ref[c], s.astype(BF16))
        sret_ref[c] = s * cdec_ref[c % n_tiles] + kv_ref[c]

    inv_hd = 1.0 / HEAD_DIM

    def group_norm(y, eps):
        mu = group_sum(y) * inv_hd
        yc = y - mu
        var = group_sum(yc * yc) * inv_hd
        return yc * lax.rsqrt(var + eps)

    g = full(g_ref)
    ret_out = g * jax.nn.sigmoid(g) * (group_norm(yret_ref[...], RET_GN_EPS) * rgn_g_ref[...] + rgn_b_ref[...])
    o_ref[:, :, 0:width] = ret_out.astype(o_ref.dtype).reshape(bsz, CHUNK, width)
    vr = full(vr_ref)
    bonus = group_sum(full(r_ref) * full(km_ref) * rk_ref[...]) * vr
    rw_out = (group_norm(yrw_ref[...], RWKV_GN_EPS) * wgn_g_ref[...] + wgn_b_ref[...] + bonus) * full(gate_ref)
    o_ref[:, :, width:2 * width] = rw_out.astype(o_ref.dtype).reshape(bsz, CHUNK, width)


def _mix_call(streams, dmask, qdec, kdec, cdec, rgn_g, rgn_b, wgn_g, wgn_b, rk):
    bsz, s, width = streams[0].shape
    n_tiles = width // MXU_DIM
    blk = lambda j: (0, j, 0)
    stream_spec = pl.BlockSpec((bsz, CHUNK, width), blk)
    consts = [dmask, qdec, kdec, cdec, rgn_g, rgn_b, wgn_g, wgn_b, rk]
    n_chains = bsz * n_tiles
    tc = bsz * CHUNK
    tile = (n_chains, MXU_DIM, MXU_DIM)
    flat = (n_chains, CHUNK, MXU_DIM)
    return pl.pallas_call(
        _mix_kernel,
        grid=(s // CHUNK,),
        in_specs=[stream_spec] * len(streams) + [_const_spec(c.shape) for c in consts],
        out_specs=pl.BlockSpec((bsz, CHUNK, 2 * width), blk),
        out_shape=jax.ShapeDtypeStruct((bsz, s, 2 * width), BF16),
        scratch_shapes=[pltpu.VMEM(tile, F32),
                        pltpu.VMEM(tile, F32),
                        pltpu.VMEM((tc, width), F32), pltpu.VMEM((tc, width), F32),
                        pltpu.VMEM((tc, width), F32), pltpu.VMEM((tc, width), F32),
                        pltpu.VMEM(tile, BF16), pltpu.VMEM(tile, BF16), pltpu.VMEM(tile, BF16),
                        pltpu.VMEM(tile, F32), pltpu.VMEM(flat, F32),
                        pltpu.VMEM(tile, BF16), pltpu.VMEM(tile, F32),
                        pltpu.VMEM((n_chains, MXU_DIM, LANES), F32),
                        pltpu.VMEM((n_chains, 2 * CHUNK, MXU_DIM), BF16),
                        pltpu.VMEM(flat, F32), pltpu.VMEM(flat, BF16), pltpu.VMEM(tile, F32)],
        compiler_params=pltpu.CompilerParams(dimension_semantics=("arbitrary",),
                                             vmem_limit_bytes=VMEM_LIMIT_BYTES),
        name="mix_recurrences",
    )(*streams, *consts)


def _outproj_kernel(m_ref, x_ref, w_ref, g_ref, b_ref, o_ref, *, alpha):
    y = alpha * x_ref[...] + _dot(m_ref[...], w_ref[...])
    o_ref[...] = _layer_norm(y, g_ref[...], b_ref[...])


def _outproj_call(m2d, x2d, w, ln_g, ln_b, alpha, tm=512):
    n, d = x2d.shape
    tm = min(tm, n)
    row = lambda i: (i, 0)
    return pl.pallas_call(
        functools.partial(_outproj_kernel, alpha=alpha),
        grid=(n // tm,),
        in_specs=[pl.BlockSpec((tm, m2d.shape[1]), row), pl.BlockSpec((tm, d), row),
                  _const_spec(w.shape), _const_spec((1, d)), _const_spec((1, d))],
        out_specs=pl.BlockSpec((tm, d), row),
        out_shape=jax.ShapeDtypeStruct((n, d), F32),
        compiler_params=pltpu.CompilerParams(dimension_semantics=("arbitrary",),
                                             vmem_limit_bytes=VMEM_LIMIT_BYTES),
        name="outproj_ln",
    )(m2d, x2d, w, ln_g, ln_b)


def _ffn_weights(w_gu, w_down, fc=MXU_DIM):
    d, two_f = w_gu.shape
    d_ff = two_f // 2
    n_chunks = d_ff // fc
    wg3 = w_gu[:, :d_ff].astype(BF16).reshape(d, n_chunks, fc).transpose(1, 0, 2)
    wu3 = w_gu[:, d_ff:].astype(BF16).reshape(d, n_chunks, fc).transpose(1, 0, 2)
    wd3 = w_down.astype(BF16).reshape(n_chunks, fc, d)
    return wg3, wu3, wd3


def _retention_tables(n_heads):
    h = jnp.arange(n_heads, dtype=F32)
    log_gamma = jnp.log1p(-jnp.exp2(-5.0 - h))
    lg_lane = jnp.repeat(log_gamma, HEAD_DIM)[None, :]
    idx = jnp.arange(CHUNK, dtype=F32)[:, None]
    qdec = jnp.exp((idx + 1.0) * lg_lane)
    kdec = jnp.exp((CHUNK - 1.0 - idx) * lg_lane)
    n_tiles = n_heads // HEADS_PER_TILE
    cdec = jnp.broadcast_to(jnp.exp(CHUNK * lg_lane).reshape(n_tiles, MXU_DIM, 1),
                            (n_tiles, MXU_DIM, MXU_DIM))
    r = jnp.arange(MXU_DIM)
    rel = ((r % CHUNK)[:, None] - (r % CHUNK)[None, :]).astype(F32)
    same = (r // HEAD_DIM)[:, None] == (r // HEAD_DIM)[None, :]
    lg_rows = log_gamma.reshape(n_tiles, HEADS_PER_TILE)
    lg_rows = jnp.repeat(lg_rows, HEAD_DIM, axis=1)[:, :, None]
    dmask = jnp.where((same & (rel >= 0))[None], jnp.exp(jnp.where(rel >= 0, rel, 0.0)[None] * lg_rows), 0.0)
    return dmask, qdec, kdec, cdec


def _rotary_tables(s):
    pos = jnp.arange(s, dtype=F32)
    inv_freq = ROPE_BASE ** (-jnp.arange(0, HEAD_DIM, 2, dtype=F32) / HEAD_DIM)
    ang = pos[:, None] * inv_freq[None, :]
    cos, sin = jnp.cos(ang), jnp.sin(ang)
    reps = LANES // HEAD_DIM
    cos_t = jnp.tile(jnp.concatenate([cos, cos], axis=1), (1, reps))
    sin_t = jnp.tile(jnp.concatenate([-sin, sin], axis=1), (1, reps))
    return cos_t, sin_t


def kernel(x, p, ffn1_w_gu, ffn1_w_down, ln1_g, ln1_b, w_in, ret_gn_g, ret_gn_b, rw_mu, rw_w0, rw_w_up, rw_a0, rw_a_up, rw_g_up, rw_k_k, rw_k_a, rw_r_k, rw_gn_g, rw_gn_b, w_out, ln2_g, ln2_b, ffn2_w_gu, ffn2_w_down, ln3_g, ln3_b, ple_w_proj, ple_w_gate, ple_b_gate):
    bsz, s, d = x.shape
    depth = ffn1_w_gu.shape[0]
    alpha = (2.0 * depth) ** 0.25
    width = rw_w0.shape[1]
    n_heads = width // HEAD_DIM
    ret_cols = 4 * width
    lora = DECAY_LORA + AAA_LORA + GATE_LORA
    lora_pad = -(-lora // LANES) * LANES

    dmask, qdec, kdec, cdec = _retention_tables(n_heads)
    cos_t, sin_t = _rotary_tables(s)
    row = lambda v: v.reshape(1, -1)

    h = x.reshape(bsz * s, d)
    for i in range(depth):
        wg3, wu3, wd3 = _ffn_weights(ffn1_w_gu[i], ffn1_w_down[i])
        h = _ffn_call(h, wg3, wu3, wd3, row(ln1_g[i]), row(ln1_b[i]), alpha)

        wi = w_in[i].astype(BF16)
        wret = wi[:, :ret_cols]
        wrw = jnp.pad(wi[:, ret_cols:], ((0, 0), (0, lora_pad - lora)))
        mu = row(jnp.pad(rw_mu[i], (0, lora_pad - lora)))
        wlora = jnp.zeros((lora_pad, 3 * width), BF16)
        wlora = wlora.at[:DECAY_LORA, :width].set(rw_w_up[i].astype(BF16))
        wlora = wlora.at[DECAY_LORA:DECAY_LORA + AAA_LORA, width:2 * width].set(rw_a_up[i].astype(BF16))
        wlora = wlora.at[DECAY_LORA + AAA_LORA:lora, 2 * width:].set(rw_g_up[i].astype(BF16))
        streams = _inproj_call(h.reshape(bsz, s, d), wret, wrw, wlora, cos_t, sin_t, mu,
                               row(rw_w0[i]), row(rw_a0[i]), row(rw_k_k[i]), row(rw_k_a[i]))
        mixed = _mix_call(streams, dmask, qdec, kdec, cdec, row(ret_gn_g[i]), row(ret_gn_b[i]),
                          row(rw_gn_g[i]), row(rw_gn_b[i]), row(rw_r_k[i]))
        h = _outproj_call(mixed.reshape(bsz * s, 2 * width), h, w_out[i].astype(BF16),
                          row(ln2_g[i]), row(ln2_b[i]), alpha)

        wg3, wu3, wd3 = _ffn_weights(ffn2_w_gu[i], ffn2_w_down[i])
        ple = (p[i].reshape(bsz * s, -1), ple_w_proj[i].astype(BF16), ple_w_gate[i].astype(BF16),
               row(ple_b_gate[i]))
        h = _ffn_call(h, wg3, wu3, wd3, row(ln3_g[i]), row(ln3_b[i]), alpha, ple=ple)
    return h.reshape(bsz, s, d)
```

```python
import functools
import math

import jax
import jax.numpy as jnp
from jax import lax
from jax.experimental import pallas as pl
from jax.experimental.pallas import tpu as pltpu

D_PLE = 256
HEAD_DIM = 64
ROPE_BASE = 10000.0
DECAY_LORA = 64
AAA_LORA = 64
GATE_LORA = 160
LN_EPS = 1e-5
RET_GN_EPS = 1e-5
RWKV_GN_EPS = 64e-5
DECAY_SCALE = math.exp(-0.5)
HEAD_SHIFT = HEAD_DIM.bit_length() - 1

LANES = 128
MXU_DIM = 256
HEADS_PER_TILE = MXU_DIM // HEAD_DIM
CHUNK = 64
VMEM_LIMIT_BYTES = 56 * 1024 * 1024

F32 = jnp.float32
BF16 = jnp.bfloat16


def _dot(a, b):
    return jnp.dot(a, b, preferred_element_type=F32)


def _dot_nt(a, b):
    return lax.dot_general(a, b, (((1,), (1,)), ((), ())), preferred_element_type=F32)


def _dot_tn(a, b):
    return lax.dot_general(a, b, (((0,), (0,)), ((), ())), preferred_element_type=F32)


def _layer_norm(y, g, b):
    mu = jnp.mean(y, axis=-1, keepdims=True)
    yc = y - mu
    var = jnp.mean(yc * yc, axis=-1, keepdims=True)
    return yc * lax.rsqrt(var + LN_EPS) * g + b


def _const_spec(shape):
    nd = len(shape)
    return pl.BlockSpec(shape, lambda *_: (0,) * nd, pipeline_mode=pl.Buffered(1))


def _ffn_kernel(*refs, alpha, fc, with_ple):
    if with_ple:
        (x_ref, wgu_ref, wd_ref, g_ref, b_ref, p_ref, wp_ref, wpg_ref, bpg_ref,
         o_ref, act_ref) = refs
    else:
        x_ref, wgu_ref, wd_ref, g_ref, b_ref, o_ref, act_ref = refs
    d_ff = wd_ref.shape[0]
    xb = x_ref[...].astype(BF16)
    for c in range(d_ff // fc):
        hg = _dot(xb, wgu_ref[:, c * fc:(c + 1) * fc])
        hu = _dot(xb, wgu_ref[:, d_ff + c * fc:d_ff + (c + 1) * fc])
        act_ref[:, c * fc:(c + 1) * fc] = (hg * jax.nn.sigmoid(hg) * hu).astype(BF16)
    down = _dot(act_ref[...], wd_ref[...])
    y = _layer_norm(alpha * x_ref[...] + 0.5 * down, g_ref[...], b_ref[...])
    if with_ple:
        gate = jax.nn.sigmoid(_dot(y.astype(BF16), wpg_ref[...]) + bpg_ref[...])
        y = y + gate * _dot(p_ref[...].astype(BF16), wp_ref[...])
    o_ref[...] = y


def _ffn_call(x2d, wgu, wd, ln_g, ln_b, alpha, ple=None, tm=512, fc=MXU_DIM):
    n, d = x2d.shape
    d_ff = wd.shape[0]
    tm = min(tm, n)
    row = lambda i: (i, 0)
    in_specs = [pl.BlockSpec((tm, d), row), _const_spec(wgu.shape), _const_spec(wd.shape),
                _const_spec((1, d)), _const_spec((1, d))]
    args = [x2d, wgu, wd, ln_g, ln_b]
    if ple is not None:
        p2d, wp, wpg, bpg = ple
        in_specs += [pl.BlockSpec((tm, p2d.shape[1]), row), _const_spec(wp.shape),
                     _const_spec(wpg.shape), _const_spec((1, d))]
        args += [p2d, wp, wpg, bpg]
    return pl.pallas_call(
        functools.partial(_ffn_kernel, alpha=alpha, fc=fc, with_ple=ple is not None),
        grid=(n // tm,),
        in_specs=in_specs,
        out_specs=pl.BlockSpec((tm, d), row),
        out_shape=jax.ShapeDtypeStruct((n, d), F32),
        scratch_shapes=[pltpu.VMEM((tm, d_ff), BF16)],
        compiler_params=pltpu.CompilerParams(dimension_semantics=("arbitrary",),
                                             vmem_limit_bytes=VMEM_LIMIT_BYTES),
        name="ffn_ln",
    )(*args)


def _inproj_kernel(x_ref, wret_ref, wrw_ref, wlora_ref, cos_ref, sin_ref, mu_ref, w0_ref, a0_ref,
                   kk_ref, ka_ref,
                   q_ref, k_ref, v_ref, g_ref, r_ref, km_ref, vr_ref, lw_ref, a_ref, kku_ref,
                   gate_ref, carry_ref, *, width):
    j = pl.program_id(1)
    tm = x_ref.shape[0]
    xb = x_ref[...].astype(BF16)

    lane = lax.broadcasted_iota(jnp.int32, (tm, width), 1)
    first_half = (lane & (HEAD_DIM // 2)) == 0
    reps = width // cos_ref.shape[1]
    cos = jnp.concatenate([cos_ref[...]] * reps, axis=1)
    sin = jnp.concatenate([sin_ref[...]] * reps, axis=1)

    def rotary(t):
        swapped = jnp.where(first_half, pltpu.roll(t, width - HEAD_DIM // 2, 1),
                            pltpu.roll(t, HEAD_DIM // 2, 1))
        return t * cos + swapped * sin

    q_ref[...] = rotary(_dot(xb, wret_ref[:, 0 * width:1 * width]))
    k_ref[...] = rotary(_dot(xb, wret_ref[:, 1 * width:2 * width])) * (HEAD_DIM ** -0.5)
    v_ref[...] = _dot(xb, wret_ref[:, 2 * width:3 * width])
    g_ref[...] = _dot(xb, wret_ref[:, 3 * width:4 * width])

    z = _dot(xb, wrw_ref[...])

    @pl.when(j == 0)
    def _():
        carry_ref[...] = jnp.zeros_like(carry_ref)

    rowid = lax.broadcasted_iota(jnp.int32, z.shape, 0)
    prev = jnp.where(rowid == 0, carry_ref[0:1, :], pltpu.roll(z, 1, 0))
    carry_ref[0:1, :] = z[tm - 1:tm, :]
    sh = z + (prev - z) * mu_ref[...]

    r = sh[:, 0 * width:1 * width]
    kr = sh[:, 1 * width:2 * width]
    vr = sh[:, 2 * width:3 * width]
    lo = sh[:, 3 * width:]
    ll = lax.broadcasted_iota(jnp.int32, lo.shape, 1)
    act = jnp.where(ll < DECAY_LORA, jnp.tanh(lo),
                    jnp.where(ll < DECAY_LORA + AAA_LORA, lo, jax.nn.sigmoid(lo)))
    up = _dot(act.astype(BF16), wlora_ref[...])
    lw_ref[...] = -DECAY_SCALE * jax.nn.sigmoid(w0_ref[...] + up[:, 0 * width:1 * width])
    a = jax.nn.sigmoid(a0_ref[...] + up[:, 1 * width:2 * width])
    gate_ref[...] = up[:, 2 * width:3 * width]
    a_ref[...] = a
    r_ref[...] = r
    vr_ref[...] = vr
    kku_ref[...] = kr * kk_ref[...]
    km_ref[...] = kr * (1.0 + (a - 1.0) * ka_ref[...])


def _inproj_call(x3d, wret, wrw, wlora, cos, sin, mu, w0, a0, k_k, k_a, tm=256):
    bsz, s, d = x3d.shape
    width = wret.shape[1] // 4
    tm = min(tm, s)
    blk = lambda b, j: (b, j, 0)
    out_spec = pl.BlockSpec((None, tm, width), blk)
    out_sds = jax.ShapeDtypeStruct((bsz, s, width), F32)
    n_out = 11
    return pl.pallas_call(
        functools.partial(_inproj_kernel, width=width),
        grid=(bsz, s // tm),
        in_specs=[pl.BlockSpec((None, tm, d), blk), _const_spec(wret.shape), _const_spec(wrw.shape),
                  _const_spec(wlora.shape),
                  pl.BlockSpec((tm, cos.shape[1]), lambda b, j: (j, 0)),
                  pl.BlockSpec((tm, sin.shape[1]), lambda b, j: (j, 0)),
                  _const_spec(mu.shape), _const_spec(w0.shape), _const_spec(a0.shape),
                  _const_spec(k_k.shape), _const_spec(k_a.shape)],
        out_specs=[out_spec] * n_out,
        out_shape=[out_sds] * n_out,
        scratch_shapes=[pltpu.VMEM((8, wrw.shape[1]), F32)],
        compiler_params=pltpu.CompilerParams(dimension_semantics=("arbitrary", "arbitrary"),
                                             vmem_limit_bytes=VMEM_LIMIT_BYTES),
        name="inproj_prep",
    )(x3d, wret, wrw, wlora, cos, sin, mu, w0, a0, k_k, k_a)


def _split_hi_lo(x):
    hi = x.astype(BF16)
    lo = (x - hi.astype(F32)).astype(BF16)
    return hi, lo


def _mix_kernel(q_ref, k_ref, v_ref, g_ref, r_ref, km_ref, vr_ref, lw_ref, a_ref, kku_ref, gate_ref,
                dmask_ref, qdec_ref, kdec_ref, cdec_ref, rgn_g_ref, rgn_b_ref, wgn_g_ref, wgn_b_ref,
                rk_ref, o_ref,
                sret_ref, srw_ref, kkn_ref, p_ref, yret_ref, yrw_ref,
                pw_ref, tinv_ref, brp_ref, aqkv_ref, brkv_ref, pht_ref, psi_ref, dec_ref, qr_ref,
                inner_ref, qd_ref, kv_ref):
    bsz, _, width = q_ref.shape
    n_tiles = width // MXU_DIM
    n_chains = bsz * n_tiles
    hpt = HEADS_PER_TILE
    n_rows = bsz * CHUNK

    def full(ref):
        return ref[...].reshape(n_rows, ref.shape[-1])

    @pl.when(pl.program_id(0) == 0)
    def _():
        sret_ref[...] = jnp.zeros_like(sret_ref)
        srw_ref[...] = jnp.zeros_like(srw_ref)

    ri = lax.broadcasted_iota(jnp.int32, (MXU_DIM, MXU_DIM), 0)
    ci = lax.broadcasted_iota(jnp.int32, (MXU_DIM, MXU_DIM), 1)
    same_head = (ri >> HEAD_SHIFT) == (ci >> HEAD_SHIFT)
    strict_lower = same_head & ((ri & (CHUNK - 1)) > (ci & (CHUNK - 1)))
    incl_lower = same_head & ((ri & (CHUNK - 1)) >= (ci & (CHUNK - 1)))
    eye = ri == ci
    one_b = jnp.ones((MXU_DIM, MXU_DIM), BF16)
    ones_blk = same_head.astype(BF16)
    lane_head = lax.broadcasted_iota(jnp.int32, (CHUNK, MXU_DIM), 1) >> HEAD_SHIFT
    ti = lax.broadcasted_iota(jnp.int32, (CHUNK, CHUNK), 0)
    tj = lax.broadcasted_iota(jnp.int32, (CHUNK, CHUNK), 1)
    tri = (ti >= tj).astype(BF16)

    def tile_rows(x):
        return jnp.concatenate([x] * hpt, axis=0)

    def stack_own(x):
        return jnp.where(same_head, tile_rows(x), jnp.zeros((), x.dtype))

    def unstack(xs):
        out = jnp.where(lane_head == 0, xs[0:CHUNK], 0.0)
        for h in range(1, hpt):
            out = out + jnp.where(lane_head == h, xs[h * CHUNK:(h + 1) * CHUNK], 0.0)
        return out

    def group_sum(x):
        outs = []
        for t in range(n_tiles):
            outs.append(_dot(x[:, t * MXU_DIM:(t + 1) * MXU_DIM].astype(BF16), ones_blk))
        return jnp.concatenate(outs, axis=1)

    kku = full(kku_ref)
    norm = jnp.maximum(jnp.sqrt(group_sum(kku * kku)), 1e-12)
    kkn = kku / norm
    kkn_ref[...] = kkn
    p_ref[...] = -(kkn * full(a_ref))

    def prepare(b, carry):
        rows = pl.ds(b * CHUNK, CHUNK)
        lw = lw_ref[b]
        lw_hi, lw_lo = _split_hi_lo(lw)
        cum = _dot(tri, lw_hi) + _dot(tri, lw_lo)
        e_in = jnp.exp(cum)
        e_ex = jnp.exp(cum - lw)
        e_neg = jnp.exp(-cum)
        e_last = e_in[CHUNK - 1:CHUNK, :]
        qt_all = kkn_ref[rows, :] * e_ex
        rt_all = r_ref[b] * e_in
        pt_all = p_ref[rows, :] * e_neg
        kt_all = km_ref[b] * e_neg
        ph_all = pt_all * e_last
        kh_all = (kt_all * e_last).astype(BF16)
        vr_all = vr_ref[b].astype(BF16)
        q_all = q_ref[b]
        k_all = k_ref[b]
        v_all = v_ref[b].astype(BF16)
        qd_all = (q_all * qdec_ref[...]).astype(BF16)
        kd_all = (k_all * kdec_ref[...]).astype(BF16)
        qt_b, rt_b = qt_all.astype(BF16), rt_all.astype(BF16)
        pt_b, kt_b = pt_all.astype(BF16), kt_all.astype(BF16)
        q_b, k_b = q_all.astype(BF16), k_all.astype(BF16)

        for t in range(n_tiles):
            c = b * n_tiles + t
            sl = slice(t * MXU_DIM, (t + 1) * MXU_DIM)
            qs = stack_own(qt_b[:, sl])
            rs = stack_own(rt_b[:, sl])
            pt = tile_rows(pt_b[:, sl])
            kt = tile_rows(kt_b[:, sl])
            vt = tile_rows(vr_all[:, sl])
            a_qp = jnp.where(strict_lower, _dot_nt(qs, pt), 0.0).astype(BF16)
            pw_ref[c] = a_qp
            tinv_ref[c] = jnp.where(eye, one_b, a_qp)
            a_qk = jnp.where(strict_lower, _dot_nt(qs, kt), 0.0).astype(BF16)
            aqkv_ref[c] = _dot(a_qk, vt)
            b_rk = jnp.where(incl_lower, _dot_nt(rs, kt), 0.0).astype(BF16)
            brkv_ref[c] = unstack(_dot(b_rk, vt))
            brp_ref[c] = jnp.where(incl_lower, _dot_nt(rs, pt), 0.0).astype(BF16)
            pht_ref[c] = jnp.transpose(stack_own(ph_all[:, sl])).astype(BF16)
            psi_ref[c] = jnp.where(same_head, _dot_tn(kh_all[:, sl], vr_all[:, sl]), 0.0)
            dec_ref[c] = jnp.transpose(jnp.broadcast_to(e_last[:, sl], (LANES, MXU_DIM)))
            qr_ref[c] = jnp.concatenate([qt_b[:, sl], rt_b[:, sl]], axis=0)
            scores = (_dot_nt(stack_own(q_b[:, sl]), tile_rows(k_b[:, sl])) * dmask_ref[t]).astype(BF16)
            inner_ref[c] = unstack(_dot(scores, tile_rows(v_all[:, sl])))
            qd_ref[c] = qd_all[:, sl]
            kv_ref[c] = jnp.where(same_head, _dot_tn(kd_all[:, sl], v_all[:, sl]), 0.0)
        return carry

    for b in range(bsz):
        prepare(b, 0)

    for _ in range(5):
        for c in range(n_chains):
            pwb = pw_ref[c]
            pw_ref[c] = _dot(pwb, pwb).astype(BF16)
        for c in range(n_chains):
            tinv_ref[c] = _dot(tinv_ref[c], jnp.where(eye, one_b, pw_ref[c])).astype(BF16)

    chains = [(c, pl.ds((c // n_tiles) * CHUNK, CHUNK), slice((c % n_tiles) * MXU_DIM, (c % n_tiles + 1) * MXU_DIM))
              for c in range(n_chains)]
    qrm = [_dot(qr_ref[c], srw_ref[c].astype(BF16)) for c, _, _ in chains]
    u = [_dot(tinv_ref[c], (tile_rows(qrm[c][0:CHUNK]) + aqkv_ref[c]).astype(BF16)).astype(BF16)
         for c, _, _ in chains]
    for c, rows, sl in chains:
        yrw_ref[rows, sl] = qrm[c][CHUNK:] + brkv_ref[c] + unstack(_dot(brp_ref[c], u[c]))
        dec = dec_ref[c]
        srw_ref[c] = (srw_ref[c] * jnp.concatenate([dec, dec], axis=1) + psi_ref[c]
                      + jnp.where(same_head, _dot(pht_ref[c], u[c]), 0.0))
        s = sret_ref[c]
        yret_ref[rows, sl] = inner_ref[c] + _dot(qd_ref[c], s.astype(BF16))
        sret_ref[c] = s * cdec_ref[c % n_tiles] + kv_ref[c]

    inv_hd = 1.0 / HEAD_DIM

    def group_norm(y, eps):
        mu = group_sum(y) * inv_hd
        yc = y - mu
        var = group_sum(yc * yc) * inv_hd
        return yc * lax.rsqrt(var + eps)

    g = full(g_ref)
    ret_out = g * jax.nn.sigmoid(g) * (group_norm(yret_ref[...], RET_GN_EPS) * rgn_g_ref[...] + rgn_b_ref[...])
    o_ref[:, :, 0:width] = ret_out.astype(o_ref.dtype).reshape(bsz, CHUNK, width)
    vr = full(vr_ref)
    bonus = group_sum(full(r_ref) * full(km_ref) * rk_ref[...]) * vr
    rw_out = (group_norm(yrw_ref[...], RWKV_GN_EPS) * wgn_g_ref[...] + wgn_b_ref[...] + bonus) * full(gate_ref)
    o_ref[:, :, width:2 * width] = rw_out.astype(o_ref.dtype).reshape(bsz, CHUNK, width)


def _mix_call(streams, dmask, qdec, kdec, cdec, rgn_g, rgn_b, wgn_g, wgn_b, rk):
    bsz, s, width = streams[0].shape
    n_tiles = width // MXU_DIM
    blk = lambda j: (0, j, 0)
    stream_spec = pl.BlockSpec((bsz, CHUNK, width), blk)
    consts = [dmask, qdec, kdec, cdec, rgn_g, rgn_b, wgn_g, wgn_b, rk]
    n_chains = bsz * n_tiles
    tc = bsz * CHUNK
    tile = (n_chains, MXU_DIM, MXU_DIM)
    flat = (n_chains, CHUNK, MXU_DIM)
    return pl.pallas_call(
        _mix_kernel,
        grid=(s // CHUNK,),
        in_specs=[stream_spec] * len(streams) + [_const_spec(c.shape) for c in consts],
        out_specs=pl.BlockSpec((bsz, CHUNK, 2 * width), blk),
        out_shape=jax.ShapeDtypeStruct((bsz, s, 2 * width), BF16),
        scratch_shapes=[pltpu.VMEM(tile, F32),
                        pltpu.VMEM(tile, F32),
                        pltpu.VMEM((tc, width), F32), pltpu.VMEM((tc, width), F32),
                        pltpu.VMEM((tc, width), F32), pltpu.VMEM((tc, width), F32),
                        pltpu.VMEM(tile, BF16), pltpu.VMEM(tile, BF16), pltpu.VMEM(tile, BF16),
                        pltpu.VMEM(tile, F32), pltpu.VMEM(flat, F32),
                        pltpu.VMEM(tile, BF16), pltpu.VMEM(tile, F32),
                        pltpu.VMEM((n_chains, MXU_DIM, LANES), F32),
                        pltpu.VMEM((n_chains, 2 * CHUNK, MXU_DIM), BF16),
                        pltpu.VMEM(flat, F32), pltpu.VMEM(flat, BF16), pltpu.VMEM(tile, F32)],
        compiler_params=pltpu.CompilerParams(dimension_semantics=("arbitrary",),
                                             vmem_limit_bytes=VMEM_LIMIT_BYTES),
        name="mix_recurrences",
    )(*streams, *consts)


def _outproj_kernel(m_ref, x_ref, w_ref, g_ref, b_ref, o_ref, *, alpha):
    y = alpha * x_ref[...] + _dot(m_ref[...], w_ref[...])
    o_ref[...] = _layer_norm(y, g_ref[...], b_ref[...])


def _outproj_call(m2d, x2d, w, ln_g, ln_b, alpha, tm=512):
    n, d = x2d.shape
    tm = min(tm, n)
    row = lambda i: (i, 0)
    return pl.pallas_call(
        functools.partial(_outproj_kernel, alpha=alpha),
        grid=(n // tm,),
        in_specs=[pl.BlockSpec((tm, m2d.shape[1]), row), pl.BlockSpec((tm, d), row),
                  _const_spec(w.shape), _const_spec((1, d)), _const_spec((1, d))],
        out_specs=pl.BlockSpec((tm, d), row),
        out_shape=jax.ShapeDtypeStruct((n, d), F32),
        compiler_params=pltpu.CompilerParams(dimension_semantics=("arbitrary",),
                                             vmem_limit_bytes=VMEM_LIMIT_BYTES),
        name="outproj_ln",
    )(m2d, x2d, w, ln_g, ln_b)


def _retention_tables(n_heads):
    h = jnp.arange(n_heads, dtype=F32)
    log_gamma = jnp.log1p(-jnp.exp2(-5.0 - h))
    lg_lane = jnp.repeat(log_gamma, HEAD_DIM)[None, :]
    idx = jnp.arange(CHUNK, dtype=F32)[:, None]
    qdec = jnp.exp((idx + 1.0) * lg_lane)
    kdec = jnp.exp((CHUNK - 1.0 - idx) * lg_lane)
    n_tiles = n_heads // HEADS_PER_TILE
    cdec = jnp.broadcast_to(jnp.exp(CHUNK * lg_lane).reshape(n_tiles, MXU_DIM, 1),
                            (n_tiles, MXU_DIM, MXU_DIM))
    r = jnp.arange(MXU_DIM)
    rel = ((r % CHUNK)[:, None] - (r % CHUNK)[None, :]).astype(F32)
    same = (r // HEAD_DIM)[:, None] == (r // HEAD_DIM)[None, :]
    lg_rows = log_gamma.reshape(n_tiles, HEADS_PER_TILE)
    lg_rows = jnp.repeat(lg_rows, HEAD_DIM, axis=1)[:, :, None]
    dmask = jnp.where((same & (rel >= 0))[None], jnp.exp(jnp.where(rel >= 0, rel, 0.0)[None] * lg_rows), 0.0)
    return dmask, qdec, kdec, cdec


def _rotary_tables(s):
    pos = jnp.arange(s, dtype=F32)
    inv_freq = ROPE_BASE ** (-jnp.arange(0, HEAD_DIM, 2, dtype=F32) / HEAD_DIM)
    ang = pos[:, None] * inv_freq[None, :]
    cos, sin = jnp.cos(ang), jnp.sin(ang)
    reps = LANES // HEAD_DIM
    cos_t = jnp.tile(jnp.concatenate([cos, cos], axis=1), (1, reps))
    sin_t = jnp.tile(jnp.concatenate([-sin, sin], axis=1), (1, reps))
    return cos_t, sin_t


def kernel(x, p, ffn1_w_gu, ffn1_w_down, ln1_g, ln1_b, w_in, ret_gn_g, ret_gn_b, rw_mu, rw_w0, rw_w_up, rw_a0, rw_a_up, rw_g_up, rw_k_k, rw_k_a, rw_r_k, rw_gn_g, rw_gn_b, w_out, ln2_g, ln2_b, ffn2_w_gu, ffn2_w_down, ln3_g, ln3_b, ple_w_proj, ple_w_gate, ple_b_gate):
    bsz, s, d = x.shape
    depth = ffn1_w_gu.shape[0]
    alpha = (2.0 * depth) ** 0.25
    width = rw_w0.shape[1]
    n_heads = width // HEAD_DIM
    ret_cols = 4 * width
    lora = DECAY_LORA + AAA_LORA + GATE_LORA
    lora_pad = -(-lora // LANES) * LANES

    dmask, qdec, kdec, cdec = _retention_tables(n_heads)
    cos_t, sin_t = _rotary_tables(s)
    row = lambda v: v.reshape(1, -1)

    h = x.reshape(bsz * s, d)
    for i in range(depth):
        h = _ffn_call(h, ffn1_w_gu[i].astype(BF16), ffn1_w_down[i].astype(BF16),
                      row(ln1_g[i]), row(ln1_b[i]), alpha)

        wi = w_in[i].astype(BF16)
        wret = wi[:, :ret_cols]
        wrw = jnp.pad(wi[:, ret_cols:], ((0, 0), (0, lora_pad - lora)))
        mu = row(jnp.pad(rw_mu[i], (0, lora_pad - lora)))
        wlora = jnp.zeros((lora_pad, 3 * width), BF16)
        wlora = wlora.at[:DECAY_LORA, :width].set(rw_w_up[i].astype(BF16))
        wlora = wlora.at[DECAY_LORA:DECAY_LORA + AAA_LORA, width:2 * width].set(rw_a_up[i].astype(BF16))
        wlora = wlora.at[DECAY_LORA + AAA_LORA:lora, 2 * width:].set(rw_g_up[i].astype(BF16))
        streams = _inproj_call(h.reshape(bsz, s, d), wret, wrw, wlora, cos_t, sin_t, mu,
                               row(rw_w0[i]), row(rw_a0[i]), row(rw_k_k[i]), row(rw_k_a[i]))
        mixed = _mix_call(streams, dmask, qdec, kdec, cdec, row(ret_gn_g[i]), row(ret_gn_b[i]),
                          row(rw_gn_g[i]), row(rw_gn_b[i]), row(rw_r_k[i]))
        h = _outproj_call(mixed.reshape(bsz * s, 2 * width), h, w_out[i].astype(BF16),
                          row(ln2_g[i]), row(ln2_b[i]), alpha)

        ple = (p[i].reshape(bsz * s, -1), ple_w_proj[i].astype(BF16), ple_w_gate[i].astype(BF16),
               row(ple_b_gate[i]))
        h = _ffn_call(h, ffn2_w_gu[i].astype(BF16), ffn2_w_down[i].astype(BF16),
                      row(ln3_g[i]), row(ln3_b[i]), alpha, ple=ple)
    return h.reshape(bsz, s, d)
```

```python
import functools
import math

import jax
import jax.numpy as jnp
from jax import lax
from jax.experimental import pallas as pl
from jax.experimental.pallas import tpu as pltpu

D_PLE = 256
HEAD_DIM = 64
ROPE_BASE = 10000.0
DECAY_LORA = 64
AAA_LORA = 64
GATE_LORA = 160
LN_EPS = 1e-5
RET_GN_EPS = 1e-5
RWKV_GN_EPS = 64e-5
DECAY_SCALE = math.exp(-0.5)
HEAD_SHIFT = HEAD_DIM.bit_length() - 1

LANES = 128
MXU_DIM = 256
HEADS_PER_TILE = MXU_DIM // HEAD_DIM
CHUNK = 64
VMEM_LIMIT_BYTES = 56 * 1024 * 1024

F32 = jnp.float32
BF16 = jnp.bfloat16


def _dot(a, b):
    return jnp.dot(a, b, preferred_element_type=F32)


def _dot_nt(a, b):
    return lax.dot_general(a, b, (((1,), (1,)), ((), ())), preferred_element_type=F32)


def _dot_tn(a, b):
    return lax.dot_general(a, b, (((0,), (0,)), ((), ())), preferred_element_type=F32)


def _layer_norm(y, g, b):
    mu = jnp.mean(y, axis=-1, keepdims=True)
    yc = y - mu
    var = jnp.mean(yc * yc, axis=-1, keepdims=True)
    return yc * lax.rsqrt(var + LN_EPS) * g + b


def _const_spec(shape):
    nd = len(shape)
    return pl.BlockSpec(shape, lambda *_: (0,) * nd, pipeline_mode=pl.Buffered(1))


def _ffn_kernel(*refs, alpha, fc, with_pre, with_ple):
    refs = list(refs)
    x_ref = refs.pop(0)
    if with_pre:
        m_ref, wo_ref, gp_ref, bp_ref = refs[:4]
        refs = refs[4:]
    wgu_ref, wd_ref, g_ref, b_ref = refs[:4]
    refs = refs[4:]
    if with_ple:
        p_ref, wp_ref, wpg_ref, bpg_ref = refs[:4]
        refs = refs[4:]
    o_ref, act_ref = refs[:2]
    d_ff = wd_ref.shape[0]
    if with_pre:
        xs_ref = refs[2]
        xs_ref[...] = _layer_norm(alpha * x_ref[...] + _dot(m_ref[...], wo_ref[...]), gp_ref[...], bp_ref[...])
    else:
        xs_ref = x_ref
    xb = xs_ref[...].astype(BF16)
    for c in range(d_ff // fc):
        hg = _dot(xb, wgu_ref[:, c * fc:(c + 1) * fc])
        hu = _dot(xb, wgu_ref[:, d_ff + c * fc:d_ff + (c + 1) * fc])
        act_ref[:, c * fc:(c + 1) * fc] = (hg * jax.nn.sigmoid(hg) * hu).astype(BF16)
    down = _dot(act_ref[...], wd_ref[...])
    y = _layer_norm(alpha * xs_ref[...] + 0.5 * down, g_ref[...], b_ref[...])
    if with_ple:
        gate = jax.nn.sigmoid(_dot(y.astype(BF16), wpg_ref[...]) + bpg_ref[...])
        y = y + gate * _dot(p_ref[...].astype(BF16), wp_ref[...])
    o_ref[...] = y


def _ffn_call(x2d, wgu, wd, ln_g, ln_b, alpha, pre=None, ple=None, tm=512, fc=MXU_DIM):
    n, d = x2d.shape
    d_ff = wd.shape[0]
    tm = min(tm, n)
    row = lambda i: (i, 0)
    in_specs = [pl.BlockSpec((tm, d), row)]
    args = [x2d]
    if pre is not None:
        m2d, wo, gp, bp = pre
        in_specs += [pl.BlockSpec((tm, m2d.shape[1]), row), _const_spec(wo.shape),
                     _const_spec((1, d)), _const_spec((1, d))]
        args += [m2d, wo, gp, bp]
    in_specs += [_const_spec(wgu.shape), _const_spec(wd.shape), _const_spec((1, d)), _const_spec((1, d))]
    args += [wgu, wd, ln_g, ln_b]
    if ple is not None:
        p2d, wp, wpg, bpg = ple
        in_specs += [pl.BlockSpec((tm, p2d.shape[1]), row), _const_spec(wp.shape),
                     _const_spec(wpg.shape), _const_spec((1, d))]
        args += [p2d, wp, wpg, bpg]
    return pl.pallas_call(
        functools.partial(_ffn_kernel, alpha=alpha, fc=fc, with_pre=pre is not None,
                          with_ple=ple is not None),
        grid=(n // tm,),
        in_specs=in_specs,
        out_specs=pl.BlockSpec((tm, d), row),
        out_shape=jax.ShapeDtypeStruct((n, d), F32),
        scratch_shapes=[pltpu.VMEM((tm, d_ff), BF16)] + ([pltpu.VMEM((tm, d), F32)] if pre is not None else []),
        compiler_params=pltpu.CompilerParams(dimension_semantics=("arbitrary",),
                                             vmem_limit_bytes=VMEM_LIMIT_BYTES),
        name="ffn_ln",
    )(*args)


def _inproj_kernel(x_ref, wret_ref, wrw_ref, wlora_ref, cos_ref, sin_ref, mu_ref, w0_ref, a0_ref,
                   kk_ref, ka_ref,
                   q_ref, k_ref, v_ref, g_ref, r_ref, km_ref, vr_ref, lw_ref, a_ref, kku_ref,
                   gate_ref, carry_ref, *, width):
    j = pl.program_id(1)
    tm = x_ref.shape[0]
    xb = x_ref[...].astype(BF16)

    lane = lax.broadcasted_iota(jnp.int32, (tm, width), 1)
    first_half = (lane & (HEAD_DIM // 2)) == 0
    reps = width // cos_ref.shape[1]
    cos = jnp.concatenate([cos_ref[...]] * reps, axis=1)
    sin = jnp.concatenate([sin_ref[...]] * reps, axis=1)

    def rotary(t):
        swapped = jnp.where(first_half, pltpu.roll(t, width - HEAD_DIM // 2, 1),
                            pltpu.roll(t, HEAD_DIM // 2, 1))
        return t * cos + swapped * sin

    def put(ref, val):
        ref[...] = val.astype(ref.dtype)

    put(q_ref, rotary(_dot(xb, wret_ref[:, 0 * width:1 * width])))
    put(k_ref, rotary(_dot(xb, wret_ref[:, 1 * width:2 * width])) * (HEAD_DIM ** -0.5))
    put(v_ref, _dot(xb, wret_ref[:, 2 * width:3 * width]))
    put(g_ref, _dot(xb, wret_ref[:, 3 * width:4 * width]))

    z = _dot(xb, wrw_ref[...])

    @pl.when(j == 0)
    def _():
        carry_ref[...] = jnp.zeros_like(carry_ref)

    rowid = lax.broadcasted_iota(jnp.int32, z.shape, 0)
    prev = jnp.where(rowid == 0, carry_ref[0:1, :], pltpu.roll(z, 1, 0))
    carry_ref[0:1, :] = z[tm - 1:tm, :]
    sh = z + (prev - z) * mu_ref[...]

    r = sh[:, 0 * width:1 * width]
    kr = sh[:, 1 * width:2 * width]
    vr = sh[:, 2 * width:3 * width]
    lo = sh[:, 3 * width:]
    ll = lax.broadcasted_iota(jnp.int32, lo.shape, 1)
    act = jnp.where(ll < DECAY_LORA, jnp.tanh(lo),
                    jnp.where(ll < DECAY_LORA + AAA_LORA, lo, jax.nn.sigmoid(lo)))
    up = _dot(act.astype(BF16), wlora_ref[...])
    lw_ref[...] = -DECAY_SCALE * jax.nn.sigmoid(w0_ref[...] + up[:, 0 * width:1 * width])
    a = jax.nn.sigmoid(a0_ref[...] + up[:, 1 * width:2 * width])
    put(gate_ref, up[:, 2 * width:3 * width])
    put(a_ref, a)
    put(r_ref, r)
    put(vr_ref, vr)
    put(kku_ref, kr * kk_ref[...])
    put(km_ref, kr * (1.0 + (a - 1.0) * ka_ref[...]))


LW_STREAM = 7


def _inproj_call(x3d, wret, wrw, wlora, cos, sin, mu, w0, a0, k_k, k_a, tm=512):
    bsz, s, d = x3d.shape
    width = wret.shape[1] // 4
    tm = min(tm, s)
    blk = lambda b, j: (b, j, 0)
    out_spec = pl.BlockSpec((None, tm, width), blk)
    n_out = 11
    out_sds = [jax.ShapeDtypeStruct((bsz, s, width), F32 if i == LW_STREAM else BF16) for i in range(n_out)]
    return pl.pallas_call(
        functools.partial(_inproj_kernel, width=width),
        grid=(bsz, s // tm),
        in_specs=[pl.BlockSpec((None, tm, d), blk), _const_spec(wret.shape), _const_spec(wrw.shape),
                  _const_spec(wlora.shape),
                  pl.BlockSpec((tm, cos.shape[1]), lambda b, j: (j, 0)),
                  pl.BlockSpec((tm, sin.shape[1]), lambda b, j: (j, 0)),
                  _const_spec(mu.shape), _const_spec(w0.shape), _const_spec(a0.shape),
                  _const_spec(k_k.shape), _const_spec(k_a.shape)],
        out_specs=[out_spec] * n_out,
        out_shape=out_sds,
        scratch_shapes=[pltpu.VMEM((8, wrw.shape[1]), F32)],
        compiler_params=pltpu.CompilerParams(dimension_semantics=("arbitrary", "arbitrary"),
                                             vmem_limit_bytes=VMEM_LIMIT_BYTES),
        name="inproj_prep",
    )(x3d, wret, wrw, wlora, cos, sin, mu, w0, a0, k_k, k_a)


def _split_hi_lo(x):
    hi = x.astype(BF16)
    lo = (x - hi.astype(F32)).astype(BF16)
    return hi, lo


def _mix_kernel(q_ref, k_ref, v_ref, g_ref, r_ref, km_ref, vr_ref, lw_ref, a_ref, kku_ref, gate_ref,
                dmask_ref, qdec_ref, kdec_ref, cdec_ref, rgn_g_ref, rgn_b_ref, wgn_g_ref, wgn_b_ref,
                rk_ref, o_ref,
                sret_ref, srw_ref, kkn_ref, p_ref, yret_ref, yrw_ref,
                pw_ref, tinv_ref, brp_ref, aqkv_ref, brkv_ref, pht_ref, psi_ref, dec_ref, qr_ref,
                inner_ref, qd_ref, kv_ref):
    bsz, _, width = q_ref.shape
    n_tiles = width // MXU_DIM
    n_chains = bsz * n_tiles
    hpt = HEADS_PER_TILE
    n_rows = bsz * CHUNK

    def full(ref):
        return ref[...].reshape(n_rows, ref.shape[-1]).astype(F32)

    @pl.when(pl.program_id(0) == 0)
    def _():
        sret_ref[...] = jnp.zeros_like(sret_ref)
        srw_ref[...] = jnp.zeros_like(srw_ref)

    ri = lax.broadcasted_iota(jnp.int32, (MXU_DIM, MXU_DIM), 0)
    ci = lax.broadcasted_iota(jnp.int32, (MXU_DIM, MXU_DIM), 1)
    same_head = (ri >> HEAD_SHIFT) == (ci >> HEAD_SHIFT)
    strict_lower = same_head & ((ri & (CHUNK - 1)) > (ci & (CHUNK - 1)))
    incl_lower = same_head & ((ri & (CHUNK - 1)) >= (ci & (CHUNK - 1)))
    eye = ri == ci
    one_b = jnp.ones((MXU_DIM, MXU_DIM), BF16)
    ones_blk = same_head.astype(BF16)
    lane_head = lax.broadcasted_iota(jnp.int32, (CHUNK, MXU_DIM), 1) >> HEAD_SHIFT
    ti = lax.broadcasted_iota(jnp.int32, (CHUNK, CHUNK), 0)
    tj = lax.broadcasted_iota(jnp.int32, (CHUNK, CHUNK), 1)
    tri = (ti >= tj).astype(BF16)

    def tile_rows(x):
        return jnp.concatenate([x] * hpt, axis=0)

    def stack_own(x):
        return jnp.where(same_head, tile_rows(x), jnp.zeros((), x.dtype))

    def unstack(xs):
        out = jnp.where(lane_head == 0, xs[0:CHUNK], 0.0)
        for h in range(1, hpt):
            out = out + jnp.where(lane_head == h, xs[h * CHUNK:(h + 1) * CHUNK], 0.0)
        return out

    def group_sum(x):
        outs = []
        for t in range(n_tiles):
            outs.append(_dot(x[:, t * MXU_DIM:(t + 1) * MXU_DIM].astype(BF16), ones_blk))
        return jnp.concatenate(outs, axis=1)

    kku = full(kku_ref)
    norm = jnp.maximum(jnp.sqrt(group_sum(kku * kku)), 1e-12)
    kkn = kku / norm
    kkn_ref[...] = kkn
    p_ref[...] = -(kkn * full(a_ref))

    def prepare(b, carry):
        rows = pl.ds(b * CHUNK, CHUNK)
        lw = lw_ref[b]
        lw_hi, lw_lo = _split_hi_lo(lw)
        cum = _dot(tri, lw_hi) + _dot(tri, lw_lo)
        e_in = jnp.exp(cum)
        e_ex = jnp.exp(cum - lw)
        e_neg = jnp.exp(-cum)
        e_last = e_in[CHUNK - 1:CHUNK, :]
        qt_all = kkn_ref[rows, :] * e_ex
        rt_all = r_ref[b].astype(F32) * e_in
        pt_all = p_ref[rows, :] * e_neg
        kt_all = km_ref[b].astype(F32) * e_neg
        ph_all = pt_all * e_last
        kh_all = (kt_all * e_last).astype(BF16)
        vr_all = vr_ref[b].astype(BF16)
        q_all = q_ref[b]
        k_all = k_ref[b]
        v_all = v_ref[b].astype(BF16)
        qd_all = (q_all.astype(F32) * qdec_ref[...]).astype(BF16)
        kd_all = (k_all.astype(F32) * kdec_ref[...]).astype(BF16)
        qt_b, rt_b = qt_all.astype(BF16), rt_all.astype(BF16)
        pt_b, kt_b = pt_all.astype(BF16), kt_all.astype(BF16)
        q_b, k_b = q_all.astype(BF16), k_all.astype(BF16)

        for t in range(n_tiles):
            c = b * n_tiles + t
            sl = slice(t * MXU_DIM, (t + 1) * MXU_DIM)
            qs = stack_own(qt_b[:, sl])
            rs = stack_own(rt_b[:, sl])
            pt = tile_rows(pt_b[:, sl])
            kt = tile_rows(kt_b[:, sl])
            vt = tile_rows(vr_all[:, sl])
            a_qp = jnp.where(strict_lower, _dot_nt(qs, pt), 0.0).astype(BF16)
            pw_ref[c] = a_qp
            tinv_ref[c] = jnp.where(eye, one_b, a_qp)
            a_qk = jnp.where(strict_lower, _dot_nt(qs, kt), 0.0).astype(BF16)
            aqkv_ref[c] = _dot(a_qk, vt)
            b_rk = jnp.where(incl_lower, _dot_nt(rs, kt), 0.0).astype(BF16)
            brkv_ref[c] = unstack(_dot(b_rk, vt))
            brp_ref[c] = jnp.where(incl_lower, _dot_nt(rs, pt), 0.0).astype(BF16)
            pht_ref[c] = jnp.transpose(stack_own(ph_all[:, sl])).astype(BF16)
            psi_ref[c] = jnp.where(same_head, _dot_tn(kh_all[:, sl], vr_all[:, sl]), 0.0)
            dec_ref[c] = jnp.transpose(jnp.broadcast_to(e_last[:, sl], (LANES, MXU_DIM)))
            qr_ref[c] = jnp.concatenate([qt_b[:, sl], rt_b[:, sl]], axis=0)
            scores = (_dot_nt(stack_own(q_b[:, sl]), tile_rows(k_b[:, sl])) * dmask_ref[t]).astype(BF16)
            inner_ref[c] = unstack(_dot(scores, tile_rows(v_all[:, sl])))
            qd_ref[c] = qd_all[:, sl]
            kv_ref[c] = jnp.where(same_head, _dot_tn(kd_all[:, sl], v_all[:, sl]), 0.0)
        return carry

    for b in range(bsz):
        prepare(b, 0)

    for _ in range(5):
        for c in range(n_chains):
            pwb = pw_ref[c]
            pw_ref[c] = _dot(pwb, pwb).astype(BF16)
        for c in range(n_chains):
            tinv_ref[c] = _dot(tinv_ref[c], jnp.where(eye, one_b, pw_ref[c])).astype(BF16)

    chains = [(c, pl.ds((c // n_tiles) * CHUNK, CHUNK), slice((c % n_tiles) * MXU_DIM, (c % n_tiles + 1) * MXU_DIM))
              for c in range(n_chains)]
    qrm = [_dot(qr_ref[c], srw_ref[c].astype(BF16)) for c, _, _ in chains]
    u = [_dot(tinv_ref[c], (tile_rows(qrm[c][0:CHUNK]) + aqkv_ref[c]).astype(BF16)).astype(BF16)
         for c, _, _ in chains]
    for c, rows, sl in chains:
        yrw_ref[rows, sl] = qrm[c][CHUNK:] + brkv_ref[c] + unstack(_dot(brp_ref[c], u[c]))
        dec = dec_ref[c]
        srw_ref[c] = (srw_ref[c] * jnp.concatenate([dec, dec], axis=1) + psi_ref[c]
                      + jnp.where(same_head, _dot(pht_ref[c], u[c]), 0.0))
        s = sret_ref[c]
        yret_ref[rows, sl] = inner_ref[c] + _dot(qd_ref[c], s.astype(BF16))
        sret_ref[c] = s * cdec_ref[c % n_tiles] + kv_ref[c]

    inv_hd = 1.0 / HEAD_DIM

    def group_norm(y, eps):
        mu = group_sum(y) * inv_hd
        yc = y - mu
        var = group_sum(yc * yc) * inv_hd
        return yc * lax.rsqrt(var + eps)

    g = full(g_ref)
    ret_out = g * jax.nn.sigmoid(g) * (group_norm(yret_ref[...], RET_GN_EPS) * rgn_g_ref[...] + rgn_b_ref[...])
    o_ref[:, :, 0:width] = ret_out.astype(o_ref.dtype).reshape(bsz, CHUNK, width)
    vr = full(vr_ref)
    bonus = group_sum(full(r_ref) * full(km_ref) * rk_ref[...]) * vr
    rw_out = (group_norm(yrw_ref[...], RWKV_GN_EPS) * wgn_g_ref[...] + wgn_b_ref[...] + bonus) * full(gate_ref)
    o_ref[:, :, width:2 * width] = rw_out.astype(o_ref.dtype).reshape(bsz, CHUNK, width)


def _mix_call(streams, dmask, qdec, kdec, cdec, rgn_g, rgn_b, wgn_g, wgn_b, rk):
    bsz, s, width = streams[0].shape
    n_tiles = width // MXU_DIM
    blk = lambda j: (0, j, 0)
    stream_spec = pl.BlockSpec((bsz, CHUNK, width), blk)
    consts = [dmask, qdec, kdec, cdec, rgn_g, rgn_b, wgn_g, wgn_b, rk]
    n_chains = bsz * n_tiles
    tc = bsz * CHUNK
    tile = (n_chains, MXU_DIM, MXU_DIM)
    flat = (n_chains, CHUNK, MXU_DIM)
    return pl.pallas_call(
        _mix_kernel,
        grid=(s // CHUNK,),
        in_specs=[stream_spec] * len(streams) + [_const_spec(c.shape) for c in consts],
        out_specs=pl.BlockSpec((bsz, CHUNK, 2 * width), blk),
        out_shape=jax.ShapeDtypeStruct((bsz, s, 2 * width), BF16),
        scratch_shapes=[pltpu.VMEM(tile, F32),
                        pltpu.VMEM(tile, F32),
                        pltpu.VMEM((tc, width), F32), pltpu.VMEM((tc, width), F32),
                        pltpu.VMEM((tc, width), F32), pltpu.VMEM((tc, width), F32),
                        pltpu.VMEM(tile, BF16), pltpu.VMEM(tile, BF16), pltpu.VMEM(tile, BF16),
                        pltpu.VMEM(tile, F32), pltpu.VMEM(flat, F32),
                        pltpu.VMEM(tile, BF16), pltpu.VMEM(tile, F32),
                        pltpu.VMEM((n_chains, MXU_DIM, LANES), F32),
                        pltpu.VMEM((n_chains, 2 * CHUNK, MXU_DIM), BF16),
                        pltpu.VMEM(flat, F32), pltpu.VMEM(flat, BF16), pltpu.VMEM(tile, F32)],
        compiler_params=pltpu.CompilerParams(dimension_semantics=("arbitrary",),
                                             vmem_limit_bytes=VMEM_LIMIT_BYTES),
        name="mix_recurrences",
    )(*streams, *consts)


def _retention_tables(n_heads):
    h = jnp.arange(n_heads, dtype=F32)
    log_gamma = jnp.log1p(-jnp.exp2(-5.0 - h))
    lg_lane = jnp.repeat(log_gamma, HEAD_DIM)[None, :]
    idx = jnp.arange(CHUNK, dtype=F32)[:, None]
    qdec = jnp.exp((idx + 1.0) * lg_lane)
    kdec = jnp.exp((CHUNK - 1.0 - idx) * lg_lane)
    n_tiles = n_heads // HEADS_PER_TILE
    cdec = jnp.broadcast_to(jnp.exp(CHUNK * lg_lane).reshape(n_tiles, MXU_DIM, 1),
                            (n_tiles, MXU_DIM, MXU_DIM))
    r = jnp.arange(MXU_DIM)
    rel = ((r % CHUNK)[:, None] - (r % CHUNK)[None, :]).astype(F32)
    same = (r // HEAD_DIM)[:, None] == (r // HEAD_DIM)[None, :]
    lg_rows = log_gamma.reshape(n_tiles, HEADS_PER_TILE)
    lg_rows = jnp.repeat(lg_rows, HEAD_DIM, axis=1)[:, :, None]
    dmask = jnp.where((same & (rel >= 0))[None], jnp.exp(jnp.where(rel >= 0, rel, 0.0)[None] * lg_rows), 0.0)
    return dmask, qdec, kdec, cdec


def _rotary_tables(s):
    pos = jnp.arange(s, dtype=F32)
    inv_freq = ROPE_BASE ** (-jnp.arange(0, HEAD_DIM, 2, dtype=F32) / HEAD_DIM)
    ang = pos[:, None] * inv_freq[None, :]
    cos, sin = jnp.cos(ang), jnp.sin(ang)
    reps = LANES // HEAD_DIM
    cos_t = jnp.tile(jnp.concatenate([cos, cos], axis=1), (1, reps))
    sin_t = jnp.tile(jnp.concatenate([-sin, sin], axis=1), (1, reps))
    return cos_t, sin_t


def kernel(x, p, ffn1_w_gu, ffn1_w_down, ln1_g, ln1_b, w_in, ret_gn_g, ret_gn_b, rw_mu, rw_w0, rw_w_up, rw_a0, rw_a_up, rw_g_up, rw_k_k, rw_k_a, rw_r_k, rw_gn_g, rw_gn_b, w_out, ln2_g, ln2_b, ffn2_w_gu, ffn2_w_down, ln3_g, ln3_b, ple_w_proj, ple_w_gate, ple_b_gate):
    bsz, s, d = x.shape
    depth = ffn1_w_gu.shape[0]
    alpha = (2.0 * depth) ** 0.25
    width = rw_w0.shape[1]
    n_heads = width // HEAD_DIM
    ret_cols = 4 * width
    lora = DECAY_LORA + AAA_LORA + GATE_LORA
    lora_pad = -(-lora // LANES) * LANES

    dmask, qdec, kdec, cdec = _retention_tables(n_heads)
    cos_t, sin_t = _rotary_tables(s)
    row = lambda v: v.reshape(1, -1)

    h = x.reshape(bsz * s, d)
    for i in range(depth):
        h = _ffn_call(h, ffn1_w_gu[i].astype(BF16), ffn1_w_down[i].astype(BF16),
                      row(ln1_g[i]), row(ln1_b[i]), alpha)

        wi = w_in[i].astype(BF16)
        wret = wi[:, :ret_cols]
        wrw = jnp.pad(wi[:, ret_cols:], ((0, 0), (0, lora_pad - lora)))
        mu = row(jnp.pad(rw_mu[i], (0, lora_pad - lora)))
        wlora = jnp.zeros((lora_pad, 3 * width), BF16)
        wlora = wlora.at[:DECAY_LORA, :width].set(rw_w_up[i].astype(BF16))
        wlora = wlora.at[DECAY_LORA:DECAY_LORA + AAA_LORA, width:2 * width].set(rw_a_up[i].astype(BF16))
        wlora = wlora.at[DECAY_LORA + AAA_LORA:lora, 2 * width:].set(rw_g_up[i].astype(BF16))
        streams = _inproj_call(h.reshape(bsz, s, d), wret, wrw, wlora, cos_t, sin_t, mu,
                               row(rw_w0[i]), row(rw_a0[i]), row(rw_k_k[i]), row(rw_k_a[i]))
        mixed = _mix_call(streams, dmask, qdec, kdec, cdec, row(ret_gn_g[i]), row(ret_gn_b[i]),
                          row(rw_gn_g[i]), row(rw_gn_b[i]), row(rw_r_k[i]))
        pre = (mixed.reshape(bsz * s, 2 * width), w_out[i].astype(BF16), row(ln2_g[i]), row(ln2_b[i]))
        ple = (p[i].reshape(bsz * s, -1), ple_w_proj[i].astype(BF16), ple_w_gate[i].astype(BF16),
               row(ple_b_gate[i]))
        h = _ffn_call(h, ffn2_w_gu[i].astype(BF16), ffn2_w_down[i].astype(BF16),
                      row(ln3_g[i]), row(ln3_b[i]), alpha, pre=pre, ple=ple)
    return h.reshape(bsz, s, d)
```

```python
import functools
import math

import jax
import jax.numpy as jnp
from jax import lax
from jax.experimental import pallas as pl
from jax.experimental.pallas import tpu as pltpu

D_PLE = 256
HEAD_DIM = 64
ROPE_BASE = 10000.0
DECAY_LORA = 64
AAA_LORA = 64
GATE_LORA = 160
LN_EPS = 1e-5
RET_GN_EPS = 1e-5
RWKV_GN_EPS = 64e-5
DECAY_SCALE = math.exp(-0.5)
HEAD_SHIFT = HEAD_DIM.bit_length() - 1

LANES = 128
MXU_DIM = 256
HEADS_PER_TILE = MXU_DIM // HEAD_DIM
CHUNK = 64
VMEM_LIMIT_BYTES = 56 * 1024 * 1024

F32 = jnp.float32
BF16 = jnp.bfloat16


def _dot(a, b):
    return jnp.dot(a, b, preferred_element_type=F32)


def _dot_nt(a, b):
    return lax.dot_general(a, b, (((1,), (1,)), ((), ())), preferred_element_type=F32)


def _dot_tn(a, b):
    return lax.dot_general(a, b, (((0,), (0,)), ((), ())), preferred_element_type=F32)


def _layer_norm(y, g, b):
    mu = jnp.mean(y, axis=-1, keepdims=True)
    yc = y - mu
    var = jnp.mean(yc * yc, axis=-1, keepdims=True)
    return yc * lax.rsqrt(var + LN_EPS) * g + b


def _const_spec(shape):
    nd = len(shape)
    return pl.BlockSpec(shape, lambda *_: (0,) * nd, pipeline_mode=pl.Buffered(1))


def _ffn_kernel(*refs, alpha, fc, with_pre, with_ple):
    refs = list(refs)
    x_ref = refs.pop(0)
    if with_pre:
        m_ref, wo_ref, gp_ref, bp_ref = refs[:4]
        refs = refs[4:]
    wgu_ref, wd_ref, g_ref, b_ref = refs[:4]
    refs = refs[4:]
    if with_ple:
        p_ref, wp_ref, wpg_ref, bpg_ref = refs[:4]
        refs = refs[4:]
    o_ref, act_ref = refs[:2]
    d_ff = wd_ref.shape[0]
    if with_pre:
        xs_ref = refs[2]
        xs_ref[...] = _layer_norm(alpha * x_ref[...] + _dot(m_ref[...], wo_ref[...]), gp_ref[...], bp_ref[...])
    else:
        xs_ref = x_ref
    xb = xs_ref[...].astype(BF16)
    for c in range(d_ff // fc):
        hg = _dot(xb, wgu_ref[:, c * fc:(c + 1) * fc])
        hu = _dot(xb, wgu_ref[:, d_ff + c * fc:d_ff + (c + 1) * fc])
        act_ref[:, c * fc:(c + 1) * fc] = (hg * jax.nn.sigmoid(hg) * hu).astype(BF16)
    down = _dot(act_ref[...], wd_ref[...])
    y = _layer_norm(alpha * xs_ref[...] + 0.5 * down, g_ref[...], b_ref[...])
    if with_ple:
        gate = jax.nn.sigmoid(_dot(y.astype(BF16), wpg_ref[...]) + bpg_ref[...])
        y = y + gate * _dot(p_ref[...].astype(BF16), wp_ref[...])
    o_ref[...] = y


def _ffn_call(x2d, wgu, wd, ln_g, ln_b, alpha, pre=None, ple=None, tm=512, fc=MXU_DIM):
    n, d = x2d.shape
    d_ff = wd.shape[0]
    tm = min(tm, n)
    row = lambda i: (i, 0)
    in_specs = [pl.BlockSpec((tm, d), row)]
    args = [x2d]
    if pre is not None:
        m2d, wo, gp, bp = pre
        in_specs += [pl.BlockSpec((tm, m2d.shape[1]), row), _const_spec(wo.shape),
                     _const_spec((1, d)), _const_spec((1, d))]
        args += [m2d, wo, gp, bp]
    in_specs += [_const_spec(wgu.shape), _const_spec(wd.shape), _const_spec((1, d)), _const_spec((1, d))]
    args += [wgu, wd, ln_g, ln_b]
    if ple is not None:
        p2d, wp, wpg, bpg = ple
        in_specs += [pl.BlockSpec((tm, p2d.shape[1]), row), _const_spec(wp.shape),
                     _const_spec(wpg.shape), _const_spec((1, d))]
        args += [p2d, wp, wpg, bpg]
    return pl.pallas_call(
        functools.partial(_ffn_kernel, alpha=alpha, fc=fc, with_pre=pre is not None,
                          with_ple=ple is not None),
        grid=(n // tm,),
        in_specs=in_specs,
        out_specs=pl.BlockSpec((tm, d), row),
        out_shape=jax.ShapeDtypeStruct((n, d), F32),
        scratch_shapes=[pltpu.VMEM((tm, d_ff), BF16)] + ([pltpu.VMEM((tm, d), F32)] if pre is not None else []),
        compiler_params=pltpu.CompilerParams(dimension_semantics=("arbitrary",),
                                             vmem_limit_bytes=VMEM_LIMIT_BYTES),
        name="ffn_ln",
    )(*args)


def _inproj_kernel(x_ref, wret_ref, wrw_ref, wlora_ref, cos_ref, sin_ref, mu_ref, w0_ref, a0_ref,
                   kk_ref, ka_ref,
                   q_ref, k_ref, v_ref, g_ref, r_ref, km_ref, vr_ref, lw_ref, a_ref, kku_ref,
                   gate_ref, carry_ref, *, width):
    tm = x_ref.shape[0]

    @pl.when(pl.program_id(1) == 0)
    def _():
        carry_ref[...] = jnp.zeros_like(carry_ref)

    xb = x_ref[...].astype(BF16)

    lane = lax.broadcasted_iota(jnp.int32, (tm, width), 1)
    first_half = (lane & (HEAD_DIM // 2)) == 0
    reps = width // cos_ref.shape[1]
    cos = jnp.concatenate([cos_ref[...]] * reps, axis=1)
    sin = jnp.concatenate([sin_ref[...]] * reps, axis=1)

    def rotary(t):
        swapped = jnp.where(first_half, pltpu.roll(t, width - HEAD_DIM // 2, 1),
                            pltpu.roll(t, HEAD_DIM // 2, 1))
        return t * cos + swapped * sin

    def put(ref, val):
        ref[...] = val.astype(ref.dtype)

    def shifted(lo_col, hi_col):
        z = _dot(xb, wrw_ref[:, lo_col:hi_col])
        first = carry_ref[0:1, lo_col:hi_col]
        carry_ref[0:1, lo_col:hi_col] = z[tm - 1:tm, :]
        rowid = lax.broadcasted_iota(jnp.int32, z.shape, 0)
        prev = jnp.where(rowid == 0, first, pltpu.roll(z, 1, 0))
        return z + (prev - z) * mu_ref[:, lo_col:hi_col]

    lo = shifted(3 * width, wrw_ref.shape[1])
    ll = lax.broadcasted_iota(jnp.int32, lo.shape, 1)
    act = jnp.where(ll < DECAY_LORA, jnp.tanh(lo),
                    jnp.where(ll < DECAY_LORA + AAA_LORA, lo, jax.nn.sigmoid(lo)))
    up = _dot(act.astype(BF16), wlora_ref[...])
    lw_ref[...] = -DECAY_SCALE * jax.nn.sigmoid(w0_ref[...] + up[:, 0 * width:1 * width])
    a = jax.nn.sigmoid(a0_ref[...] + up[:, 1 * width:2 * width])
    put(gate_ref, up[:, 2 * width:3 * width])
    put(a_ref, a)
    kr = shifted(1 * width, 2 * width)
    put(kku_ref, kr * kk_ref[...])
    put(km_ref, kr * (1.0 + (a - 1.0) * ka_ref[...]))
    put(r_ref, shifted(0 * width, 1 * width))
    put(vr_ref, shifted(2 * width, 3 * width))

    put(q_ref, rotary(_dot(xb, wret_ref[:, 0 * width:1 * width])))
    put(k_ref, rotary(_dot(xb, wret_ref[:, 1 * width:2 * width])) * (HEAD_DIM ** -0.5))
    put(v_ref, _dot(xb, wret_ref[:, 2 * width:3 * width]))
    put(g_ref, _dot(xb, wret_ref[:, 3 * width:4 * width]))


LW_STREAM = 7


def _inproj_call(x3d, wret, wrw, wlora, cos, sin, mu, w0, a0, k_k, k_a, tm=512):
    bsz, s, d = x3d.shape
    width = wret.shape[1] // 4
    tm = min(tm, s)
    blk = lambda b, j: (b, j, 0)
    out_spec = pl.BlockSpec((None, tm, width), blk)
    n_out = 11
    out_sds = [jax.ShapeDtypeStruct((bsz, s, width), F32 if i == LW_STREAM else BF16) for i in range(n_out)]
    return pl.pallas_call(
        functools.partial(_inproj_kernel, width=width),
        grid=(bsz, s // tm),
        in_specs=[pl.BlockSpec((None, tm, d), blk), _const_spec(wret.shape), _const_spec(wrw.shape),
                  _const_spec(wlora.shape),
                  pl.BlockSpec((tm, cos.shape[1]), lambda b, j: (j, 0)),
                  pl.BlockSpec((tm, sin.shape[1]), lambda b, j: (j, 0)),
                  _const_spec(mu.shape), _const_spec(w0.shape), _const_spec(a0.shape),
                  _const_spec(k_k.shape), _const_spec(k_a.shape)],
        out_specs=[out_spec] * n_out,
        out_shape=out_sds,
        scratch_shapes=[pltpu.VMEM((8, wrw.shape[1]), F32)],
        compiler_params=pltpu.CompilerParams(dimension_semantics=("arbitrary", "arbitrary"),
                                             vmem_limit_bytes=VMEM_LIMIT_BYTES),
        name="inproj_prep",
    )(x3d, wret, wrw, wlora, cos, sin, mu, w0, a0, k_k, k_a)


def _split_hi_lo(x):
    hi = x.astype(BF16)
    lo = (x - hi.astype(F32)).astype(BF16)
    return hi, lo


def _mix_kernel(q_ref, k_ref, v_ref, g_ref, r_ref, km_ref, vr_ref, lw_ref, a_ref, kku_ref, gate_ref,
                dmask_ref, qdec_ref, kdec_ref, cdec_ref, rgn_g_ref, rgn_b_ref, wgn_g_ref, wgn_b_ref,
                rk_ref, o_ref,
                sret_ref, srw_ref, kkn_ref, p_ref, yret_ref, yrw_ref,
                pw_ref, tinv_ref, brp_ref, aqkv_ref, brkv_ref, pht_ref, psi_ref, dec_ref, qr_ref,
                inner_ref, qd_ref, kv_ref):
    bsz, _, width = q_ref.shape
    n_tiles = width // MXU_DIM
    n_chains = bsz * n_tiles
    hpt = HEADS_PER_TILE
    n_rows = bsz * CHUNK

    def full(ref):
        return ref[...].reshape(n_rows, ref.shape[-1]).astype(F32)

    @pl.when(pl.program_id(0) == 0)
    def _():
        sret_ref[...] = jnp.zeros_like(sret_ref)
        srw_ref[...] = jnp.zeros_like(srw_ref)

    ri = lax.broadcasted_iota(jnp.int32, (MXU_DIM, MXU_DIM), 0)
    ci = lax.broadcasted_iota(jnp.int32, (MXU_DIM, MXU_DIM), 1)
    same_head = (ri >> HEAD_SHIFT) == (ci >> HEAD_SHIFT)
    strict_lower = same_head & ((ri & (CHUNK - 1)) > (ci & (CHUNK - 1)))
    incl_lower = same_head & ((ri & (CHUNK - 1)) >= (ci & (CHUNK - 1)))
    eye = ri == ci
    one_b = jnp.ones((MXU_DIM, MXU_DIM), BF16)
    ones_blk = same_head.astype(BF16)
    lane_head = lax.broadcasted_iota(jnp.int32, (CHUNK, MXU_DIM), 1) >> HEAD_SHIFT
    ti = lax.broadcasted_iota(jnp.int32, (CHUNK, CHUNK), 0)
    tj = lax.broadcasted_iota(jnp.int32, (CHUNK, CHUNK), 1)
    tri = (ti >= tj).astype(BF16)

    def tile_rows(x):
        return jnp.concatenate([x] * hpt, axis=0)

    def stack_own(x):
        return jnp.where(same_head, tile_rows(x), jnp.zeros((), x.dtype))

    def unstack(xs):
        out = jnp.where(lane_head == 0, xs[0:CHUNK], 0.0)
        for h in range(1, hpt):
            out = out + jnp.where(lane_head == h, xs[h * CHUNK:(h + 1) * CHUNK], 0.0)
        return out

    def group_sum(x):
        outs = []
        for t in range(n_tiles):
            outs.append(_dot(x[:, t * MXU_DIM:(t + 1) * MXU_DIM].astype(BF16), ones_blk))
        return jnp.concatenate(outs, axis=1)

    kku = full(kku_ref)
    norm = jnp.maximum(jnp.sqrt(group_sum(kku * kku)), 1e-12)
    kkn = kku / norm
    kkn_ref[...] = kkn
    p_ref[...] = -(kkn * full(a_ref))

    def prepare(b, carry):
        rows = pl.ds(b * CHUNK, CHUNK)
        lw = lw_ref[b]
        lw_hi, lw_lo = _split_hi_lo(lw)
        cum = _dot(tri, lw_hi) + _dot(tri, lw_lo)
        e_in = jnp.exp(cum)
        e_ex = jnp.exp(cum - lw)
        e_neg = jnp.exp(-cum)
        e_last = e_in[CHUNK - 1:CHUNK, :]
        qt_all = kkn_ref[rows, :] * e_ex
        rt_all = r_ref[b].astype(F32) * e_in
        pt_all = p_ref[rows, :] * e_neg
        kt_all = km_ref[b].astype(F32) * e_neg
        ph_all = pt_all * e_last
        kh_all = (kt_all * e_last).astype(BF16)
        vr_all = vr_ref[b].astype(BF16)
        q_all = q_ref[b]
        k_all = k_ref[b]
        v_all = v_ref[b].astype(BF16)
        qd_all = (q_all.astype(F32) * qdec_ref[...]).astype(BF16)
        kd_all = (k_all.astype(F32) * kdec_ref[...]).astype(BF16)
        qt_b, rt_b = qt_all.astype(BF16), rt_all.astype(BF16)
        pt_b, kt_b = pt_all.astype(BF16), kt_all.astype(BF16)
        q_b, k_b = q_all.astype(BF16), k_all.astype(BF16)

        for t in range(n_tiles):
            c = b * n_tiles + t
            sl = slice(t * MXU_DIM, (t + 1) * MXU_DIM)
            qs = stack_own(qt_b[:, sl])
            rs = stack_own(rt_b[:, sl])
            pt = tile_rows(pt_b[:, sl])
            kt = tile_rows(kt_b[:, sl])
            vt = tile_rows(vr_all[:, sl])
            a_qp = jnp.where(strict_lower, _dot_nt(qs, pt), 0.0).astype(BF16)
            pw_ref[c] = a_qp
            tinv_ref[c] = jnp.where(eye, one_b, a_qp)
            a_qk = jnp.where(strict_lower, _dot_nt(qs, kt), 0.0).astype(BF16)
            aqkv_ref[c] = _dot(a_qk, vt)
            b_rk = jnp.where(incl_lower, _dot_nt(rs, kt), 0.0).astype(BF16)
            brkv_ref[c] = unstack(_dot(b_rk, vt))
            brp_ref[c] = jnp.where(incl_lower, _dot_nt(rs, pt), 0.0).astype(BF16)
            pht_ref[c] = jnp.transpose(stack_own(ph_all[:, sl])).astype(BF16)
            psi_ref[c] = jnp.where(same_head, _dot_tn(kh_all[:, sl], vr_all[:, sl]), 0.0)
            dec_ref[c] = jnp.transpose(jnp.broadcast_to(e_last[:, sl], (LANES, MXU_DIM)))
            qr_ref[c] = jnp.concatenate([qt_b[:, sl], rt_b[:, sl]], axis=0)
            scores = (_dot_nt(stack_own(q_b[:, sl]), tile_rows(k_b[:, sl])) * dmask_ref[t]).astype(BF16)
            inner_ref[c] = unstack(_dot(scores, tile_rows(v_all[:, sl])))
            qd_ref[c] = qd_all[:, sl]
            kv_ref[c] = jnp.where(same_head, _dot_tn(kd_all[:, sl], v_all[:, sl]), 0.0)
        return carry

    for b in range(bsz):
        prepare(b, 0)

    for _ in range(5):
        for c in range(n_chains):
            pwb = pw_ref[c]
            pw_ref[c] = _dot(pwb, pwb).astype(BF16)
        for c in range(n_chains):
            tinv_ref[c] = _dot(tinv_ref[c], jnp.where(eye, one_b, pw_ref[c])).astype(BF16)

    chains = [(c, pl.ds((c // n_tiles) * CHUNK, CHUNK), slice((c % n_tiles) * MXU_DIM, (c % n_tiles + 1) * MXU_DIM))
              for c in range(n_chains)]
    qrm = [_dot(qr_ref[c], srw_ref[c].astype(BF16)) for c, _, _ in chains]
    u = [_dot(tinv_ref[c], (tile_rows(qrm[c][0:CHUNK]) + aqkv_ref[c]).astype(BF16)).astype(BF16)
         for c, _, _ in chains]
    for c, rows, sl in chains:
        yrw_ref[rows, sl] = qrm[c][CHUNK:] + brkv_ref[c] + unstack(_dot(brp_ref[c], u[c]))
        dec = dec_ref[c]
        srw_ref[c] = (srw_ref[c] * jnp.concatenate([dec, dec], axis=1) + psi_ref[c]
                      + jnp.where(same_head, _dot(pht_ref[c], u[c]), 0.0))
        s = sret_ref[c]
        yret_ref[rows, sl] = inner_ref[c] + _dot(qd_ref[c], s.astype(BF16))
        sret_ref[c] = s * cdec_ref[c % n_tiles] + kv_ref[c]

    inv_hd = 1.0 / HEAD_DIM

    def group_norm(y, eps):
        mu = group_sum(y) * inv_hd
        yc = y - mu
        var = group_sum(yc * yc) * inv_hd
        return yc * lax.rsqrt(var + eps)

    g = full(g_ref)
    ret_out = g * jax.nn.sigmoid(g) * (group_norm(yret_ref[...], RET_GN_EPS) * rgn_g_ref[...] + rgn_b_ref[...])
    o_ref[:, :, 0:width] = ret_out.astype(o_ref.dtype).reshape(bsz, CHUNK, width)
    vr = full(vr_ref)
    bonus = group_sum(full(r_ref) * full(km_ref) * rk_ref[...]) * vr
    rw_out = (group_norm(yrw_ref[...], RWKV_GN_EPS) * wgn_g_ref[...] + wgn_b_ref[...] + bonus) * full(gate_ref)
    o_ref[:, :, width:2 * width] = rw_out.astype(o_ref.dtype).reshape(bsz, CHUNK, width)


def _mix_call(streams, dmask, qdec, kdec, cdec, rgn_g, rgn_b, wgn_g, wgn_b, rk):
    bsz, s, width = streams[0].shape
    n_tiles = width // MXU_DIM
    blk = lambda j: (0, j, 0)
    stream_spec = pl.BlockSpec((bsz, CHUNK, width), blk)
    consts = [dmask, qdec, kdec, cdec, rgn_g, rgn_b, wgn_g, wgn_b, rk]
    n_chains = bsz * n_tiles
    tc = bsz * CHUNK
    tile = (n_chains, MXU_DIM, MXU_DIM)
    flat = (n_chains, CHUNK, MXU_DIM)
    return pl.pallas_call(
        _mix_kernel,
        grid=(s // CHUNK,),
        in_specs=[stream_spec] * len(streams) + [_const_spec(c.shape) for c in consts],
        out_specs=pl.BlockSpec((bsz, CHUNK, 2 * width), blk),
        out_shape=jax.ShapeDtypeStruct((bsz, s, 2 * width), BF16),
        scratch_shapes=[pltpu.VMEM(tile, F32),
                        pltpu.VMEM(tile, F32),
                        pltpu.VMEM((tc, width), F32), pltpu.VMEM((tc, width), F32),
                        pltpu.VMEM((tc, width), F32), pltpu.VMEM((tc, width), F32),
                        pltpu.VMEM(tile, BF16), pltpu.VMEM(tile, BF16), pltpu.VMEM(tile, BF16),
                        pltpu.VMEM(tile, F32), pltpu.VMEM(flat, F32),
                        pltpu.VMEM(tile, BF16), pltpu.VMEM(tile, F32),
                        pltpu.VMEM((n_chains, MXU_DIM, LANES), F32),
                        pltpu.VMEM((n_chains, 2 * CHUNK, MXU_DIM), BF16),
                        pltpu.VMEM(flat, F32), pltpu.VMEM(flat, BF16), pltpu.VMEM(tile, F32)],
        compiler_params=pltpu.CompilerParams(dimension_semantics=("arbitrary",),
                                             vmem_limit_bytes=VMEM_LIMIT_BYTES),
        name="mix_recurrences",
    )(*streams, *consts)


def _retention_tables(n_heads):
    h = jnp.arange(n_heads, dtype=F32)
    log_gamma = jnp.log1p(-jnp.exp2(-5.0 - h))
    lg_lane = jnp.repeat(log_gamma, HEAD_DIM)[None, :]
    idx = jnp.arange(CHUNK, dtype=F32)[:, None]
    qdec = jnp.exp((idx + 1.0) * lg_lane)
    kdec = jnp.exp((CHUNK - 1.0 - idx) * lg_lane)
    n_tiles = n_heads // HEADS_PER_TILE
    cdec = jnp.broadcast_to(jnp.exp(CHUNK * lg_lane).reshape(n_tiles, MXU_DIM, 1),
                            (n_tiles, MXU_DIM, MXU_DIM))
    r = jnp.arange(MXU_DIM)
    rel = ((r % CHUNK)[:, None] - (r % CHUNK)[None, :]).astype(F32)
    same = (r // HEAD_DIM)[:, None] == (r // HEAD_DIM)[None, :]
    lg_rows = log_gamma.reshape(n_tiles, HEADS_PER_TILE)
    lg_rows = jnp.repeat(lg_rows, HEAD_DIM, axis=1)[:, :, None]
    dmask = jnp.where((same & (rel >= 0))[None], jnp.exp(jnp.where(rel >= 0, rel, 0.0)[None] * lg_rows), 0.0)
    return dmask, qdec, kdec, cdec


def _rotary_tables(s):
    pos = jnp.arange(s, dtype=F32)
    inv_freq = ROPE_BASE ** (-jnp.arange(0, HEAD_DIM, 2, dtype=F32) / HEAD_DIM)
    ang = pos[:, None] * inv_freq[None, :]
    cos, sin = jnp.cos(ang), jnp.sin(ang)
    reps = LANES // HEAD_DIM
    cos_t = jnp.tile(jnp.concatenate([cos, cos], axis=1), (1, reps))
    sin_t = jnp.tile(jnp.concatenate([-sin, sin], axis=1), (1, reps))
    return cos_t, sin_t


def kernel(x, p, ffn1_w_gu, ffn1_w_down, ln1_g, ln1_b, w_in, ret_gn_g, ret_gn_b, rw_mu, rw_w0, rw_w_up, rw_a0, rw_a_up, rw_g_up, rw_k_k, rw_k_a, rw_r_k, rw_gn_g, rw_gn_b, w_out, ln2_g, ln2_b, ffn2_w_gu, ffn2_w_down, ln3_g, ln3_b, ple_w_proj, ple_w_gate, ple_b_gate):
    bsz, s, d = x.shape
    depth = ffn1_w_gu.shape[0]
    alpha = (2.0 * depth) ** 0.25
    width = rw_w0.shape[1]
    n_heads = width // HEAD_DIM
    ret_cols = 4 * width
    lora = DECAY_LORA + AAA_LORA + GATE_LORA
    lora_pad = -(-lora // LANES) * LANES

    dmask, qdec, kdec, cdec = _retention_tables(n_heads)
    cos_t, sin_t = _rotary_tables(s)
    row = lambda v: v.reshape(1, -1)

    h = x.reshape(bsz * s, d)
    for i in range(depth):
        h = _ffn_call(h, ffn1_w_gu[i].astype(BF16), ffn1_w_down[i].astype(BF16),
                      row(ln1_g[i]), row(ln1_b[i]), alpha)

        wi = w_in[i].astype(BF16)
        wret = wi[:, :ret_cols]
        wrw = jnp.pad(wi[:, ret_cols:], ((0, 0), (0, lora_pad - lora)))
        mu = row(jnp.pad(rw_mu[i], (0, lora_pad - lora)))
        wlora = jnp.zeros((lora_pad, 3 * width), BF16)
        wlora = wlora.at[:DECAY_LORA, :width].set(rw_w_up[i].astype(BF16))
        wlora = wlora.at[DECAY_LORA:DECAY_LORA + AAA_LORA, width:2 * width].set(rw_a_up[i].astype(BF16))
        wlora = wlora.at[DECAY_LORA + AAA_LORA:lora, 2 * width:].set(rw_g_up[i].astype(BF16))
        streams = _inproj_call(h.reshape(bsz, s, d), wret, wrw, wlora, cos_t, sin_t, mu,
                               row(rw_w0[i]), row(rw_a0[i]), row(rw_k_k[i]), row(rw_k_a[i]))
        mixed = _mix_call(streams, dmask, qdec, kdec, cdec, row(ret_gn_g[i]), row(ret_gn_b[i]),
                          row(rw_gn_g[i]), row(rw_gn_b[i]), row(rw_r_k[i]))
        pre = (mixed.reshape(bsz * s, 2 * width), w_out[i].astype(BF16), row(ln2_g[i]), row(ln2_b[i]))
        ple = (p[i].reshape(bsz * s, -1), ple_w_proj[i].astype(BF16), ple_w_gate[i].astype(BF16),
               row(ple_b_gate[i]))
        h = _ffn_call(h, ffn2_w_gu[i].astype(BF16), ffn2_w_down[i].astype(BF16),
                      row(ln3_g[i]), row(ln3_b[i]), alpha, pre=pre, ple=ple)
    return h.reshape(bsz, s, d)
```

```python
import functools
import math

import jax
import jax.numpy as jnp
from jax import lax
from jax.experimental import pallas as pl
from jax.experimental.pallas import tpu as pltpu

D_PLE = 256
HEAD_DIM = 64
ROPE_BASE = 10000.0
DECAY_LORA = 64
AAA_LORA = 64
GATE_LORA = 160
LN_EPS = 1e-5
RET_GN_EPS = 1e-5
RWKV_GN_EPS = 64e-5
DECAY_SCALE = math.exp(-0.5)
HEAD_SHIFT = HEAD_DIM.bit_length() - 1

LANES = 128
MXU_DIM = 256
HEADS_PER_TILE = MXU_DIM // HEAD_DIM
CHUNK = 64
VMEM_LIMIT_BYTES = 56 * 1024 * 1024

F32 = jnp.float32
BF16 = jnp.bfloat16


def _dot(a, b):
    return jnp.dot(a, b, preferred_element_type=F32)


def _dot_nt(a, b):
    return lax.dot_general(a, b, (((1,), (1,)), ((), ())), preferred_element_type=F32)


def _dot_tn(a, b):
    return lax.dot_general(a, b, (((0,), (0,)), ((), ())), preferred_element_type=F32)


def _layer_norm(y, g, b):
    mu = jnp.mean(y, axis=-1, keepdims=True)
    yc = y - mu
    var = jnp.mean(yc * yc, axis=-1, keepdims=True)
    return yc * lax.rsqrt(var + LN_EPS) * g + b


def _const_spec(shape):
    nd = len(shape)
    return pl.BlockSpec(shape, lambda *_: (0,) * nd, pipeline_mode=pl.Buffered(1))


def _ffn_kernel(*refs, alpha, fc, with_pre, with_ple):
    refs = list(refs)
    x_ref = refs.pop(0)
    if with_pre:
        m_ref, wo_ref, gp_ref, bp_ref = refs[:4]
        refs = refs[4:]
    wgu_ref, wd_ref, g_ref, b_ref = refs[:4]
    refs = refs[4:]
    if with_ple:
        p_ref, wp_ref, wpg_ref, bpg_ref = refs[:4]
        refs = refs[4:]
    o_ref, act_ref = refs[:2]
    d_ff = wd_ref.shape[0]
    if with_pre:
        xs_ref = refs[2]
        xs_ref[...] = _layer_norm(alpha * x_ref[...] + _dot(m_ref[...], wo_ref[...]), gp_ref[...], bp_ref[...])
    else:
        xs_ref = x_ref
    xb = xs_ref[...].astype(BF16)
    for c in range(d_ff // fc):
        hg = _dot(xb, wgu_ref[:, c * fc:(c + 1) * fc])
        hu = _dot(xb, wgu_ref[:, d_ff + c * fc:d_ff + (c + 1) * fc])
        act_ref[:, c * fc:(c + 1) * fc] = (hg * jax.nn.sigmoid(hg) * hu).astype(BF16)
    down = _dot(act_ref[...], wd_ref[...])
    y = _layer_norm(alpha * xs_ref[...] + 0.5 * down, g_ref[...], b_ref[...])
    if with_ple:
        gate = jax.nn.sigmoid(_dot(y.astype(BF16), wpg_ref[...]) + bpg_ref[...])
        y = y + gate * _dot(p_ref[...].astype(BF16), wp_ref[...])
    o_ref[...] = y


def _ffn_call(x2d, wgu, wd, ln_g, ln_b, alpha, pre=None, ple=None, tm=512, fc=MXU_DIM):
    n, d = x2d.shape
    d_ff = wd.shape[0]
    tm = min(tm, n)
    row = lambda i: (i, 0)
    in_specs = [pl.BlockSpec((tm, d), row)]
    args = [x2d]
    if pre is not None:
        m2d, wo, gp, bp = pre
        in_specs += [pl.BlockSpec((tm, m2d.shape[1]), row), _const_spec(wo.shape),
                     _const_spec((1, d)), _const_spec((1, d))]
        args += [m2d, wo, gp, bp]
    in_specs += [_const_spec(wgu.shape), _const_spec(wd.shape), _const_spec((1, d)), _const_spec((1, d))]
    args += [wgu, wd, ln_g, ln_b]
    if ple is not None:
        p2d, wp, wpg, bpg = ple
        in_specs += [pl.BlockSpec((tm, p2d.shape[1]), row), _const_spec(wp.shape),
                     _const_spec(wpg.shape), _const_spec((1, d))]
        args += [p2d, wp, wpg, bpg]
    return pl.pallas_call(
        functools.partial(_ffn_kernel, alpha=alpha, fc=fc, with_pre=pre is not None,
                          with_ple=ple is not None),
        grid=(n // tm,),
        in_specs=in_specs,
        out_specs=pl.BlockSpec((tm, d), row),
        out_shape=jax.ShapeDtypeStruct((n, d), F32),
        scratch_shapes=[pltpu.VMEM((tm, d_ff), BF16)] + ([pltpu.VMEM((tm, d), F32)] if pre is not None else []),
        compiler_params=pltpu.CompilerParams(dimension_semantics=("arbitrary",),
                                             vmem_limit_bytes=VMEM_LIMIT_BYTES),
        name="ffn_ln",
    )(*args)


def _group_sum(x, ones_blk):
    outs = [_dot(x[:, t * MXU_DIM:(t + 1) * MXU_DIM].astype(BF16), ones_blk)
            for t in range(x.shape[1] // MXU_DIM)]
    return jnp.concatenate(outs, axis=1)


def _inproj_kernel(x_ref, wret_ref, wrw_ref, wlora_ref, cos_ref, sin_ref, mu_ref, w0_ref, a0_ref,
                   kk_ref, ka_ref,
                   q_ref, k_ref, v_ref, g_ref, r_ref, km_ref, vr_ref, lw_ref, a_ref, kku_ref,
                   gate_ref, carry_ref, *, width):
    tm = x_ref.shape[0]

    @pl.when(pl.program_id(1) == 0)
    def _():
        carry_ref[...] = jnp.zeros_like(carry_ref)

    xb = x_ref[...].astype(BF16)

    lane = lax.broadcasted_iota(jnp.int32, (tm, width), 1)
    first_half = (lane & (HEAD_DIM // 2)) == 0
    reps = width // cos_ref.shape[1]
    cos = jnp.concatenate([cos_ref[...]] * reps, axis=1)
    sin = jnp.concatenate([sin_ref[...]] * reps, axis=1)

    def rotary(t):
        swapped = jnp.where(first_half, pltpu.roll(t, width - HEAD_DIM // 2, 1),
                            pltpu.roll(t, HEAD_DIM // 2, 1))
        return t * cos + swapped * sin

    def put(ref, val):
        ref[...] = val.astype(ref.dtype)

    def shifted(lo_col, hi_col):
        z = _dot(xb, wrw_ref[:, lo_col:hi_col])
        first = carry_ref[0:1, lo_col:hi_col]
        carry_ref[0:1, lo_col:hi_col] = z[tm - 1:tm, :]
        rowid = lax.broadcasted_iota(jnp.int32, z.shape, 0)
        prev = jnp.where(rowid == 0, first, pltpu.roll(z, 1, 0))
        return z + (prev - z) * mu_ref[:, lo_col:hi_col]

    lo = shifted(3 * width, wrw_ref.shape[1])
    ll = lax.broadcasted_iota(jnp.int32, lo.shape, 1)
    act = jnp.where(ll < DECAY_LORA, jnp.tanh(lo),
                    jnp.where(ll < DECAY_LORA + AAA_LORA, lo, jax.nn.sigmoid(lo)))
    up = _dot(act.astype(BF16), wlora_ref[...])
    lw_ref[...] = -DECAY_SCALE * jax.nn.sigmoid(w0_ref[...] + up[:, 0 * width:1 * width])
    a = jax.nn.sigmoid(a0_ref[...] + up[:, 1 * width:2 * width])
    put(gate_ref, up[:, 2 * width:3 * width])
    put(a_ref, a)
    kr = shifted(1 * width, 2 * width)
    put(kku_ref, kr * kk_ref[...])
    put(km_ref, kr * (1.0 + (a - 1.0) * ka_ref[...]))
    put(r_ref, shifted(0 * width, 1 * width))
    put(vr_ref, shifted(2 * width, 3 * width))

    put(q_ref, rotary(_dot(xb, wret_ref[:, 0 * width:1 * width])))
    put(k_ref, rotary(_dot(xb, wret_ref[:, 1 * width:2 * width])) * (HEAD_DIM ** -0.5))
    put(v_ref, _dot(xb, wret_ref[:, 2 * width:3 * width]))
    put(g_ref, _dot(xb, wret_ref[:, 3 * width:4 * width]))


LW_STREAM = 7


def _inproj_call(x3d, wret, wrw, wlora, cos, sin, mu, w0, a0, k_k, k_a, tm=512):
    bsz, s, d = x3d.shape
    width = wret.shape[1] // 4
    tm = min(tm, s)
    blk = lambda b, j: (b, j, 0)
    out_spec = pl.BlockSpec((None, tm, width), blk)
    n_out = 11
    out_sds = [jax.ShapeDtypeStruct((bsz, s, width), F32 if i == LW_STREAM else BF16) for i in range(n_out)]
    return pl.pallas_call(
        functools.partial(_inproj_kernel, width=width),
        grid=(bsz, s // tm),
        in_specs=[pl.BlockSpec((None, tm, d), blk), _const_spec(wret.shape), _const_spec(wrw.shape),
                  _const_spec(wlora.shape),
                  pl.BlockSpec((tm, cos.shape[1]), lambda b, j: (j, 0)),
                  pl.BlockSpec((tm, sin.shape[1]), lambda b, j: (j, 0)),
                  _const_spec(mu.shape), _const_spec(w0.shape), _const_spec(a0.shape),
                  _const_spec(k_k.shape), _const_spec(k_a.shape)],
        out_specs=[out_spec] * n_out,
        out_shape=out_sds,
        scratch_shapes=[pltpu.VMEM((8, wrw.shape[1]), F32)],
        compiler_params=pltpu.CompilerParams(dimension_semantics=("arbitrary", "arbitrary"),
                                             vmem_limit_bytes=VMEM_LIMIT_BYTES),
        name="inproj_prep",
    )(x3d, wret, wrw, wlora, cos, sin, mu, w0, a0, k_k, k_a)


def _split_hi_lo(x):
    hi = x.astype(BF16)
    lo = (x - hi.astype(F32)).astype(BF16)
    return hi, lo


def _mix_kernel(q_ref, k_ref, v_ref, r_ref, km_ref, vr_ref, lw_ref, a_ref, kku_ref,
                g_ref, gate_ref, rl_ref, kml_ref, vrl_ref,
                dmask_ref, qdec_ref, kdec_ref, cdec_ref, rgn_g_ref, rgn_b_ref, wgn_g_ref, wgn_b_ref,
                rk_ref, o_ref,
                sret_ref, srw_ref, kkn_ref, p_ref, yret_ref, yrw_ref,
                pw_ref, tinv_ref, brp_ref, av_ref, pkt_ref, dec_ref, qr_ref, inner_ref, qd_ref, kv_ref):
    bsz, _, width = q_ref.shape
    n_tiles = width // MXU_DIM
    n_chains = bsz * n_tiles
    hpt = HEADS_PER_TILE
    n_rows = bsz * CHUNK

    def full(ref):
        return ref[...].reshape(n_rows, ref.shape[-1]).astype(F32)

    @pl.when(pl.program_id(0) == 0)
    def _():
        sret_ref[...] = jnp.zeros_like(sret_ref)
        srw_ref[...] = jnp.zeros_like(srw_ref)
        yret_ref[...] = jnp.zeros_like(yret_ref)
        yrw_ref[...] = jnp.zeros_like(yrw_ref)

    ri = lax.broadcasted_iota(jnp.int32, (MXU_DIM, MXU_DIM), 0)
    ci = lax.broadcasted_iota(jnp.int32, (MXU_DIM, MXU_DIM), 1)
    same_head = (ri >> HEAD_SHIFT) == (ci >> HEAD_SHIFT)
    ones_blk = same_head.astype(BF16)
    cr = lax.broadcasted_iota(jnp.int32, (CHUNK, MXU_DIM), 0)
    cj = lax.broadcasted_iota(jnp.int32, (CHUNK, MXU_DIM), 1) & (HEAD_DIM - 1)
    strict_lower, incl_lower = cr > cj, cr >= cj
    eye_c = (cr == cj).astype(F32)
    ti = lax.broadcasted_iota(jnp.int32, (CHUNK, CHUNK), 0)
    tj = lax.broadcasted_iota(jnp.int32, (CHUNK, CHUNK), 1)
    tri = (ti >= tj).astype(BF16)

    def block_diag(x):
        return jnp.where(same_head, jnp.concatenate([x] * hpt, axis=0), jnp.zeros((), x.dtype))

    def group_sum(x):
        return _group_sum(x, ones_blk)

    kku = full(kku_ref)
    norm = jnp.maximum(jnp.sqrt(group_sum(kku * kku)), 1e-12)
    kkn = kku / norm
    kkn_ref[...] = kkn
    p_ref[...] = -(kkn * full(a_ref))

    def prepare(b, carry):
        rows = pl.ds(b * CHUNK, CHUNK)
        lw = lw_ref[b]
        lw_hi, lw_lo = _split_hi_lo(lw)
        cum = _dot(tri, lw_hi) + _dot(tri, lw_lo)
        e_in = jnp.exp(cum)
        e_ex = jnp.exp(cum - lw)
        e_neg = jnp.exp(-cum)
        e_last = e_in[CHUNK - 1:CHUNK, :]
        qt_all = kkn_ref[rows, :] * e_ex
        rt_all = r_ref[b].astype(F32) * e_in
        pt_all = p_ref[rows, :] * e_neg
        kt_all = km_ref[b].astype(F32) * e_neg
        ph_all = pt_all * e_last
        kh_all = kt_all * e_last
        vr_all = vr_ref[b].astype(BF16)
        q_all = q_ref[b]
        k_all = k_ref[b]
        v_all = v_ref[b].astype(BF16)
        qd_all = (q_all.astype(F32) * qdec_ref[...]).astype(BF16)
        kd_all = (k_all.astype(F32) * kdec_ref[...]).astype(BF16)
        qt_b, rt_b = qt_all.astype(BF16), rt_all.astype(BF16)
        pt_b, kt_b = pt_all.astype(BF16), kt_all.astype(BF16)
        q_b, k_b = q_all.astype(BF16), k_all.astype(BF16)

        for t in range(n_tiles):
            c = b * n_tiles + t
            sl = slice(t * MXU_DIM, (t + 1) * MXU_DIM)
            qr = jnp.concatenate([qt_b[:, sl], rt_b[:, sl]], axis=0)
            qr_ref[c] = qr
            g_p = _dot_nt(qr, block_diag(pt_b[:, sl]))
            g_k = _dot_nt(qr, block_diag(kt_b[:, sl]))
            a_qp = jnp.where(strict_lower, g_p[0:CHUNK], 0.0)
            pw_ref[c] = a_qp.astype(BF16)
            tinv_ref[c] = a_qp + eye_c
            brp_ref[c] = jnp.where(incl_lower, g_p[CHUNK:], 0.0).astype(BF16)
            ab = jnp.concatenate([jnp.where(strict_lower, g_k[0:CHUNK], 0.0),
                                  jnp.where(incl_lower, g_k[CHUNK:], 0.0)], axis=0).astype(BF16)
            av_ref[c] = _dot(ab, block_diag(vr_all[:, sl]))
            pkt_ref[c] = jnp.transpose(jnp.concatenate([ph_all[:, sl], kh_all[:, sl]], axis=0)).astype(BF16)
            dec_ref[c] = jnp.transpose(jnp.broadcast_to(e_last[:, sl], (LANES, MXU_DIM)))
            scores = (_dot_nt(q_b[:, sl], block_diag(k_b[:, sl])) * dmask_ref[t]).astype(BF16)
            inner_ref[c] = _dot(scores, block_diag(v_all[:, sl]))
            qd_ref[c] = qd_all[:, sl]
            kv_ref[c] = jnp.where(same_head, _dot_tn(kd_all[:, sl], v_all[:, sl]), 0.0)
        return carry

    for b in range(bsz):
        prepare(b, 0)

    inv_hd = 1.0 / HEAD_DIM

    def group_norm(y, eps):
        mu = group_sum(y) * inv_hd
        yc = y - mu
        var = group_sum(yc * yc) * inv_hd
        return yc * lax.rsqrt(var + eps)

    def norm_retention():
        yret_ref[...] = group_norm(yret_ref[...], RET_GN_EPS) * rgn_g_ref[...] + rgn_b_ref[...]

    def norm_rwkv():
        bonus = group_sum(full(rl_ref) * full(kml_ref) * rk_ref[...]) * full(vrl_ref)
        yrw_ref[...] = group_norm(yrw_ref[...], RWKV_GN_EPS) * wgn_g_ref[...] + wgn_b_ref[...] + bonus

    def gate_and_store(b_lo, b_hi):
        for b in range(b_lo, b_hi):
            rows = pl.ds(b * CHUNK, CHUNK)
            g = g_ref[b].astype(F32)
            o_ref[b, :, 0:width] = (g * jax.nn.sigmoid(g) * yret_ref[rows, :]).astype(o_ref.dtype)
            o_ref[b, :, width:2 * width] = (yrw_ref[rows, :] * gate_ref[b].astype(F32)).astype(o_ref.dtype)

    filler = [norm_retention, norm_rwkv, functools.partial(gate_and_store, 0, bsz // 2),
              functools.partial(gate_and_store, bsz // 2, bsz)]

    for c in range(n_chains):
        pwb = pw_ref[c]
        pw_ref[c] = _dot(pwb, block_diag(pwb)).astype(BF16)
    for rnd in range(4):
        for c in range(n_chains):
            pwb, t_acc = pw_ref[c], tinv_ref[c]
            res = _dot(jnp.concatenate([pwb, t_acc.astype(BF16)], axis=0), block_diag(pwb))
            pw_ref[c] = res[0:CHUNK].astype(BF16)
            tinv_ref[c] = t_acc + res[CHUNK:]
        filler[rnd]()
    for c in range(n_chains):
        t_acc = tinv_ref[c]
        tinv_ref[c] = t_acc + _dot(t_acc.astype(BF16), block_diag(pw_ref[c]))

    chains = [(c, c // n_tiles, pl.ds((c // n_tiles) * CHUNK, CHUNK),
               slice((c % n_tiles) * MXU_DIM, (c % n_tiles + 1) * MXU_DIM)) for c in range(n_chains)]
    qrm = [_dot(qr_ref[c], srw_ref[c].astype(BF16)) for c, _, _, _ in chains]
    u = [_dot(tinv_ref[c].astype(BF16), block_diag((qrm[c][0:CHUNK] + av_ref[c, 0:CHUNK, :]).astype(BF16)))
         for c, _, _, _ in chains]
    for c, b, rows, sl in chains:
        ub = u[c].astype(BF16)
        yrw_ref[rows, sl] = qrm[c][CHUNK:] + av_ref[c, CHUNK:, :] + _dot(brp_ref[c], block_diag(ub))
        dec = dec_ref[c]
        upd = _dot(pkt_ref[c], jnp.concatenate([ub, vr_ref[b][:, sl]], axis=0))
        srw_ref[c] = srw_ref[c] * jnp.concatenate([dec, dec], axis=1) + jnp.where(same_head, upd, 0.0)
        s = sret_ref[c]
        yret_ref[rows, sl] = inner_ref[c] + _dot(qd_ref[c], s.astype(BF16))
        sret_ref[c] = s * cdec_ref[c % n_tiles] + kv_ref[c]


def _mix_call(streams, dmask, qdec, kdec, cdec, rgn_g, rgn_b, wgn_g, wgn_b, rk):
    q, k, v, g, r, km, vr, lw, a, kku, gate = streams
    bsz, s, width = q.shape
    n_tiles = width // MXU_DIM
    n_steps = s // CHUNK
    cur = pl.BlockSpec((bsz, CHUNK, width), lambda j: (0, jnp.minimum(j, n_steps - 1), 0))
    prev_idx = lambda j: (0, jnp.maximum(j - 1, 0), 0)
    prev = pl.BlockSpec((bsz, CHUNK, width), prev_idx)
    current_streams = [q, k, v, r, km, vr, lw, a, kku]
    previous_streams = [g, gate, r, km, vr]
    consts = [dmask, qdec, kdec, cdec, rgn_g, rgn_b, wgn_g, wgn_b, rk]
    n_chains = bsz * n_tiles
    tc = bsz * CHUNK
    tile = (n_chains, MXU_DIM, MXU_DIM)
    flat = (n_chains, CHUNK, MXU_DIM)
    pair = (n_chains, 2 * CHUNK, MXU_DIM)
    return pl.pallas_call(
        _mix_kernel,
        grid=(n_steps + 1,),
        in_specs=([cur] * len(current_streams) + [prev] * len(previous_streams)
                  + [_const_spec(c.shape) for c in consts]),
        out_specs=pl.BlockSpec((bsz, CHUNK, 2 * width), prev_idx),
        out_shape=jax.ShapeDtypeStruct((bsz, s, 2 * width), BF16),
        scratch_shapes=[pltpu.VMEM(tile, F32),
                        pltpu.VMEM(tile, F32),
                        pltpu.VMEM((tc, width), F32), pltpu.VMEM((tc, width), F32),
                        pltpu.VMEM((tc, width), F32), pltpu.VMEM((tc, width), F32),
                        pltpu.VMEM(flat, BF16), pltpu.VMEM(flat, F32), pltpu.VMEM(flat, BF16),
                        pltpu.VMEM(pair, F32),
                        pltpu.VMEM((n_chains, MXU_DIM, 2 * CHUNK), BF16),
                        pltpu.VMEM((n_chains, MXU_DIM, LANES), F32),
                        pltpu.VMEM(pair, BF16),
                        pltpu.VMEM(flat, F32), pltpu.VMEM(flat, BF16), pltpu.VMEM(tile, F32)],
        compiler_params=pltpu.CompilerParams(dimension_semantics=("arbitrary",),
                                             vmem_limit_bytes=VMEM_LIMIT_BYTES),
        name="mix_recurrences",
    )(*current_streams, *previous_streams, *consts)


def _retention_tables(n_heads):
    h = jnp.arange(n_heads, dtype=F32)
    log_gamma = jnp.log1p(-jnp.exp2(-5.0 - h))
    lg_lane = jnp.repeat(log_gamma, HEAD_DIM)[None, :]
    idx = jnp.arange(CHUNK, dtype=F32)[:, None]
    qdec = jnp.exp((idx + 1.0) * lg_lane)
    kdec = jnp.exp((CHUNK - 1.0 - idx) * lg_lane)
    n_tiles = n_heads // HEADS_PER_TILE
    cdec = jnp.broadcast_to(jnp.exp(CHUNK * lg_lane).reshape(n_tiles, MXU_DIM, 1),
                            (n_tiles, MXU_DIM, MXU_DIM))
    rel = idx - (jnp.arange(MXU_DIM) % HEAD_DIM).astype(F32)[None, :]
    lg_tiles = lg_lane.reshape(n_tiles, 1, MXU_DIM)
    dmask = jnp.where((rel >= 0)[None], jnp.exp(jnp.where(rel >= 0, rel, 0.0)[None] * lg_tiles), 0.0)
    return dmask, qdec, kdec, cdec


def _rotary_tables(s):
    pos = jnp.arange(s, dtype=F32)
    inv_freq = ROPE_BASE ** (-jnp.arange(0, HEAD_DIM, 2, dtype=F32) / HEAD_DIM)
    ang = pos[:, None] * inv_freq[None, :]
    cos, sin = jnp.cos(ang), jnp.sin(ang)
    reps = LANES // HEAD_DIM
    cos_t = jnp.tile(jnp.concatenate([cos, cos], axis=1), (1, reps))
    sin_t = jnp.tile(jnp.concatenate([-sin, sin], axis=1), (1, reps))
    return cos_t, sin_t


def kernel(x, p, ffn1_w_gu, ffn1_w_down, ln1_g, ln1_b, w_in, ret_gn_g, ret_gn_b, rw_mu, rw_w0, rw_w_up, rw_a0, rw_a_up, rw_g_up, rw_k_k, rw_k_a, rw_r_k, rw_gn_g, rw_gn_b, w_out, ln2_g, ln2_b, ffn2_w_gu, ffn2_w_down, ln3_g, ln3_b, ple_w_proj, ple_w_gate, ple_b_gate):
    bsz, s, d = x.shape
    depth = ffn1_w_gu.shape[0]
    alpha = (2.0 * depth) ** 0.25
    width = rw_w0.shape[1]
    n_heads = width // HEAD_DIM
    ret_cols = 4 * width
    lora = DECAY_LORA + AAA_LORA + GATE_LORA
    lora_pad = -(-lora // LANES) * LANES

    dmask, qdec, kdec, cdec = _retention_tables(n_heads)
    cos_t, sin_t = _rotary_tables(s)
    row = lambda v: v.reshape(1, -1)

    h = x.reshape(bsz * s, d)
    for i in range(depth):
        h = _ffn_call(h, ffn1_w_gu[i].astype(BF16), ffn1_w_down[i].astype(BF16),
                      row(ln1_g[i]), row(ln1_b[i]), alpha)

        wi = w_in[i].astype(BF16)
        wret = wi[:, :ret_cols]
        wrw = jnp.pad(wi[:, ret_cols:], ((0, 0), (0, lora_pad - lora)))
        mu = row(jnp.pad(rw_mu[i], (0, lora_pad - lora)))
        wlora = jnp.zeros((lora_pad, 3 * width), BF16)
        wlora = wlora.at[:DECAY_LORA, :width].set(rw_w_up[i].astype(BF16))
        wlora = wlora.at[DECAY_LORA:DECAY_LORA + AAA_LORA, width:2 * width].set(rw_a_up[i].astype(BF16))
        wlora = wlora.at[DECAY_LORA + AAA_LORA:lora, 2 * width:].set(rw_g_up[i].astype(BF16))
        streams = _inproj_call(h.reshape(bsz, s, d), wret, wrw, wlora, cos_t, sin_t, mu,
                               row(rw_w0[i]), row(rw_a0[i]), row(rw_k_k[i]), row(rw_k_a[i]))
        mixed = _mix_call(streams, dmask, qdec, kdec, cdec, row(ret_gn_g[i]), row(ret_gn_b[i]),
                          row(rw_gn_g[i]), row(rw_gn_b[i]), row(rw_r_k[i]))
        pre = (mixed.reshape(bsz * s, 2 * width), w_out[i].astype(BF16), row(ln2_g[i]), row(ln2_b[i]))
        ple = (p[i].reshape(bsz * s, -1), ple_w_proj[i].astype(BF16), ple_w_gate[i].astype(BF16),
               row(ple_b_gate[i]))
        h = _ffn_call(h, ffn2_w_gu[i].astype(BF16), ffn2_w_down[i].astype(BF16),
                      row(ln3_g[i]), row(ln3_b[i]), alpha, pre=pre, ple=ple)
    return h.reshape(bsz, s, d)
```

```python
import functools
import math

import jax
import jax.numpy as jnp
from jax import lax
from jax.experimental import pallas as pl
from jax.experimental.pallas import tpu as pltpu

D_PLE = 256
HEAD_DIM = 64
ROPE_BASE = 10000.0
DECAY_LORA = 64
AAA_LORA = 64
GATE_LORA = 160
LN_EPS = 1e-5
RET_GN_EPS = 1e-5
RWKV_GN_EPS = 64e-5
DECAY_SCALE = math.exp(-0.5)
HEAD_SHIFT = HEAD_DIM.bit_length() - 1

LANES = 128
MXU_DIM = 256
HEADS_PER_TILE = MXU_DIM // HEAD_DIM
CHUNK = 64
VMEM_LIMIT_BYTES = 56 * 1024 * 1024

F32 = jnp.float32
BF16 = jnp.bfloat16


def _dot(a, b):
    return jnp.dot(a, b, preferred_element_type=F32)


def _dot_nt(a, b):
    return lax.dot_general(a, b, (((1,), (1,)), ((), ())), preferred_element_type=F32)


def _dot_tn(a, b):
    return lax.dot_general(a, b, (((0,), (0,)), ((), ())), preferred_element_type=F32)


def _layer_norm(y, g, b):
    mu = jnp.mean(y, axis=-1, keepdims=True)
    yc = y - mu
    var = jnp.mean(yc * yc, axis=-1, keepdims=True)
    return yc * lax.rsqrt(var + LN_EPS) * g + b


def _const_spec(shape):
    nd = len(shape)
    return pl.BlockSpec(shape, lambda *_: (0,) * nd, pipeline_mode=pl.Buffered(1))


def _ffn_kernel(*refs, alpha, fc, with_pre, with_ple):
    refs = list(refs)
    x_ref = refs.pop(0)
    if with_pre:
        m_ref, wo_ref, gp_ref, bp_ref = refs[:4]
        refs = refs[4:]
    wgu_ref, wd_ref, g_ref, b_ref = refs[:4]
    refs = refs[4:]
    if with_ple:
        p_ref, wp_ref, wpg_ref, bpg_ref = refs[:4]
        refs = refs[4:]
    o_ref, act_ref = refs[:2]
    d_ff = wd_ref.shape[0]
    if with_pre:
        xs_ref = refs[2]
        xs_ref[...] = _layer_norm(alpha * x_ref[...] + _dot(m_ref[...], wo_ref[...]), gp_ref[...], bp_ref[...])
    else:
        xs_ref = x_ref
    xb = xs_ref[...].astype(BF16)
    for c in range(d_ff // fc):
        hg = _dot(xb, wgu_ref[:, c * fc:(c + 1) * fc])
        hu = _dot(xb, wgu_ref[:, d_ff + c * fc:d_ff + (c + 1) * fc])
        act_ref[:, c * fc:(c + 1) * fc] = (hg * jax.nn.sigmoid(hg) * hu).astype(BF16)
    down = _dot(act_ref[...], wd_ref[...])
    y = _layer_norm(alpha * xs_ref[...] + 0.5 * down, g_ref[...], b_ref[...])
    if with_ple:
        gate = jax.nn.sigmoid(_dot(y.astype(BF16), wpg_ref[...]) + bpg_ref[...])
        y = y + gate * _dot(p_ref[...].astype(BF16), wp_ref[...])
    o_ref[...] = y


def _ffn_call(x2d, wgu, wd, ln_g, ln_b, alpha, pre=None, ple=None, tm=512, fc=MXU_DIM):
    n, d = x2d.shape
    d_ff = wd.shape[0]
    tm = min(tm, n)
    row = lambda i: (i, 0)
    in_specs = [pl.BlockSpec((tm, d), row)]
    args = [x2d]
    if pre is not None:
        m2d, wo, gp, bp = pre
        in_specs += [pl.BlockSpec((tm, m2d.shape[1]), row), _const_spec(wo.shape),
                     _const_spec((1, d)), _const_spec((1, d))]
        args += [m2d, wo, gp, bp]
    in_specs += [_const_spec(wgu.shape), _const_spec(wd.shape), _const_spec((1, d)), _const_spec((1, d))]
    args += [wgu, wd, ln_g, ln_b]
    if ple is not None:
        p2d, wp, wpg, bpg = ple
        in_specs += [pl.BlockSpec((tm, p2d.shape[1]), row), _const_spec(wp.shape),
                     _const_spec(wpg.shape), _const_spec((1, d))]
        args += [p2d, wp, wpg, bpg]
    return pl.pallas_call(
        functools.partial(_ffn_kernel, alpha=alpha, fc=fc, with_pre=pre is not None,
                          with_ple=ple is not None),
        grid=(n // tm,),
        in_specs=in_specs,
        out_specs=pl.BlockSpec((tm, d), row),
        out_shape=jax.ShapeDtypeStruct((n, d), F32),
        scratch_shapes=[pltpu.VMEM((tm, d_ff), BF16)] + ([pltpu.VMEM((tm, d), F32)] if pre is not None else []),
        compiler_params=pltpu.CompilerParams(dimension_semantics=("arbitrary",),
                                             vmem_limit_bytes=VMEM_LIMIT_BYTES),
        name="ffn_ln",
    )(*args)


def _group_sum(x, ones_blk):
    outs = [_dot(x[:, t * MXU_DIM:(t + 1) * MXU_DIM].astype(BF16), ones_blk)
            for t in range(x.shape[1] // MXU_DIM)]
    return jnp.concatenate(outs, axis=1)


def _inproj_kernel(x_ref, wret_ref, wrw_ref, wlora_ref, cos_ref, sin_ref, mu_ref, w0_ref, a0_ref,
                   kk_ref, ka_ref,
                   q_ref, k_ref, v_ref, g_ref, r_ref, km_ref, vr_ref, lwh_ref, lwl_ref, a_ref, kku_ref,
                   gate_ref, carry_ref, *, width):
    tm = x_ref.shape[0]

    @pl.when(pl.program_id(1) == 0)
    def _():
        carry_ref[...] = jnp.zeros_like(carry_ref)

    xb = x_ref[...].astype(BF16)

    lane = lax.broadcasted_iota(jnp.int32, (tm, width), 1)
    first_half = (lane & (HEAD_DIM // 2)) == 0
    reps = width // cos_ref.shape[1]
    cos = jnp.concatenate([cos_ref[...]] * reps, axis=1)
    sin = jnp.concatenate([sin_ref[...]] * reps, axis=1)

    def rotary(t):
        swapped = jnp.where(first_half, pltpu.roll(t, width - HEAD_DIM // 2, 1),
                            pltpu.roll(t, HEAD_DIM // 2, 1))
        return t * cos + swapped * sin

    def put(ref, val):
        ref[...] = val.astype(ref.dtype)

    def shifted(lo_col, hi_col):
        z = _dot(xb, wrw_ref[:, lo_col:hi_col])
        first = carry_ref[0:1, lo_col:hi_col]
        carry_ref[0:1, lo_col:hi_col] = z[tm - 1:tm, :]
        rowid = lax.broadcasted_iota(jnp.int32, z.shape, 0)
        prev = jnp.where(rowid == 0, first, pltpu.roll(z, 1, 0))
        return z + (prev - z) * mu_ref[:, lo_col:hi_col]

    lo = shifted(3 * width, wrw_ref.shape[1])
    ll = lax.broadcasted_iota(jnp.int32, lo.shape, 1)
    act = jnp.where(ll < DECAY_LORA, jnp.tanh(lo),
                    jnp.where(ll < DECAY_LORA + AAA_LORA, lo, jax.nn.sigmoid(lo)))
    up = _dot(act.astype(BF16), wlora_ref[...])
    lw = -DECAY_SCALE * jax.nn.sigmoid(w0_ref[...] + up[:, 0 * width:1 * width])
    lw_hi, lw_lo = _split_hi_lo(lw)
    lwh_ref[...] = lw_hi
    lwl_ref[...] = lw_lo
    a = jax.nn.sigmoid(a0_ref[...] + up[:, 1 * width:2 * width])
    put(gate_ref, up[:, 2 * width:3 * width])
    put(a_ref, a)
    kr = shifted(1 * width, 2 * width)
    put(kku_ref, kr * kk_ref[...])
    put(km_ref, kr * (1.0 + (a - 1.0) * ka_ref[...]))
    put(r_ref, shifted(0 * width, 1 * width))
    put(vr_ref, shifted(2 * width, 3 * width))

    put(q_ref, rotary(_dot(xb, wret_ref[:, 0 * width:1 * width])))
    put(k_ref, rotary(_dot(xb, wret_ref[:, 1 * width:2 * width])) * (HEAD_DIM ** -0.5))
    put(v_ref, _dot(xb, wret_ref[:, 2 * width:3 * width]))
    put(g_ref, _dot(xb, wret_ref[:, 3 * width:4 * width]))


def _inproj_call(x3d, wret, wrw, wlora, cos, sin, mu, w0, a0, k_k, k_a, tm=512):
    bsz, s, d = x3d.shape
    width = wret.shape[1] // 4
    tm = min(tm, s)
    blk = lambda b, j: (b, j, 0)
    out_spec = pl.BlockSpec((None, tm, width), blk)
    n_out = 12
    out_sds = [jax.ShapeDtypeStruct((bsz, s, width), BF16)] * n_out
    return pl.pallas_call(
        functools.partial(_inproj_kernel, width=width),
        grid=(bsz, s // tm),
        in_specs=[pl.BlockSpec((None, tm, d), blk), _const_spec(wret.shape), _const_spec(wrw.shape),
                  _const_spec(wlora.shape),
                  pl.BlockSpec((tm, cos.shape[1]), lambda b, j: (j, 0)),
                  pl.BlockSpec((tm, sin.shape[1]), lambda b, j: (j, 0)),
                  _const_spec(mu.shape), _const_spec(w0.shape), _const_spec(a0.shape),
                  _const_spec(k_k.shape), _const_spec(k_a.shape)],
        out_specs=[out_spec] * n_out,
        out_shape=out_sds,
        scratch_shapes=[pltpu.VMEM((8, wrw.shape[1]), F32)],
        compiler_params=pltpu.CompilerParams(dimension_semantics=("arbitrary", "arbitrary"),
                                             vmem_limit_bytes=VMEM_LIMIT_BYTES),
        name="inproj_prep",
    )(x3d, wret, wrw, wlora, cos, sin, mu, w0, a0, k_k, k_a)


def _split_hi_lo(x):
    hi = x.astype(BF16)
    lo = (x - hi.astype(F32)).astype(BF16)
    return hi, lo


def _mix_kernel(q_ref, k_ref, v_ref, r_ref, km_ref, vr_ref, lwh_ref, lwl_ref, a_ref, kku_ref,
                g_ref, gate_ref, rl_ref, kml_ref, vrl_ref,
                dmask_ref, qdec_ref, kdec_ref, cdec_ref, rgn_g_ref, rgn_b_ref, wgn_g_ref, wgn_b_ref,
                rk_ref, o_ref,
                sret_ref, srw_ref, kkn_ref, p_ref, yret_ref, yrw_ref,
                pw_ref, tinv_ref, brp_ref, av_ref, pkt_ref, dec_ref, qr_ref, inner_ref, qd_ref, kv_ref):
    bsz, _, width = q_ref.shape
    n_tiles = width // MXU_DIM
    n_chains = bsz * n_tiles
    hpt = HEADS_PER_TILE
    n_rows = bsz * CHUNK

    def full(ref):
        return ref[...].reshape(n_rows, ref.shape[-1]).astype(F32)

    @pl.when(pl.program_id(0) == 0)
    def _():
        sret_ref[...] = jnp.zeros_like(sret_ref)
        srw_ref[...] = jnp.zeros_like(srw_ref)
        yret_ref[...] = jnp.zeros_like(yret_ref)
        yrw_ref[...] = jnp.zeros_like(yrw_ref)

    ri = lax.broadcasted_iota(jnp.int32, (MXU_DIM, MXU_DIM), 0)
    ci = lax.broadcasted_iota(jnp.int32, (MXU_DIM, MXU_DIM), 1)
    same_head = (ri >> HEAD_SHIFT) == (ci >> HEAD_SHIFT)
    ones_blk = same_head.astype(BF16)
    cr = lax.broadcasted_iota(jnp.int32, (CHUNK, MXU_DIM), 0)
    cj = lax.broadcasted_iota(jnp.int32, (CHUNK, MXU_DIM), 1) & (HEAD_DIM - 1)
    strict_lower, incl_lower = cr > cj, cr >= cj
    eye_c = (cr == cj).astype(F32)
    ti = lax.broadcasted_iota(jnp.int32, (2 * CHUNK, CHUNK), 0)
    tj = lax.broadcasted_iota(jnp.int32, (2 * CHUNK, CHUNK), 1)
    tri = (((ti < CHUNK) & (ti >= tj)) | ((ti >= CHUNK) & (ti - CHUNK > tj))).astype(BF16)

    heads_per_lane_tile = LANES // HEAD_DIM
    diag_tiles = [(slice(h * HEAD_DIM, (h + 1) * HEAD_DIM),
                   slice((h // heads_per_lane_tile) * LANES, (h // heads_per_lane_tile + 1) * LANES))
                  for h in range(hpt)]

    def state_bf16(ref, c):
        zeros = jnp.zeros((HEAD_DIM, LANES), BF16)
        rows = []
        for rs, ls in diag_tiles:
            parts = [zeros] * (MXU_DIM // LANES)
            parts[ls.start // LANES] = ref[c, rs, ls].astype(BF16)
            rows.append(jnp.concatenate(parts, axis=1))
        return jnp.concatenate(rows, axis=0)

    def block_diag(x):
        return jnp.where(same_head, jnp.concatenate([x] * hpt, axis=0), jnp.zeros((), x.dtype))

    def group_sum(x):
        return _group_sum(x, ones_blk)

    kku = full(kku_ref)
    kkn = kku * lax.rsqrt(jnp.maximum(group_sum(kku * kku), 1e-24))
    kkn_ref[...] = kkn
    p_ref[...] = -(kkn * full(a_ref))

    def prepare(b, carry):
        rows = pl.ds(b * CHUNK, CHUNK)
        cums = _dot(tri, lwh_ref[b]) + _dot(tri, lwl_ref[b])
        cum = cums[0:CHUNK]
        e_in = jnp.exp(cum)
        e_ex = jnp.exp(cums[CHUNK:])
        e_neg = jnp.exp(-cum)
        e_last = e_in[CHUNK - 1:CHUNK, :]
        qt_all = kkn_ref[rows, :] * e_ex
        rt_all = r_ref[b].astype(F32) * e_in
        pt_all = p_ref[rows, :] * e_neg
        kt_all = km_ref[b].astype(F32) * e_neg
        ph_all = pt_all * e_last
        kh_all = kt_all * e_last
        vr_all = vr_ref[b].astype(BF16)
        q_all = q_ref[b]
        k_all = k_ref[b]
        v_all = v_ref[b].astype(BF16)
        qd_all = (q_all.astype(F32) * qdec_ref[...]).astype(BF16)
        kd_all = (k_all.astype(F32) * kdec_ref[...]).astype(BF16)
        qt_b, rt_b = qt_all.astype(BF16), rt_all.astype(BF16)
        pt_b, kt_b = pt_all.astype(BF16), kt_all.astype(BF16)
        q_b, k_b = q_all.astype(BF16), k_all.astype(BF16)

        for t in range(n_tiles):
            c = b * n_tiles + t
            sl = slice(t * MXU_DIM, (t + 1) * MXU_DIM)
            qr = jnp.concatenate([qt_b[:, sl], rt_b[:, sl]], axis=0)
            qr_ref[c] = qr
            g_p = _dot_nt(qr, block_diag(pt_b[:, sl]))
            g_k = _dot_nt(qr, block_diag(kt_b[:, sl]))
            a_qp = jnp.where(strict_lower, g_p[0:CHUNK], 0.0)
            pw_ref[c] = a_qp.astype(BF16)
            tinv_ref[c] = a_qp + eye_c
            brp_ref[c] = jnp.where(incl_lower, g_p[CHUNK:], 0.0).astype(BF16)
            ab = jnp.concatenate([jnp.where(strict_lower, g_k[0:CHUNK], 0.0),
                                  jnp.where(incl_lower, g_k[CHUNK:], 0.0)], axis=0).astype(BF16)
            av_ref[c] = _dot(ab, block_diag(vr_all[:, sl]))
            pkt_ref[c] = jnp.transpose(jnp.concatenate([ph_all[:, sl], kh_all[:, sl]], axis=0)).astype(BF16)
            dec_ref[c] = jnp.transpose(jnp.broadcast_to(e_last[:, sl], (LANES, MXU_DIM)))
            scores = (_dot_nt(q_b[:, sl], block_diag(k_b[:, sl])) * dmask_ref[t]).astype(BF16)
            inner_ref[c] = _dot(scores, block_diag(v_all[:, sl]))
            qd_ref[c] = qd_all[:, sl]
            kv = _dot_tn(kd_all[:, sl], v_all[:, sl])
            for rs, ls in diag_tiles:
                kv_ref[c, rs, ls] = jnp.where(same_head[rs, ls], kv[rs, ls], 0.0)
        return carry

    for b in range(bsz):
        prepare(b, 0)

    inv_hd = 1.0 / HEAD_DIM

    def group_norm(y, eps):
        mu = group_sum(y) * inv_hd
        yc = y - mu
        var = group_sum(yc * yc) * inv_hd
        return yc * lax.rsqrt(var + eps)

    def norm_retention():
        yret_ref[...] = group_norm(yret_ref[...], RET_GN_EPS) * rgn_g_ref[...] + rgn_b_ref[...]

    def norm_rwkv():
        bonus = group_sum(full(rl_ref) * full(kml_ref) * rk_ref[...]) * full(vrl_ref)
        yrw_ref[...] = group_norm(yrw_ref[...], RWKV_GN_EPS) * wgn_g_ref[...] + wgn_b_ref[...] + bonus

    def gate_and_store(b_lo, b_hi):
        for b in range(b_lo, b_hi):
            rows = pl.ds(b * CHUNK, CHUNK)
            g = g_ref[b].astype(F32)
            o_ref[b, :, 0:width] = (g * jax.nn.sigmoid(g) * yret_ref[rows, :]).astype(o_ref.dtype)
            o_ref[b, :, width:2 * width] = (yrw_ref[rows, :] * gate_ref[b].astype(F32)).astype(o_ref.dtype)

    filler = [norm_retention, norm_rwkv, functools.partial(gate_and_store, 0, bsz // 2),
              functools.partial(gate_and_store, bsz // 2, bsz)]

    for c in range(n_chains):
        pwb = pw_ref[c]
        pw_ref[c] = _dot(pwb, block_diag(pwb)).astype(BF16)
    for rnd in range(4):
        for c in range(n_chains):
            pwb, t_acc = pw_ref[c], tinv_ref[c]
            res = _dot(jnp.concatenate([pwb, t_acc.astype(BF16)], axis=0), block_diag(pwb))
            pw_ref[c] = res[0:CHUNK].astype(BF16)
            tinv_ref[c] = t_acc + res[CHUNK:]
        filler[rnd]()
    for c in range(n_chains):
        t_acc = tinv_ref[c]
        tinv_ref[c] = t_acc + _dot(t_acc.astype(BF16), block_diag(pw_ref[c]))

    chains = [(c, c // n_tiles, pl.ds((c // n_tiles) * CHUNK, CHUNK),
               slice((c % n_tiles) * MXU_DIM, (c % n_tiles + 1) * MXU_DIM)) for c in range(n_chains)]
    qrm = [_dot(qr_ref[c], state_bf16(srw_ref, c)) for c, _, _, _ in chains]
    u = [_dot(tinv_ref[c].astype(BF16), block_diag((qrm[c][0:CHUNK] + av_ref[c, 0:CHUNK, :]).astype(BF16)))
         for c, _, _, _ in chains]
    for c, b, rows, sl in chains:
        ub = u[c].astype(BF16)
        yrw_ref[rows, sl] = qrm[c][CHUNK:] + av_ref[c, CHUNK:, :] + _dot(brp_ref[c], block_diag(ub))
        dec = dec_ref[c]
        upd = _dot(pkt_ref[c], jnp.concatenate([ub, vr_ref[b][:, sl]], axis=0))
        yret_ref[rows, sl] = inner_ref[c] + _dot(qd_ref[c], state_bf16(sret_ref, c))
        for rs, ls in diag_tiles:
            srw_ref[c, rs, ls] = (srw_ref[c, rs, ls] * dec[rs, :]
                                  + jnp.where(same_head[rs, ls], upd[rs, ls], 0.0))
            sret_ref[c, rs, ls] = sret_ref[c, rs, ls] * cdec_ref[c % n_tiles, rs, ls] + kv_ref[c, rs, ls]


def _mix_call(streams, dmask, qdec, kdec, cdec, rgn_g, rgn_b, wgn_g, wgn_b, rk):
    q, k, v, g, r, km, vr, lwh, lwl, a, kku, gate = streams
    bsz, s, width = q.shape
    n_tiles = width // MXU_DIM
    n_steps = s // CHUNK
    cur = pl.BlockSpec((bsz, CHUNK, width), lambda j: (0, jnp.minimum(j, n_steps - 1), 0))
    prev_idx = lambda j: (0, jnp.maximum(j - 1, 0), 0)
    prev = pl.BlockSpec((bsz, CHUNK, width), prev_idx)
    current_streams = [q, k, v, r, km, vr, lwh, lwl, a, kku]
    previous_streams = [g, gate, r, km, vr]
    consts = [dmask, qdec, kdec, cdec, rgn_g, rgn_b, wgn_g, wgn_b, rk]
    n_chains = bsz * n_tiles
    tc = bsz * CHUNK
    tile = (n_chains, MXU_DIM, MXU_DIM)
    flat = (n_chains, CHUNK, MXU_DIM)
    pair = (n_chains, 2 * CHUNK, MXU_DIM)
    return pl.pallas_call(
        _mix_kernel,
        grid=(n_steps + 1,),
        in_specs=([cur] * len(current_streams) + [prev] * len(previous_streams)
                  + [_const_spec(c.shape) for c in consts]),
        out_specs=pl.BlockSpec((bsz, CHUNK, 2 * width), prev_idx),
        out_shape=jax.ShapeDtypeStruct((bsz, s, 2 * width), BF16),
        scratch_shapes=[pltpu.VMEM(tile, F32),
                        pltpu.VMEM(tile, F32),
                        pltpu.VMEM((tc, width), F32), pltpu.VMEM((tc, width), F32),
                        pltpu.VMEM((tc, width), F32), pltpu.VMEM((tc, width), F32),
                        pltpu.VMEM(flat, BF16), pltpu.VMEM(flat, F32), pltpu.VMEM(flat, BF16),
                        pltpu.VMEM(pair, F32),
                        pltpu.VMEM((n_chains, MXU_DIM, 2 * CHUNK), BF16),
                        pltpu.VMEM((n_chains, MXU_DIM, LANES), F32),
                        pltpu.VMEM(pair, BF16),
                        pltpu.VMEM(flat, F32), pltpu.VMEM(flat, BF16), pltpu.VMEM(tile, F32)],
        compiler_params=pltpu.CompilerParams(dimension_semantics=("arbitrary",),
                                             vmem_limit_bytes=VMEM_LIMIT_BYTES),
        name="mix_recurrences",
    )(*current_streams, *previous_streams, *consts)


def _retention_tables(n_heads):
    h = jnp.arange(n_heads, dtype=F32)
    log_gamma = jnp.log1p(-jnp.exp2(-5.0 - h))
    lg_lane = jnp.repeat(log_gamma, HEAD_DIM)[None, :]
    idx = jnp.arange(CHUNK, dtype=F32)[:, None]
    qdec = jnp.exp((idx + 1.0) * lg_lane)
    kdec = jnp.exp((CHUNK - 1.0 - idx) * lg_lane)
    n_tiles = n_heads // HEADS_PER_TILE
    cdec = jnp.broadcast_to(jnp.exp(CHUNK * lg_lane).reshape(n_tiles, MXU_DIM, 1),
                            (n_tiles, MXU_DIM, MXU_DIM))
    rel = idx - (jnp.arange(MXU_DIM) % HEAD_DIM).astype(F32)[None, :]
    lg_tiles = lg_lane.reshape(n_tiles, 1, MXU_DIM)
    dmask = jnp.where((rel >= 0)[None], jnp.exp(jnp.where(rel >= 0, rel, 0.0)[None] * lg_tiles), 0.0)
    return dmask, qdec, kdec, cdec


def _rotary_tables(s):
    pos = jnp.arange(s, dtype=F32)
    inv_freq = ROPE_BASE ** (-jnp.arange(0, HEAD_DIM, 2, dtype=F32) / HEAD_DIM)
    ang = pos[:, None] * inv_freq[None, :]
    cos, sin = jnp.cos(ang), jnp.sin(ang)
    reps = LANES // HEAD_DIM
    cos_t = jnp.tile(jnp.concatenate([cos, cos], axis=1), (1, reps))
    sin_t = jnp.tile(jnp.concatenate([-sin, sin], axis=1), (1, reps))
    return cos_t, sin_t


def kernel(x, p, ffn1_w_gu, ffn1_w_down, ln1_g, ln1_b, w_in, ret_gn_g, ret_gn_b, rw_mu, rw_w0, rw_w_up, rw_a0, rw_a_up, rw_g_up, rw_k_k, rw_k_a, rw_r_k, rw_gn_g, rw_gn_b, w_out, ln2_g, ln2_b, ffn2_w_gu, ffn2_w_down, ln3_g, ln3_b, ple_w_proj, ple_w_gate, ple_b_gate):
    bsz, s, d = x.shape
    depth = ffn1_w_gu.shape[0]
    alpha = (2.0 * depth) ** 0.25
    width = rw_w0.shape[1]
    n_heads = width // HEAD_DIM
    ret_cols = 4 * width
    lora = DECAY_LORA + AAA_LORA + GATE_LORA
    lora_pad = -(-lora // LANES) * LANES

    dmask, qdec, kdec, cdec = _retention_tables(n_heads)
    cos_t, sin_t = _rotary_tables(s)
    row = lambda v: v.reshape(1, -1)

    h = x.reshape(bsz * s, d)
    for i in range(depth):
        h = _ffn_call(h, ffn1_w_gu[i].astype(BF16), ffn1_w_down[i].astype(BF16),
                      row(ln1_g[i]), row(ln1_b[i]), alpha)

        wi = w_in[i].astype(BF16)
        wret = wi[:, :ret_cols]
        wrw = jnp.pad(wi[:, ret_cols:], ((0, 0), (0, lora_pad - lora)))
        mu = row(jnp.pad(rw_mu[i], (0, lora_pad - lora)))
        wlora = jnp.zeros((lora_pad, 3 * width), BF16)
        wlora = wlora.at[:DECAY_LORA, :width].set(rw_w_up[i].astype(BF16))
        wlora = wlora.at[DECAY_LORA:DECAY_LORA + AAA_LORA, width:2 * width].set(rw_a_up[i].astype(BF16))
        wlora = wlora.at[DECAY_LORA + AAA_LORA:lora, 2 * width:].set(rw_g_up[i].astype(BF16))
        streams = _inproj_call(h.reshape(bsz, s, d), wret, wrw, wlora, cos_t, sin_t, mu,
                               row(rw_w0[i]), row(rw_a0[i]), row(rw_k_k[i]), row(rw_k_a[i]))
        mixed = _mix_call(streams, dmask, qdec, kdec, cdec, row(ret_gn_g[i]), row(ret_gn_b[i]),
                          row(rw_gn_g[i]), row(rw_gn_b[i]), row(rw_r_k[i]))
        pre = (mixed.reshape(bsz * s, 2 * width), w_out[i].astype(BF16), row(ln2_g[i]), row(ln2_b[i]))
        ple = (p[i].reshape(bsz * s, -1), ple_w_proj[i].astype(BF16), ple_w_gate[i].astype(BF16),
               row(ple_b_gate[i]))
        h = _ffn_call(h, ffn2_w_gu[i].astype(BF16), ffn2_w_down[i].astype(BF16),
                      row(ln3_g[i]), row(ln3_b[i]), alpha, pre=pre, ple=ple)
    return h.reshape(bsz, s, d)
```

```python
import functools
import math

import jax
import jax.numpy as jnp
from jax import lax
from jax.experimental import pallas as pl
from jax.experimental.pallas import tpu as pltpu

D_PLE = 256
HEAD_DIM = 64
ROPE_BASE = 10000.0
DECAY_LORA = 64
AAA_LORA = 64
GATE_LORA = 160
LN_EPS = 1e-5
RET_GN_EPS = 1e-5
RWKV_GN_EPS = 64e-5
DECAY_SCALE = math.exp(-0.5)
HEAD_SHIFT = HEAD_DIM.bit_length() - 1

LANES = 128
MXU_DIM = 256
HEADS_PER_TILE = MXU_DIM // HEAD_DIM
CHUNK = 64
VMEM_LIMIT_BYTES = 56 * 1024 * 1024

F32 = jnp.float32
BF16 = jnp.bfloat16


def _dot(a, b):
    return jnp.dot(a, b, preferred_element_type=F32)


def _dot_nt(a, b):
    return lax.dot_general(a, b, (((1,), (1,)), ((), ())), preferred_element_type=F32)


def _dot_tn(a, b):
    return lax.dot_general(a, b, (((0,), (0,)), ((), ())), preferred_element_type=F32)


def _layer_norm(y, g, b):
    mu = jnp.mean(y, axis=-1, keepdims=True)
    yc = y - mu
    var = jnp.mean(yc * yc, axis=-1, keepdims=True)
    return yc * lax.rsqrt(var + LN_EPS) * g + b


def _const_spec(shape):
    nd = len(shape)
    return pl.BlockSpec(shape, lambda *_: (0,) * nd, pipeline_mode=pl.Buffered(1))


def _ffn_kernel(*refs, alpha, fc, with_pre, with_ple):
    refs = list(refs)
    x_ref = refs.pop(0)
    if with_pre:
        m_ref, wo_ref, gp_ref, bp_ref = refs[:4]
        refs = refs[4:]
    wgu_ref, wd_ref, g_ref, b_ref = refs[:4]
    refs = refs[4:]
    if with_ple:
        p_ref, wp_ref, wpg_ref, bpg_ref = refs[:4]
        refs = refs[4:]
    o_ref, act_ref = refs[:2]
    d_ff = wd_ref.shape[0]
    if with_pre:
        xs_ref = refs[2]
        xs_ref[...] = _layer_norm(alpha * x_ref[...] + _dot(m_ref[...], wo_ref[...]), gp_ref[...], bp_ref[...])
    else:
        xs_ref = x_ref
    xb = xs_ref[...].astype(BF16)
    for c in range(d_ff // fc):
        hg = _dot(xb, wgu_ref[:, c * fc:(c + 1) * fc].astype(BF16))
        hu = _dot(xb, wgu_ref[:, d_ff + c * fc:d_ff + (c + 1) * fc].astype(BF16))
        act_ref[:, c * fc:(c + 1) * fc] = (hg * jax.nn.sigmoid(hg) * hu).astype(BF16)
    down = _dot(act_ref[...], wd_ref[...].astype(BF16))
    y = _layer_norm(alpha * xs_ref[...] + 0.5 * down, g_ref[...], b_ref[...])
    if with_ple:
        gate = jax.nn.sigmoid(_dot(y.astype(BF16), wpg_ref[...]) + bpg_ref[...])
        y = y + gate * _dot(p_ref[...].astype(BF16), wp_ref[...])
    o_ref[...] = y


def _ffn_call(x2d, wgu, wd, ln_g, ln_b, alpha, pre=None, ple=None, tm=512, fc=MXU_DIM):
    n, d = x2d.shape
    d_ff = wd.shape[0]
    tm = min(tm, n)
    row = lambda i: (i, 0)
    in_specs = [pl.BlockSpec((tm, d), row)]
    args = [x2d]
    if pre is not None:
        m2d, wo, gp, bp = pre
        in_specs += [pl.BlockSpec((tm, m2d.shape[1]), row), _const_spec(wo.shape),
                     _const_spec((1, d)), _const_spec((1, d))]
        args += [m2d, wo, gp, bp]
    in_specs += [_const_spec(wgu.shape), _const_spec(wd.shape), _const_spec((1, d)), _const_spec((1, d))]
    args += [wgu, wd, ln_g, ln_b]
    if ple is not None:
        p2d, wp, wpg, bpg = ple
        in_specs += [pl.BlockSpec((tm, p2d.shape[1]), row), _const_spec(wp.shape),
                     _const_spec(wpg.shape), _const_spec((1, d))]
        args += [p2d, wp, wpg, bpg]
    return pl.pallas_call(
        functools.partial(_ffn_kernel, alpha=alpha, fc=fc, with_pre=pre is not None,
                          with_ple=ple is not None),
        grid=(n // tm,),
        in_specs=in_specs,
        out_specs=pl.BlockSpec((tm, d), row),
        out_shape=jax.ShapeDtypeStruct((n, d), F32),
        scratch_shapes=[pltpu.VMEM((tm, d_ff), BF16)] + ([pltpu.VMEM((tm, d), F32)] if pre is not None else []),
        compiler_params=pltpu.CompilerParams(dimension_semantics=("arbitrary",),
                                             vmem_limit_bytes=VMEM_LIMIT_BYTES),
        name="ffn_ln",
    )(*args)


def _group_sum(x, ones_blk):
    outs = [_dot(x[:, t * MXU_DIM:(t + 1) * MXU_DIM].astype(BF16), ones_blk)
            for t in range(x.shape[1] // MXU_DIM)]
    return jnp.concatenate(outs, axis=1)


def _inproj_kernel(x_ref, win_ref, wlo_ref, wlora_ref, cos_ref, sin_ref, mu_ref, w0_ref, a0_ref,
                   kk_ref, ka_ref,
                   q_ref, k_ref, v_ref, g_ref, r_ref, km_ref, vr_ref, lwh_ref, lwl_ref, a_ref, kku_ref,
                   gate_ref, carry_ref, *, width):
    tm = x_ref.shape[0]

    @pl.when(pl.program_id(1) == 0)
    def _():
        carry_ref[...] = jnp.zeros_like(carry_ref)

    xb = x_ref[...].astype(BF16)

    lane = lax.broadcasted_iota(jnp.int32, (tm, width), 1)
    first_half = (lane & (HEAD_DIM // 2)) == 0
    reps = width // cos_ref.shape[1]
    cos = jnp.concatenate([cos_ref[...]] * reps, axis=1)
    sin = jnp.concatenate([sin_ref[...]] * reps, axis=1)

    def rotary(t):
        swapped = jnp.where(first_half, pltpu.roll(t, width - HEAD_DIM // 2, 1),
                            pltpu.roll(t, HEAD_DIM // 2, 1))
        return t * cos + swapped * sin

    def put(ref, val):
        ref[...] = val.astype(ref.dtype)

    ret_cols = 4 * width

    def w_cols(lo_col, hi_col):
        return win_ref[:, lo_col:hi_col].astype(BF16)

    def shifted(lo_col, hi_col, w):
        z = _dot(xb, w)
        first = carry_ref[0:1, lo_col:hi_col]
        carry_ref[0:1, lo_col:hi_col] = z[tm - 1:tm, :]
        rowid = lax.broadcasted_iota(jnp.int32, z.shape, 0)
        prev = jnp.where(rowid == 0, first, pltpu.roll(z, 1, 0))
        return z + (prev - z) * mu_ref[:, lo_col:hi_col]

    lo = shifted(3 * width, 3 * width + wlo_ref.shape[1], wlo_ref[...])
    ll = lax.broadcasted_iota(jnp.int32, lo.shape, 1)
    act = jnp.where(ll < DECAY_LORA, jnp.tanh(lo),
                    jnp.where(ll < DECAY_LORA + AAA_LORA, lo, jax.nn.sigmoid(lo)))
    up = _dot(act.astype(BF16), wlora_ref[...])
    lw = -DECAY_SCALE * jax.nn.sigmoid(w0_ref[...] + up[:, 0 * width:1 * width])
    lw_hi, lw_lo = _split_hi_lo(lw)
    lwh_ref[...] = lw_hi
    lwl_ref[...] = lw_lo
    a = jax.nn.sigmoid(a0_ref[...] + up[:, 1 * width:2 * width])
    put(gate_ref, up[:, 2 * width:3 * width])
    put(a_ref, a)
    kr = shifted(1 * width, 2 * width, w_cols(ret_cols + 1 * width, ret_cols + 2 * width))
    put(kku_ref, kr * kk_ref[...])
    put(km_ref, kr * (1.0 + (a - 1.0) * ka_ref[...]))
    put(r_ref, shifted(0 * width, 1 * width, w_cols(ret_cols + 0 * width, ret_cols + 1 * width)))
    put(vr_ref, shifted(2 * width, 3 * width, w_cols(ret_cols + 2 * width, ret_cols + 3 * width)))

    put(q_ref, rotary(_dot(xb, w_cols(0 * width, 1 * width))))
    put(k_ref, rotary(_dot(xb, w_cols(1 * width, 2 * width))) * (HEAD_DIM ** -0.5))
    put(v_ref, _dot(xb, w_cols(2 * width, 3 * width)))
    put(g_ref, _dot(xb, w_cols(3 * width, 4 * width)))


def _inproj_call(x3d, w_in, wlo, wlora, cos, sin, mu, w0, a0, k_k, k_a, tm=512):
    bsz, s, d = x3d.shape
    width = w0.shape[1]
    tm = min(tm, s)
    blk = lambda b, j: (b, j, 0)
    out_spec = pl.BlockSpec((None, tm, width), blk)
    n_out = 12
    out_sds = [jax.ShapeDtypeStruct((bsz, s, width), BF16)] * n_out
    return pl.pallas_call(
        functools.partial(_inproj_kernel, width=width),
        grid=(bsz, s // tm),
        in_specs=[pl.BlockSpec((None, tm, d), blk), _const_spec(w_in.shape), _const_spec(wlo.shape),
                  _const_spec(wlora.shape),
                  pl.BlockSpec((tm, cos.shape[1]), lambda b, j: (j, 0)),
                  pl.BlockSpec((tm, sin.shape[1]), lambda b, j: (j, 0)),
                  _const_spec(mu.shape), _const_spec(w0.shape), _const_spec(a0.shape),
                  _const_spec(k_k.shape), _const_spec(k_a.shape)],
        out_specs=[out_spec] * n_out,
        out_shape=out_sds,
        scratch_shapes=[pltpu.VMEM((8, mu.shape[1]), F32)],
        compiler_params=pltpu.CompilerParams(dimension_semantics=("arbitrary", "arbitrary"),
                                             vmem_limit_bytes=VMEM_LIMIT_BYTES),
        name="inproj_prep",
    )(x3d, w_in, wlo, wlora, cos, sin, mu, w0, a0, k_k, k_a)


def _split_hi_lo(x):
    hi = x.astype(BF16)
    lo = (x - hi.astype(F32)).astype(BF16)
    return hi, lo


def _mix_kernel(q_ref, k_ref, v_ref, r_ref, km_ref, vr_ref, lwh_ref, lwl_ref, a_ref, kku_ref,
                g_ref, gate_ref, rl_ref, kml_ref, vrl_ref,
                dmask_ref, qdec_ref, kdec_ref, cdec_ref, rgn_g_ref, rgn_b_ref, wgn_g_ref, wgn_b_ref,
                rk_ref, o_ref,
                sret_ref, srw_ref, kkn_ref, p_ref, yret_ref, yrw_ref,
                pw_ref, tinv_ref, brp_ref, av_ref, pkt_ref, dec_ref, qr_ref, inner_ref, qd_ref, kv_ref):
    bsz, _, width = q_ref.shape
    n_tiles = width // MXU_DIM
    n_chains = bsz * n_tiles
    hpt = HEADS_PER_TILE
    n_rows = bsz * CHUNK

    def full(ref):
        return ref[...].reshape(n_rows, ref.shape[-1]).astype(F32)

    @pl.when(pl.program_id(0) == 0)
    def _():
        sret_ref[...] = jnp.zeros_like(sret_ref)
        srw_ref[...] = jnp.zeros_like(srw_ref)
        yret_ref[...] = jnp.zeros_like(yret_ref)
        yrw_ref[...] = jnp.zeros_like(yrw_ref)

    ri = lax.broadcasted_iota(jnp.int32, (MXU_DIM, MXU_DIM), 0)
    ci = lax.broadcasted_iota(jnp.int32, (MXU_DIM, MXU_DIM), 1)
    same_head = (ri >> HEAD_SHIFT) == (ci >> HEAD_SHIFT)
    ones_blk = same_head.astype(BF16)
    cr = lax.broadcasted_iota(jnp.int32, (CHUNK, MXU_DIM), 0)
    cj = lax.broadcasted_iota(jnp.int32, (CHUNK, MXU_DIM), 1) & (HEAD_DIM - 1)
    strict_lower, incl_lower = cr > cj, cr >= cj
    eye_c = (cr == cj).astype(F32)
    ti = lax.broadcasted_iota(jnp.int32, (2 * CHUNK, CHUNK), 0)
    tj = lax.broadcasted_iota(jnp.int32, (2 * CHUNK, CHUNK), 1)
    tri = (((ti < CHUNK) & (ti >= tj)) | ((ti >= CHUNK) & (ti - CHUNK > tj))).astype(BF16)

    heads_per_lane_tile = LANES // HEAD_DIM
    diag_tiles = [(slice(h * HEAD_DIM, (h + 1) * HEAD_DIM),
                   slice((h // heads_per_lane_tile) * LANES, (h // heads_per_lane_tile + 1) * LANES))
                  for h in range(hpt)]

    def state_bf16(ref, c):
        zeros = jnp.zeros((HEAD_DIM, LANES), BF16)
        rows = []
        for rs, ls in diag_tiles:
            parts = [zeros] * (MXU_DIM // LANES)
            parts[ls.start // LANES] = ref[c, rs, ls].astype(BF16)
            rows.append(jnp.concatenate(parts, axis=1))
        return jnp.concatenate(rows, axis=0)

    def block_diag(x):
        return jnp.where(same_head, jnp.concatenate([x] * hpt, axis=0), jnp.zeros((), x.dtype))

    def group_sum(x):
        return _group_sum(x, ones_blk)

    kku = full(kku_ref)
    kkn = kku * lax.rsqrt(jnp.maximum(group_sum(kku * kku), 1e-24))
    kkn_ref[...] = kkn
    p_ref[...] = -(kkn * full(a_ref))

    def prepare(b, carry):
        rows = pl.ds(b * CHUNK, CHUNK)
        cums = _dot(tri, lwh_ref[b]) + _dot(tri, lwl_ref[b])
        cum = cums[0:CHUNK]
        e_in = jnp.exp(cum)
        e_ex = jnp.exp(cums[CHUNK:])
        e_neg = jnp.exp(-cum)
        e_last = e_in[CHUNK - 1:CHUNK, :]
        qt_all = kkn_ref[rows, :] * e_ex
        rt_all = r_ref[b].astype(F32) * e_in
        pt_all = p_ref[rows, :] * e_neg
        kt_all = km_ref[b].astype(F32) * e_neg
        ph_all = pt_all * e_last
        kh_all = kt_all * e_last
        vr_all = vr_ref[b].astype(BF16)
        q_all = q_ref[b]
        k_all = k_ref[b]
        v_all = v_ref[b].astype(BF16)
        qd_all = (q_all.astype(F32) * qdec_ref[...]).astype(BF16)
        kd_all = (k_all.astype(F32) * kdec_ref[...]).astype(BF16)
        qt_b, rt_b = qt_all.astype(BF16), rt_all.astype(BF16)
        pt_b, kt_b = pt_all.astype(BF16), kt_all.astype(BF16)
        q_b, k_b = q_all.astype(BF16), k_all.astype(BF16)

        for t in range(n_tiles):
            c = b * n_tiles + t
            sl = slice(t * MXU_DIM, (t + 1) * MXU_DIM)
            qr = jnp.concatenate([qt_b[:, sl], rt_b[:, sl]], axis=0)
            qr_ref[c] = qr
            g_p = _dot_nt(qr, block_diag(pt_b[:, sl]))
            g_k = _dot_nt(qr, block_diag(kt_b[:, sl]))
            a_qp = jnp.where(strict_lower, g_p[0:CHUNK], 0.0)
            pw_ref[c] = a_qp.astype(BF16)
            tinv_ref[c] = a_qp + eye_c
            brp_ref[c] = jnp.where(incl_lower, g_p[CHUNK:], 0.0).astype(BF16)
            ab = jnp.concatenate([jnp.where(strict_lower, g_k[0:CHUNK], 0.0),
                                  jnp.where(incl_lower, g_k[CHUNK:], 0.0)], axis=0).astype(BF16)
            av_ref[c] = _dot(ab, block_diag(vr_all[:, sl]))
            pkt_ref[c] = jnp.transpose(jnp.concatenate([ph_all[:, sl], kh_all[:, sl]], axis=0)).astype(BF16)
            dec_ref[c] = jnp.transpose(jnp.broadcast_to(e_last[:, sl], (LANES, MXU_DIM)))
            scores = (_dot_nt(q_b[:, sl], block_diag(k_b[:, sl])) * dmask_ref[t]).astype(BF16)
            inner_ref[c] = _dot(scores, block_diag(v_all[:, sl]))
            qd_ref[c] = qd_all[:, sl]
            kv = _dot_tn(kd_all[:, sl], v_all[:, sl])
            for rs, ls in diag_tiles:
                kv_ref[c, rs, ls] = jnp.where(same_head[rs, ls], kv[rs, ls], 0.0)
        return carry

    for b in range(bsz):
        prepare(b, 0)

    inv_hd = 1.0 / HEAD_DIM

    def group_norm(y, eps):
        mu = group_sum(y) * inv_hd
        yc = y - mu
        var = group_sum(yc * yc) * inv_hd
        return yc * lax.rsqrt(var + eps)

    def norm_retention():
        yret_ref[...] = group_norm(yret_ref[...], RET_GN_EPS) * rgn_g_ref[...] + rgn_b_ref[...]

    def norm_rwkv():
        bonus = group_sum(full(rl_ref) * full(kml_ref) * rk_ref[...]) * full(vrl_ref)
        yrw_ref[...] = group_norm(yrw_ref[...], RWKV_GN_EPS) * wgn_g_ref[...] + wgn_b_ref[...] + bonus

    def gate_and_store(b_lo, b_hi):
        for b in range(b_lo, b_hi):
            rows = pl.ds(b * CHUNK, CHUNK)
            g = g_ref[b].astype(F32)
            o_ref[b, :, 0:width] = (g * jax.nn.sigmoid(g) * yret_ref[rows, :]).astype(o_ref.dtype)
            o_ref[b, :, width:2 * width] = (yrw_ref[rows, :] * gate_ref[b].astype(F32)).astype(o_ref.dtype)

    filler = [norm_retention, norm_rwkv, functools.partial(gate_and_store, 0, bsz // 2),
              functools.partial(gate_and_store, bsz // 2, bsz)]

    for c in range(n_chains):
        pwb = pw_ref[c]
        pw_ref[c] = _dot(pwb, block_diag(pwb)).astype(BF16)
    for rnd in range(4):
        for c in range(n_chains):
            pwb, t_acc = pw_ref[c], tinv_ref[c]
            res = _dot(jnp.concatenate([pwb, t_acc.astype(BF16)], axis=0), block_diag(pwb))
            pw_ref[c] = res[0:CHUNK].astype(BF16)
            tinv_ref[c] = t_acc + res[CHUNK:]
        filler[rnd]()
    for c in range(n_chains):
        t_acc = tinv_ref[c]
        tinv_ref[c] = t_acc + _dot(t_acc.astype(BF16), block_diag(pw_ref[c]))

    chains = [(c, c // n_tiles, pl.ds((c // n_tiles) * CHUNK, CHUNK),
               slice((c % n_tiles) * MXU_DIM, (c % n_tiles + 1) * MXU_DIM)) for c in range(n_chains)]
    qrm = [_dot(qr_ref[c], state_bf16(srw_ref, c)) for c, _, _, _ in chains]
    u = [_dot(tinv_ref[c].astype(BF16), block_diag((qrm[c][0:CHUNK] + av_ref[c, 0:CHUNK, :]).astype(BF16)))
         for c, _, _, _ in chains]
    for c, b, rows, sl in chains:
        ub = u[c].astype(BF16)
        yrw_ref[rows, sl] = qrm[c][CHUNK:] + av_ref[c, CHUNK:, :] + _dot(brp_ref[c], block_diag(ub))
        dec = dec_ref[c]
        upd = _dot(pkt_ref[c], jnp.concatenate([ub, vr_ref[b][:, sl]], axis=0))
        yret_ref[rows, sl] = inner_ref[c] + _dot(qd_ref[c], state_bf16(sret_ref, c))
        for rs, ls in diag_tiles:
            srw_ref[c, rs, ls] = (srw_ref[c, rs, ls] * dec[rs, :]
                                  + jnp.where(same_head[rs, ls], upd[rs, ls], 0.0))
            sret_ref[c, rs, ls] = sret_ref[c, rs, ls] * cdec_ref[c % n_tiles, rs, ls] + kv_ref[c, rs, ls]


def _mix_call(streams, dmask, qdec, kdec, cdec, rgn_g, rgn_b, wgn_g, wgn_b, rk):
    q, k, v, g, r, km, vr, lwh, lwl, a, kku, gate = streams
    bsz, s, width = q.shape
    n_tiles = width // MXU_DIM
    n_steps = s // CHUNK
    cur = pl.BlockSpec((bsz, CHUNK, width), lambda j: (0, jnp.minimum(j, n_steps - 1), 0))
    prev_idx = lambda j: (0, jnp.maximum(j - 1, 0), 0)
    prev = pl.BlockSpec((bsz, CHUNK, width), prev_idx)
    current_streams = [q, k, v, r, km, vr, lwh, lwl, a, kku]
    previous_streams = [g, gate, r, km, vr]
    consts = [dmask, qdec, kdec, cdec, rgn_g, rgn_b, wgn_g, wgn_b, rk]
    n_chains = bsz * n_tiles
    tc = bsz * CHUNK
    tile = (n_chains, MXU_DIM, MXU_DIM)
    flat = (n_chains, CHUNK, MXU_DIM)
    pair = (n_chains, 2 * CHUNK, MXU_DIM)
    return pl.pallas_call(
        _mix_kernel,
        grid=(n_steps + 1,),
        in_specs=([cur] * len(current_streams) + [prev] * len(previous_streams)
                  + [_const_spec(c.shape) for c in consts]),
        out_specs=pl.BlockSpec((bsz, CHUNK, 2 * width), prev_idx),
        out_shape=jax.ShapeDtypeStruct((bsz, s, 2 * width), BF16),
        scratch_shapes=[pltpu.VMEM(tile, F32),
                        pltpu.VMEM(tile, F32),
                        pltpu.VMEM((tc, width), F32), pltpu.VMEM((tc, width), F32),
                        pltpu.VMEM((tc, width), F32), pltpu.VMEM((tc, width), F32),
                        pltpu.VMEM(flat, BF16), pltpu.VMEM(flat, F32), pltpu.VMEM(flat, BF16),
                        pltpu.VMEM(pair, F32),
                        pltpu.VMEM((n_chains, MXU_DIM, 2 * CHUNK), BF16),
                        pltpu.VMEM((n_chains, MXU_DIM, LANES), F32),
                        pltpu.VMEM(pair, BF16),
                        pltpu.VMEM(flat, F32), pltpu.VMEM(flat, BF16), pltpu.VMEM(tile, F32)],
        compiler_params=pltpu.CompilerParams(dimension_semantics=("arbitrary",),
                                             vmem_limit_bytes=VMEM_LIMIT_BYTES),
        name="mix_recurrences",
    )(*current_streams, *previous_streams, *consts)


def _retention_tables(n_heads):
    h = jnp.arange(n_heads, dtype=F32)
    log_gamma = jnp.log1p(-jnp.exp2(-5.0 - h))
    lg_lane = jnp.repeat(log_gamma, HEAD_DIM)[None, :]
    idx = jnp.arange(CHUNK, dtype=F32)[:, None]
    qdec = jnp.exp((idx + 1.0) * lg_lane)
    kdec = jnp.exp((CHUNK - 1.0 - idx) * lg_lane)
    n_tiles = n_heads // HEADS_PER_TILE
    cdec = jnp.broadcast_to(jnp.exp(CHUNK * lg_lane).reshape(n_tiles, MXU_DIM, 1),
                            (n_tiles, MXU_DIM, MXU_DIM))
    rel = idx - (jnp.arange(MXU_DIM) % HEAD_DIM).astype(F32)[None, :]
    lg_tiles = lg_lane.reshape(n_tiles, 1, MXU_DIM)
    dmask = jnp.where((rel >= 0)[None], jnp.exp(jnp.where(rel >= 0, rel, 0.0)[None] * lg_tiles), 0.0)
    return dmask, qdec, kdec, cdec


def _rotary_tables(s):
    pos = jnp.arange(s, dtype=F32)
    inv_freq = ROPE_BASE ** (-jnp.arange(0, HEAD_DIM, 2, dtype=F32) / HEAD_DIM)
    ang = pos[:, None] * inv_freq[None, :]
    cos, sin = jnp.cos(ang), jnp.sin(ang)
    reps = LANES // HEAD_DIM
    cos_t = jnp.tile(jnp.concatenate([cos, cos], axis=1), (1, reps))
    sin_t = jnp.tile(jnp.concatenate([-sin, sin], axis=1), (1, reps))
    return cos_t, sin_t


def kernel(x, p, ffn1_w_gu, ffn1_w_down, ln1_g, ln1_b, w_in, ret_gn_g, ret_gn_b, rw_mu, rw_w0, rw_w_up, rw_a0, rw_a_up, rw_g_up, rw_k_k, rw_k_a, rw_r_k, rw_gn_g, rw_gn_b, w_out, ln2_g, ln2_b, ffn2_w_gu, ffn2_w_down, ln3_g, ln3_b, ple_w_proj, ple_w_gate, ple_b_gate):
    bsz, s, d = x.shape
    depth = ffn1_w_gu.shape[0]
    alpha = (2.0 * depth) ** 0.25
    width = rw_w0.shape[1]
    n_heads = width // HEAD_DIM
    ret_cols = 4 * width
    lora = DECAY_LORA + AAA_LORA + GATE_LORA
    lora_pad = -(-lora // LANES) * LANES

    dmask, qdec, kdec, cdec = _retention_tables(n_heads)
    cos_t, sin_t = _rotary_tables(s)
    row = lambda v: v.reshape(1, -1)

    h = x.reshape(bsz * s, d)
    for i in range(depth):
        h = _ffn_call(h, ffn1_w_gu[i], ffn1_w_down[i],
                      row(ln1_g[i]), row(ln1_b[i]), alpha)

        wlo = jnp.pad(w_in[i][:, ret_cols + 3 * width:].astype(BF16), ((0, 0), (0, lora_pad - lora)))
        mu = row(jnp.pad(rw_mu[i], (0, lora_pad - lora)))
        wlora = jnp.zeros((lora_pad, 3 * width), BF16)
        wlora = wlora.at[:DECAY_LORA, :width].set(rw_w_up[i].astype(BF16))
        wlora = wlora.at[DECAY_LORA:DECAY_LORA + AAA_LORA, width:2 * width].set(rw_a_up[i].astype(BF16))
        wlora = wlora.at[DECAY_LORA + AAA_LORA:lora, 2 * width:].set(rw_g_up[i].astype(BF16))
        streams = _inproj_call(h.reshape(bsz, s, d), w_in[i], wlo, wlora, cos_t, sin_t, mu,
                               row(rw_w0[i]), row(rw_a0[i]), row(rw_k_k[i]), row(rw_k_a[i]))
        mixed = _mix_call(streams, dmask, qdec, kdec, cdec, row(ret_gn_g[i]), row(ret_gn_b[i]),
                          row(rw_gn_g[i]), row(rw_gn_b[i]), row(rw_r_k[i]))
        pre = (mixed.reshape(bsz * s, 2 * width), w_out[i].astype(BF16), row(ln2_g[i]), row(ln2_b[i]))
        ple = (p[i].reshape(bsz * s, -1), ple_w_proj[i].astype(BF16), ple_w_gate[i].astype(BF16),
               row(ple_b_gate[i]))
        h = _ffn_call(h, ffn2_w_gu[i], ffn2_w_down[i].astype(BF16),
                      row(ln3_g[i]), row(ln3_b[i]), alpha, pre=pre, ple=ple)
    return h.reshape(bsz, s, d)
```

```python
import functools
import math

import jax
import jax.numpy as jnp
from jax import lax
from jax.experimental import pallas as pl
from jax.experimental.pallas import tpu as pltpu

D_PLE = 256
HEAD_DIM = 64
ROPE_BASE = 10000.0
DECAY_LORA = 64
AAA_LORA = 64
GATE_LORA = 160
LN_EPS = 1e-5
RET_GN_EPS = 1e-5
RWKV_GN_EPS = 64e-5
DECAY_SCALE = math.exp(-0.5)
HEAD_SHIFT = HEAD_DIM.bit_length() - 1

LANES = 128
MXU_DIM = 256
HEADS_PER_TILE = MXU_DIM // HEAD_DIM
CHUNK = 64
VMEM_LIMIT_BYTES = 56 * 1024 * 1024

F32 = jnp.float32
BF16 = jnp.bfloat16


def _dot(a, b):
    return jnp.dot(a, b, preferred_element_type=F32)


def _dot_nt(a, b):
    return lax.dot_general(a, b, (((1,), (1,)), ((), ())), preferred_element_type=F32)


def _dot_tn(a, b):
    return lax.dot_general(a, b, (((0,), (0,)), ((), ())), preferred_element_type=F32)


def _layer_norm(y, g, b):
    mu = jnp.mean(y, axis=-1, keepdims=True)
    yc = y - mu
    var = jnp.mean(yc * yc, axis=-1, keepdims=True)
    return yc * lax.rsqrt(var + LN_EPS) * g + b


def _const_spec(shape):
    nd = len(shape)
    return pl.BlockSpec(shape, lambda *_: (0,) * nd, pipeline_mode=pl.Buffered(1))


def _ffn_kernel(*refs, alpha, fc, with_pre, with_ple):
    refs = list(refs)
    x_ref = refs.pop(0)
    if with_pre:
        m_ref, wo_ref, gp_ref, bp_ref = refs[:4]
        refs = refs[4:]
    wgu_ref, wd_ref, g_ref, b_ref = refs[:4]
    refs = refs[4:]
    if with_ple:
        p_ref, wp_ref, wpg_ref, bpg_ref = refs[:4]
        refs = refs[4:]
    o_ref, act_ref = refs[:2]
    d_ff = wd_ref.shape[0]
    if with_pre:
        xs_ref = refs[2]
        xs_ref[...] = _layer_norm(alpha * x_ref[...] + _dot(m_ref[...], wo_ref[...]), gp_ref[...], bp_ref[...])
    else:
        xs_ref = x_ref
    xb = xs_ref[...].astype(BF16)
    for c in range(d_ff // fc):
        hg = _dot(xb, wgu_ref[:, c * fc:(c + 1) * fc].astype(BF16))
        hu = _dot(xb, wgu_ref[:, d_ff + c * fc:d_ff + (c + 1) * fc].astype(BF16))
        act_ref[:, c * fc:(c + 1) * fc] = (hg * jax.nn.sigmoid(hg) * hu).astype(BF16)
    down = _dot(act_ref[...], wd_ref[...].astype(BF16))
    y = _layer_norm(alpha * xs_ref[...] + 0.5 * down, g_ref[...], b_ref[...])
    if with_ple:
        gate = jax.nn.sigmoid(_dot(y.astype(BF16), wpg_ref[...]) + bpg_ref[...])
        y = y + gate * _dot(p_ref[...].astype(BF16), wp_ref[...])
    o_ref[...] = y


def _ffn_call(x2d, wgu, wd, ln_g, ln_b, alpha, pre=None, ple=None, tm=512, fc=MXU_DIM):
    n, d = x2d.shape
    d_ff = wd.shape[0]
    tm = min(tm, n)
    row = lambda i: (i, 0)
    in_specs = [pl.BlockSpec((tm, d), row)]
    args = [x2d]
    if pre is not None:
        m2d, wo, gp, bp = pre
        in_specs += [pl.BlockSpec((tm, m2d.shape[1]), row), _const_spec(wo.shape),
                     _const_spec((1, d)), _const_spec((1, d))]
        args += [m2d, wo, gp, bp]
    in_specs += [_const_spec(wgu.shape), _const_spec(wd.shape), _const_spec((1, d)), _const_spec((1, d))]
    args += [wgu, wd, ln_g, ln_b]
    if ple is not None:
        p2d, wp, wpg, bpg = ple
        in_specs += [pl.BlockSpec((tm, p2d.shape[1]), row), _const_spec(wp.shape),
                     _const_spec(wpg.shape), _const_spec((1, d))]
        args += [p2d, wp, wpg, bpg]
    return pl.pallas_call(
        functools.partial(_ffn_kernel, alpha=alpha, fc=fc, with_pre=pre is not None,
                          with_ple=ple is not None),
        grid=(n // tm,),
        in_specs=in_specs,
        out_specs=pl.BlockSpec((tm, d), row),
        out_shape=jax.ShapeDtypeStruct((n, d), F32),
        scratch_shapes=[pltpu.VMEM((tm, d_ff), BF16)] + ([pltpu.VMEM((tm, d), F32)] if pre is not None else []),
        compiler_params=pltpu.CompilerParams(dimension_semantics=("arbitrary",),
                                             vmem_limit_bytes=VMEM_LIMIT_BYTES),
        name="ffn_ln",
    )(*args)


def _group_sum(x, ones_blk):
    outs = [_dot(x[:, t * MXU_DIM:(t + 1) * MXU_DIM].astype(BF16), ones_blk)
            for t in range(x.shape[1] // MXU_DIM)]
    return jnp.concatenate(outs, axis=1)


def _inproj_kernel(x_ref, win_ref, wlo_ref, wlora_ref, cos_ref, sin_ref, mu_ref, w0_ref, a0_ref,
                   kk_ref, ka_ref,
                   q_ref, k_ref, v_ref, g_ref, r_ref, km_ref, vr_ref, lwh_ref, lwl_ref, a_ref, kku_ref,
                   gate_ref, carry_ref, *, width):
    tm = x_ref.shape[0]

    @pl.when(pl.program_id(1) == 0)
    def _():
        carry_ref[...] = jnp.zeros_like(carry_ref)

    xb = x_ref[...].astype(BF16)

    lane = lax.broadcasted_iota(jnp.int32, (tm, width), 1)
    first_half = (lane & (HEAD_DIM // 2)) == 0
    reps = width // cos_ref.shape[1]
    cos = jnp.concatenate([cos_ref[...]] * reps, axis=1)
    sin = jnp.concatenate([sin_ref[...]] * reps, axis=1)

    def rotary(t):
        swapped = jnp.where(first_half, pltpu.roll(t, width - HEAD_DIM // 2, 1),
                            pltpu.roll(t, HEAD_DIM // 2, 1))
        return t * cos + swapped * sin

    def put(ref, val):
        ref[...] = val.astype(ref.dtype)

    ret_cols = 4 * width

    def project(lo_col, hi_col):
        return _dot_nt(xb, win_ref[lo_col:hi_col, :].astype(BF16))

    def shifted(lo_col, hi_col, z):
        first = carry_ref[0:1, lo_col:hi_col]
        carry_ref[0:1, lo_col:hi_col] = z[tm - 1:tm, :]
        rowid = lax.broadcasted_iota(jnp.int32, z.shape, 0)
        prev = jnp.where(rowid == 0, first, pltpu.roll(z, 1, 0))
        return z + (prev - z) * mu_ref[:, lo_col:hi_col]

    lo = shifted(3 * width, 3 * width + wlo_ref.shape[1], _dot(xb, wlo_ref[...]))
    ll = lax.broadcasted_iota(jnp.int32, lo.shape, 1)
    act = jnp.where(ll < DECAY_LORA, jnp.tanh(lo),
                    jnp.where(ll < DECAY_LORA + AAA_LORA, lo, jax.nn.sigmoid(lo)))
    up = _dot(act.astype(BF16), wlora_ref[...])
    lw = -DECAY_SCALE * jax.nn.sigmoid(w0_ref[...] + up[:, 0 * width:1 * width])
    lw_hi, lw_lo = _split_hi_lo(lw)
    lwh_ref[...] = lw_hi
    lwl_ref[...] = lw_lo
    a = jax.nn.sigmoid(a0_ref[...] + up[:, 1 * width:2 * width])
    put(gate_ref, up[:, 2 * width:3 * width])
    put(a_ref, a)
    kr = shifted(1 * width, 2 * width, project(ret_cols + 1 * width, ret_cols + 2 * width))
    put(kku_ref, kr * kk_ref[...])
    put(km_ref, kr * (1.0 + (a - 1.0) * ka_ref[...]))
    put(r_ref, shifted(0 * width, 1 * width, project(ret_cols + 0 * width, ret_cols + 1 * width)))
    put(vr_ref, shifted(2 * width, 3 * width, project(ret_cols + 2 * width, ret_cols + 3 * width)))

    put(q_ref, rotary(project(0 * width, 1 * width)))
    put(k_ref, rotary(project(1 * width, 2 * width)) * (HEAD_DIM ** -0.5))
    put(v_ref, project(2 * width, 3 * width))
    put(g_ref, project(3 * width, 4 * width))


def _inproj_call(x3d, w_in, wlo, wlora, cos, sin, mu, w0, a0, k_k, k_a, tm=512):
    bsz, s, d = x3d.shape
    width = w0.shape[1]
    tm = min(tm, s)
    blk = lambda b, j: (b, j, 0)
    out_spec = pl.BlockSpec((None, tm, width), blk)
    n_out = 12
    out_sds = [jax.ShapeDtypeStruct((bsz, s, width), BF16)] * n_out
    return pl.pallas_call(
        functools.partial(_inproj_kernel, width=width),
        grid=(bsz, s // tm),
        in_specs=[pl.BlockSpec((None, tm, d), blk), _const_spec(w_in.shape), _const_spec(wlo.shape),
                  _const_spec(wlora.shape),
                  pl.BlockSpec((tm, cos.shape[1]), lambda b, j: (j, 0)),
                  pl.BlockSpec((tm, sin.shape[1]), lambda b, j: (j, 0)),
                  _const_spec(mu.shape), _const_spec(w0.shape), _const_spec(a0.shape),
                  _const_spec(k_k.shape), _const_spec(k_a.shape)],
        out_specs=[out_spec] * n_out,
        out_shape=out_sds,
        scratch_shapes=[pltpu.VMEM((8, mu.shape[1]), F32)],
        compiler_params=pltpu.CompilerParams(dimension_semantics=("arbitrary", "arbitrary"),
                                             vmem_limit_bytes=VMEM_LIMIT_BYTES),
        name="inproj_prep",
    )(x3d, w_in, wlo, wlora, cos, sin, mu, w0, a0, k_k, k_a)


def _split_hi_lo(x):
    hi = x.astype(BF16)
    lo = (x - hi.astype(F32)).astype(BF16)
    return hi, lo


def _mix_kernel(q_ref, k_ref, v_ref, r_ref, km_ref, vr_ref, lwh_ref, lwl_ref, a_ref, kku_ref,
                g_ref, gate_ref, rl_ref, kml_ref, vrl_ref,
                dmask_ref, qdec_ref, kdec_ref, cdec_ref, rgn_g_ref, rgn_b_ref, wgn_g_ref, wgn_b_ref,
                rk_ref, o_ref,
                sret_ref, srw_ref, kkn_ref, p_ref, yret_ref, yrw_ref,
                pw_ref, tinv_ref, brp_ref, av_ref, pkt_ref, dec_ref, qr_ref, inner_ref, qd_ref, kv_ref):
    bsz, _, width = q_ref.shape
    n_tiles = width // MXU_DIM
    n_chains = bsz * n_tiles
    hpt = HEADS_PER_TILE
    n_rows = bsz * CHUNK

    def full(ref):
        return ref[...].reshape(n_rows, ref.shape[-1]).astype(F32)

    @pl.when(pl.program_id(0) == 0)
    def _():
        sret_ref[...] = jnp.zeros_like(sret_ref)
        srw_ref[...] = jnp.zeros_like(srw_ref)
        yret_ref[...] = jnp.zeros_like(yret_ref)
        yrw_ref[...] = jnp.zeros_like(yrw_ref)

    ri = lax.broadcasted_iota(jnp.int32, (MXU_DIM, MXU_DIM), 0)
    ci = lax.broadcasted_iota(jnp.int32, (MXU_DIM, MXU_DIM), 1)
    same_head = (ri >> HEAD_SHIFT) == (ci >> HEAD_SHIFT)
    ones_blk = same_head.astype(BF16)
    cr = lax.broadcasted_iota(jnp.int32, (CHUNK, MXU_DIM), 0)
    cj = lax.broadcasted_iota(jnp.int32, (CHUNK, MXU_DIM), 1) & (HEAD_DIM - 1)
    strict_lower, incl_lower = cr > cj, cr >= cj
    eye_c = (cr == cj).astype(F32)
    ti = lax.broadcasted_iota(jnp.int32, (2 * CHUNK, CHUNK), 0)
    tj = lax.broadcasted_iota(jnp.int32, (2 * CHUNK, CHUNK), 1)
    tri = (((ti < CHUNK) & (ti >= tj)) | ((ti >= CHUNK) & (ti - CHUNK > tj))).astype(BF16)

    heads_per_lane_tile = LANES // HEAD_DIM
    diag_tiles = [(slice(h * HEAD_DIM, (h + 1) * HEAD_DIM),
                   slice((h // heads_per_lane_tile) * LANES, (h // heads_per_lane_tile + 1) * LANES))
                  for h in range(hpt)]

    def state_bf16(ref, c):
        zeros = jnp.zeros((HEAD_DIM, LANES), BF16)
        rows = []
        for rs, ls in diag_tiles:
            parts = [zeros] * (MXU_DIM // LANES)
            parts[ls.start // LANES] = ref[c, rs, ls].astype(BF16)
            rows.append(jnp.concatenate(parts, axis=1))
        return jnp.concatenate(rows, axis=0)

    def block_diag(x):
        return jnp.where(same_head, jnp.concatenate([x] * hpt, axis=0), jnp.zeros((), x.dtype))

    def group_sum(x):
        return _group_sum(x, ones_blk)

    kku = full(kku_ref)
    kkn = kku * lax.rsqrt(jnp.maximum(group_sum(kku * kku), 1e-24))
    kkn_ref[...] = kkn
    p_ref[...] = -(kkn * full(a_ref))

    def prepare(b, carry):
        rows = pl.ds(b * CHUNK, CHUNK)
        cums = _dot(tri, lwh_ref[b]) + _dot(tri, lwl_ref[b])
        cum = cums[0:CHUNK]
        e_in = jnp.exp(cum)
        e_ex = jnp.exp(cums[CHUNK:])
        e_neg = jnp.exp(-cum)
        e_last = e_in[CHUNK - 1:CHUNK, :]
        qt_all = kkn_ref[rows, :] * e_ex
        rt_all = r_ref[b].astype(F32) * e_in
        pt_all = p_ref[rows, :] * e_neg
        kt_all = km_ref[b].astype(F32) * e_neg
        ph_all = pt_all * e_last
        kh_all = kt_all * e_last
        vr_all = vr_ref[b].astype(BF16)
        q_all = q_ref[b]
        k_all = k_ref[b]
        v_all = v_ref[b].astype(BF16)
        qd_all = (q_all.astype(F32) * qdec_ref[...]).astype(BF16)
        kd_all = (k_all.astype(F32) * kdec_ref[...]).astype(BF16)
        qt_b, rt_b = qt_all.astype(BF16), rt_all.astype(BF16)
        pt_b, kt_b = pt_all.astype(BF16), kt_all.astype(BF16)
        q_b, k_b = q_all.astype(BF16), k_all.astype(BF16)

        for t in range(n_tiles):
            c = b * n_tiles + t
            sl = slice(t * MXU_DIM, (t + 1) * MXU_DIM)
            qr = jnp.concatenate([qt_b[:, sl], rt_b[:, sl]], axis=0)
            qr_ref[c] = qr
            g_p = _dot_nt(qr, block_diag(pt_b[:, sl]))
            g_k = _dot_nt(qr, block_diag(kt_b[:, sl]))
            a_qp = jnp.where(strict_lower, g_p[0:CHUNK], 0.0)
            pw_ref[c] = a_qp.astype(BF16)
            tinv_ref[c] = a_qp + eye_c
            brp_ref[c] = jnp.where(incl_lower, g_p[CHUNK:], 0.0).astype(BF16)
            ab = jnp.concatenate([jnp.where(strict_lower, g_k[0:CHUNK], 0.0),
                                  jnp.where(incl_lower, g_k[CHUNK:], 0.0)], axis=0).astype(BF16)
            av_ref[c] = _dot(ab, block_diag(vr_all[:, sl]))
            pkt_ref[c] = jnp.transpose(jnp.concatenate([ph_all[:, sl], kh_all[:, sl]], axis=0)).astype(BF16)
            dec_ref[c] = jnp.transpose(jnp.broadcast_to(e_last[:, sl], (LANES, MXU_DIM)))
            scores = (_dot_nt(q_b[:, sl], block_diag(k_b[:, sl])) * dmask_ref[t]).astype(BF16)
            inner_ref[c] = _dot(scores, block_diag(v_all[:, sl]))
            qd_ref[c] = qd_all[:, sl]
            kv = _dot_tn(kd_all[:, sl], v_all[:, sl])
            for rs, ls in diag_tiles:
                kv_ref[c, rs, ls] = jnp.where(same_head[rs, ls], kv[rs, ls], 0.0)
        return carry

    for b in range(bsz):
        prepare(b, 0)

    inv_hd = 1.0 / HEAD_DIM

    def group_norm(y, eps):
        mu = group_sum(y) * inv_hd
        yc = y - mu
        var = group_sum(yc * yc) * inv_hd
        return yc * lax.rsqrt(var + eps)

    def norm_retention():
        yret_ref[...] = group_norm(yret_ref[...], RET_GN_EPS) * rgn_g_ref[...] + rgn_b_ref[...]

    def norm_rwkv():
        bonus = group_sum(full(rl_ref) * full(kml_ref) * rk_ref[...]) * full(vrl_ref)
        yrw_ref[...] = group_norm(yrw_ref[...], RWKV_GN_EPS) * wgn_g_ref[...] + wgn_b_ref[...] + bonus

    def gate_and_store(b_lo, b_hi):
        for b in range(b_lo, b_hi):
            rows = pl.ds(b * CHUNK, CHUNK)
            g = g_ref[b].astype(F32)
            o_ref[b, :, 0:width] = (g * jax.nn.sigmoid(g) * yret_ref[rows, :]).astype(o_ref.dtype)
            o_ref[b, :, width:2 * width] = (yrw_ref[rows, :] * gate_ref[b].astype(F32)).astype(o_ref.dtype)

    filler = [norm_retention, norm_rwkv, functools.partial(gate_and_store, 0, bsz // 2),
              functools.partial(gate_and_store, bsz // 2, bsz)]

    for c in range(n_chains):
        pwb = pw_ref[c]
        pw_ref[c] = _dot(pwb, block_diag(pwb)).astype(BF16)
    for rnd in range(4):
        for c in range(n_chains):
            pwb, t_acc = pw_ref[c], tinv_ref[c]
            res = _dot(jnp.concatenate([pwb, t_acc.astype(BF16)], axis=0), block_diag(pwb))
            pw_ref[c] = res[0:CHUNK].astype(BF16)
            tinv_ref[c] = t_acc + res[CHUNK:]
        filler[rnd]()
    for c in range(n_chains):
        t_acc = tinv_ref[c]
        tinv_ref[c] = t_acc + _dot(t_acc.astype(BF16), block_diag(pw_ref[c]))

    chains = [(c, c // n_tiles, pl.ds((c // n_tiles) * CHUNK, CHUNK),
               slice((c % n_tiles) * MXU_DIM, (c % n_tiles + 1) * MXU_DIM)) for c in range(n_chains)]
    qrm = [_dot(qr_ref[c], state_bf16(srw_ref, c)) for c, _, _, _ in chains]
    u = [_dot(tinv_ref[c].astype(BF16), block_diag((qrm[c][0:CHUNK] + av_ref[c, 0:CHUNK, :]).astype(BF16)))
         for c, _, _, _ in chains]
    for c, b, rows, sl in chains:
        ub = u[c].astype(BF16)
        yrw_ref[rows, sl] = qrm[c][CHUNK:] + av_ref[c, CHUNK:, :] + _dot(brp_ref[c], block_diag(ub))
        dec = dec_ref[c]
        upd = _dot(pkt_ref[c], jnp.concatenate([ub, vr_ref[b][:, sl]], axis=0))
        yret_ref[rows, sl] = inner_ref[c] + _dot(qd_ref[c], state_bf16(sret_ref, c))
        for rs, ls in diag_tiles:
            srw_ref[c, rs, ls] = (srw_ref[c, rs, ls] * dec[rs, :]
                                  + jnp.where(same_head[rs, ls], upd[rs, ls], 0.0))
            sret_ref[c, rs, ls] = sret_ref[c, rs, ls] * cdec_ref[c % n_tiles, rs, ls] + kv_ref[c, rs, ls]


def _mix_call(streams, dmask, qdec, kdec, cdec, rgn_g, rgn_b, wgn_g, wgn_b, rk):
    q, k, v, g, r, km, vr, lwh, lwl, a, kku, gate = streams
    bsz, s, width = q.shape
    n_tiles = width // MXU_DIM
    n_steps = s // CHUNK
    cur = pl.BlockSpec((bsz, CHUNK, width), lambda j: (0, jnp.minimum(j, n_steps - 1), 0))
    prev_idx = lambda j: (0, jnp.maximum(j - 1, 0), 0)
    prev = pl.BlockSpec((bsz, CHUNK, width), prev_idx)
    current_streams = [q, k, v, r, km, vr, lwh, lwl, a, kku]
    previous_streams = [g, gate, r, km, vr]
    consts = [dmask, qdec, kdec, cdec, rgn_g, rgn_b, wgn_g, wgn_b, rk]
    n_chains = bsz * n_tiles
    tc = bsz * CHUNK
    tile = (n_chains, MXU_DIM, MXU_DIM)
    flat = (n_chains, CHUNK, MXU_DIM)
    pair = (n_chains, 2 * CHUNK, MXU_DIM)
    return pl.pallas_call(
        _mix_kernel,
        grid=(n_steps + 1,),
        in_specs=([cur] * len(current_streams) + [prev] * len(previous_streams)
                  + [_const_spec(c.shape) for c in consts]),
        out_specs=pl.BlockSpec((bsz, CHUNK, 2 * width), prev_idx),
        out_shape=jax.ShapeDtypeStruct((bsz, s, 2 * width), BF16),
        scratch_shapes=[pltpu.VMEM(tile, F32),
                        pltpu.VMEM(tile, F32),
                        pltpu.VMEM((tc, width), F32), pltpu.VMEM((tc, width), F32),
                        pltpu.VMEM((tc, width), F32), pltpu.VMEM((tc, width), F32),
                        pltpu.VMEM(flat, BF16), pltpu.VMEM(flat, F32), pltpu.VMEM(flat, BF16),
                        pltpu.VMEM(pair, F32),
                        pltpu.VMEM((n_chains, MXU_DIM, 2 * CHUNK), BF16),
                        pltpu.VMEM((n_chains, MXU_DIM, LANES), F32),
                        pltpu.VMEM(pair, BF16),
                        pltpu.VMEM(flat, F32), pltpu.VMEM(flat, BF16), pltpu.VMEM(tile, F32)],
        compiler_params=pltpu.CompilerParams(dimension_semantics=("arbitrary",),
                                             vmem_limit_bytes=VMEM_LIMIT_BYTES),
        name="mix_recurrences",
    )(*current_streams, *previous_streams, *consts)


def _retention_tables(n_heads):
    h = jnp.arange(n_heads, dtype=F32)
    log_gamma = jnp.log1p(-jnp.exp2(-5.0 - h))
    lg_lane = jnp.repeat(log_gamma, HEAD_DIM)[None, :]
    idx = jnp.arange(CHUNK, dtype=F32)[:, None]
    qdec = jnp.exp((idx + 1.0) * lg_lane)
    kdec = jnp.exp((CHUNK - 1.0 - idx) * lg_lane)
    n_tiles = n_heads // HEADS_PER_TILE
    cdec = jnp.broadcast_to(jnp.exp(CHUNK * lg_lane).reshape(n_tiles, MXU_DIM, 1),
                            (n_tiles, MXU_DIM, MXU_DIM))
    rel = idx - (jnp.arange(MXU_DIM) % HEAD_DIM).astype(F32)[None, :]
    lg_tiles = lg_lane.reshape(n_tiles, 1, MXU_DIM)
    dmask = jnp.where((rel >= 0)[None], jnp.exp(jnp.where(rel >= 0, rel, 0.0)[None] * lg_tiles), 0.0)
    return dmask, qdec, kdec, cdec


def _rotary_tables(s):
    pos = jnp.arange(s, dtype=F32)
    inv_freq = ROPE_BASE ** (-jnp.arange(0, HEAD_DIM, 2, dtype=F32) / HEAD_DIM)
    ang = pos[:, None] * inv_freq[None, :]
    cos, sin = jnp.cos(ang), jnp.sin(ang)
    reps = LANES // HEAD_DIM
    cos_t = jnp.tile(jnp.concatenate([cos, cos], axis=1), (1, reps))
    sin_t = jnp.tile(jnp.concatenate([-sin, sin], axis=1), (1, reps))
    return cos_t, sin_t


def kernel(x, p, ffn1_w_gu, ffn1_w_down, ln1_g, ln1_b, w_in, ret_gn_g, ret_gn_b, rw_mu, rw_w0, rw_w_up, rw_a0, rw_a_up, rw_g_up, rw_k_k, rw_k_a, rw_r_k, rw_gn_g, rw_gn_b, w_out, ln2_g, ln2_b, ffn2_w_gu, ffn2_w_down, ln3_g, ln3_b, ple_w_proj, ple_w_gate, ple_b_gate):
    bsz, s, d = x.shape
    depth = ffn1_w_gu.shape[0]
    alpha = (2.0 * depth) ** 0.25
    width = rw_w0.shape[1]
    n_heads = width // HEAD_DIM
    ret_cols = 4 * width
    lora = DECAY_LORA + AAA_LORA + GATE_LORA
    lora_pad = -(-lora // LANES) * LANES

    dmask, qdec, kdec, cdec = _retention_tables(n_heads)
    cos_t, sin_t = _rotary_tables(s)
    row = lambda v: v.reshape(1, -1)

    h = x.reshape(bsz * s, d)
    for i in range(depth):
        h = _ffn_call(h, ffn1_w_gu[i], ffn1_w_down[i],
                      row(ln1_g[i]), row(ln1_b[i]), alpha)

        wlo = jnp.pad(w_in[i][:, ret_cols + 3 * width:].astype(BF16), ((0, 0), (0, lora_pad - lora)))
        mu = row(jnp.pad(rw_mu[i], (0, lora_pad - lora)))
        wlora = jnp.zeros((lora_pad, 3 * width), BF16)
        wlora = wlora.at[:DECAY_LORA, :width].set(rw_w_up[i].astype(BF16))
        wlora = wlora.at[DECAY_LORA:DECAY_LORA + AAA_LORA, width:2 * width].set(rw_a_up[i].astype(BF16))
        wlora = wlora.at[DECAY_LORA + AAA_LORA:lora, 2 * width:].set(rw_g_up[i].astype(BF16))
        streams = _inproj_call(h.reshape(bsz, s, d), jnp.swapaxes(w_in[i], 0, 1), wlo, wlora, cos_t, sin_t, mu,
                               row(rw_w0[i]), row(rw_a0[i]), row(rw_k_k[i]), row(rw_k_a[i]))
        mixed = _mix_call(streams, dmask, qdec, kdec, cdec, row(ret_gn_g[i]), row(ret_gn_b[i]),
                          row(rw_gn_g[i]), row(rw_gn_b[i]), row(rw_r_k[i]))
        pre = (mixed.reshape(bsz * s, 2 * width), w_out[i].astype(BF16), row(ln2_g[i]), row(ln2_b[i]))
        ple = (p[i].reshape(bsz * s, -1), ple_w_proj[i].astype(BF16), ple_w_gate[i].astype(BF16),
               row(ple_b_gate[i]))
        h = _ffn_call(h, ffn2_w_gu[i], ffn2_w_down[i].astype(BF16),
                      row(ln3_g[i]), row(ln3_b[i]), alpha, pre=pre, ple=ple)
    return h.reshape(bsz, s, d)
```

```python
import functools
import math

import jax
import jax.numpy as jnp
from jax import lax
from jax.experimental import pallas as pl
from jax.experimental.pallas import tpu as pltpu

D_PLE = 256
HEAD_DIM = 64
ROPE_BASE = 10000.0
DECAY_LORA = 64
AAA_LORA = 64
GATE_LORA = 160
LN_EPS = 1e-5
RET_GN_EPS = 1e-5
RWKV_GN_EPS = 64e-5
DECAY_SCALE = math.exp(-0.5)
HEAD_SHIFT = HEAD_DIM.bit_length() - 1

LANES = 128
MXU_DIM = 256
HEADS_PER_TILE = MXU_DIM // HEAD_DIM
CHUNK = 64
VMEM_LIMIT_BYTES = 56 * 1024 * 1024

F32 = jnp.float32
BF16 = jnp.bfloat16


def _dot(a, b):
    return jnp.dot(a, b, preferred_element_type=F32)


def _dot_nt(a, b):
    return lax.dot_general(a, b, (((1,), (1,)), ((), ())), preferred_element_type=F32)


def _dot_tn(a, b):
    return lax.dot_general(a, b, (((0,), (0,)), ((), ())), preferred_element_type=F32)


def _layer_norm(y, g, b):
    mu = jnp.mean(y, axis=-1, keepdims=True)
    yc = y - mu
    var = jnp.mean(yc * yc, axis=-1, keepdims=True)
    return yc * lax.rsqrt(var + LN_EPS) * g + b


def _const_spec(shape):
    nd = len(shape)
    return pl.BlockSpec(shape, lambda *_: (0,) * nd, pipeline_mode=pl.Buffered(1))


def _ffn_kernel(*refs, alpha, fc, with_pre, with_ple):
    refs = list(refs)
    x_ref = refs.pop(0)
    if with_pre:
        m_ref, wo_ref, gp_ref, bp_ref = refs[:4]
        refs = refs[4:]
    wgu_ref, wd_ref, g_ref, b_ref = refs[:4]
    refs = refs[4:]
    if with_ple:
        p_ref, wp_ref, wpg_ref, bpg_ref = refs[:4]
        refs = refs[4:]
    o_ref, act_ref = refs[:2]
    d_ff = wd_ref.shape[0]
    if with_pre:
        xs_ref = refs[2]
        xs_ref[...] = _layer_norm(alpha * x_ref[...] + _dot(m_ref[...], wo_ref[...]), gp_ref[...], bp_ref[...])
    else:
        xs_ref = x_ref
    xb = xs_ref[...].astype(BF16)
    for c in range(d_ff // fc):
        hg = _dot(xb, wgu_ref[:, c * fc:(c + 1) * fc].astype(BF16))
        hu = _dot(xb, wgu_ref[:, d_ff + c * fc:d_ff + (c + 1) * fc].astype(BF16))
        act_ref[:, c * fc:(c + 1) * fc] = (hg * jax.nn.sigmoid(hg) * hu).astype(BF16)
    down = _dot(act_ref[...], wd_ref[...].astype(BF16))
    y = _layer_norm(alpha * xs_ref[...] + 0.5 * down, g_ref[...], b_ref[...])
    if with_ple:
        gate = jax.nn.sigmoid(_dot(y.astype(BF16), wpg_ref[...]) + bpg_ref[...])
        y = y + gate * _dot(p_ref[...].astype(BF16), wp_ref[...])
    o_ref[...] = y


def _ffn_call(x2d, wgu, wd, ln_g, ln_b, alpha, pre=None, ple=None, tm=512, fc=MXU_DIM):
    n, d = x2d.shape
    d_ff = wd.shape[0]
    tm = min(tm, n)
    row = lambda i: (i, 0)
    in_specs = [pl.BlockSpec((tm, d), row)]
    args = [x2d]
    if pre is not None:
        m2d, wo, gp, bp = pre
        in_specs += [pl.BlockSpec((tm, m2d.shape[1]), row), _const_spec(wo.shape),
                     _const_spec((1, d)), _const_spec((1, d))]
        args += [m2d, wo, gp, bp]
    in_specs += [_const_spec(wgu.shape), _const_spec(wd.shape), _const_spec((1, d)), _const_spec((1, d))]
    args += [wgu, wd, ln_g, ln_b]
    if ple is not None:
        p2d, wp, wpg, bpg = ple
        in_specs += [pl.BlockSpec((tm, p2d.shape[1]), row), _const_spec(wp.shape),
                     _const_spec(wpg.shape), _const_spec((1, d))]
        args += [p2d, wp, wpg, bpg]
    return pl.pallas_call(
        functools.partial(_ffn_kernel, alpha=alpha, fc=fc, with_pre=pre is not None,
                          with_ple=ple is not None),
        grid=(n // tm,),
        in_specs=in_specs,
        out_specs=pl.BlockSpec((tm, d), row),
        out_shape=jax.ShapeDtypeStruct((n, d), F32),
        scratch_shapes=[pltpu.VMEM((tm, d_ff), BF16)] + ([pltpu.VMEM((tm, d), F32)] if pre is not None else []),
        compiler_params=pltpu.CompilerParams(dimension_semantics=("arbitrary",),
                                             vmem_limit_bytes=VMEM_LIMIT_BYTES),
        name="ffn_ln",
    )(*args)


def _group_sum(x, ones_blk):
    outs = [_dot(x[:, t * MXU_DIM:(t + 1) * MXU_DIM].astype(BF16), ones_blk)
            for t in range(x.shape[1] // MXU_DIM)]
    return jnp.concatenate(outs, axis=1)


def _inproj_kernel(x_ref, win_ref, wlo_ref, wda_ref, wg_ref, cos_ref, sin_ref, mu_ref, w0_ref, a0_ref,
                   kk_ref, ka_ref,
                   q_ref, k_ref, v_ref, g_ref, r_ref, km_ref, vr_ref, lwh_ref, lwl_ref, a_ref, kku_ref,
                   gate_ref, carry_ref, *, width):
    tm = x_ref.shape[0]

    @pl.when(pl.program_id(1) == 0)
    def _():
        carry_ref[...] = jnp.zeros_like(carry_ref)

    xb = x_ref[...].astype(BF16)

    lane = lax.broadcasted_iota(jnp.int32, (tm, width), 1)
    first_half = (lane & (HEAD_DIM // 2)) == 0
    reps = width // cos_ref.shape[1]
    cos = jnp.concatenate([cos_ref[...]] * reps, axis=1)
    sin = jnp.concatenate([sin_ref[...]] * reps, axis=1)

    def rotary(t):
        swapped = jnp.where(first_half, pltpu.roll(t, width - HEAD_DIM // 2, 1),
                            pltpu.roll(t, HEAD_DIM // 2, 1))
        return t * cos + swapped * sin

    def put(ref, val):
        ref[...] = val.astype(ref.dtype)

    ret_cols = 4 * width

    def w_cols(lo_col, hi_col):
        return win_ref[:, lo_col:hi_col].astype(BF16)

    def shifted(lo_col, hi_col, w):
        z = _dot(xb, w)
        first = carry_ref[0:1, lo_col:hi_col]
        carry_ref[0:1, lo_col:hi_col] = z[tm - 1:tm, :]
        rowid = lax.broadcasted_iota(jnp.int32, z.shape, 0)
        prev = jnp.where(rowid == 0, first, pltpu.roll(z, 1, 0))
        return z + (prev - z) * mu_ref[:, lo_col:hi_col]

    lo = shifted(3 * width, 3 * width + wlo_ref.shape[1], wlo_ref[...])
    n_da = DECAY_LORA + AAA_LORA
    lo_da = lo[:, 0:n_da]
    ll = lax.broadcasted_iota(jnp.int32, lo_da.shape, 1)
    act_da = jnp.where(ll < DECAY_LORA, jnp.tanh(lo_da), lo_da).astype(BF16)
    up = _dot(act_da, wda_ref[...])
    put(gate_ref, _dot(jax.nn.sigmoid(lo[:, n_da:]).astype(BF16), wg_ref[...]))
    lw = -DECAY_SCALE * jax.nn.sigmoid(w0_ref[...] + up[:, 0 * width:1 * width])
    lw_hi, lw_lo = _split_hi_lo(lw)
    lwh_ref[...] = lw_hi
    lwl_ref[...] = lw_lo
    a = jax.nn.sigmoid(a0_ref[...] + up[:, 1 * width:2 * width])
    put(a_ref, a)
    kr = shifted(1 * width, 2 * width, w_cols(ret_cols + 1 * width, ret_cols + 2 * width))
    put(kku_ref, kr * kk_ref[...])
    put(km_ref, kr * (1.0 + (a - 1.0) * ka_ref[...]))
    put(r_ref, shifted(0 * width, 1 * width, w_cols(ret_cols + 0 * width, ret_cols + 1 * width)))
    put(vr_ref, shifted(2 * width, 3 * width, w_cols(ret_cols + 2 * width, ret_cols + 3 * width)))

    put(q_ref, rotary(_dot(xb, w_cols(0 * width, 1 * width))))
    put(k_ref, rotary(_dot(xb, w_cols(1 * width, 2 * width))) * (HEAD_DIM ** -0.5))
    put(v_ref, _dot(xb, w_cols(2 * width, 3 * width)))
    put(g_ref, _dot(xb, w_cols(3 * width, 4 * width)))


def _inproj_call(x3d, w_in, wlo, wda, wg, cos, sin, mu, w0, a0, k_k, k_a, tm=512):
    bsz, s, d = x3d.shape
    width = w0.shape[1]
    tm = min(tm, s)
    blk = lambda b, j: (b, j, 0)
    out_spec = pl.BlockSpec((None, tm, width), blk)
    n_out = 12
    out_sds = [jax.ShapeDtypeStruct((bsz, s, width), BF16)] * n_out
    return pl.pallas_call(
        functools.partial(_inproj_kernel, width=width),
        grid=(bsz, s // tm),
        in_specs=[pl.BlockSpec((None, tm, d), blk), _const_spec(w_in.shape), _const_spec(wlo.shape),
                  _const_spec(wda.shape), _const_spec(wg.shape),
                  pl.BlockSpec((tm, cos.shape[1]), lambda b, j: (j, 0)),
                  pl.BlockSpec((tm, sin.shape[1]), lambda b, j: (j, 0)),
                  _const_spec(mu.shape), _const_spec(w0.shape), _const_spec(a0.shape),
                  _const_spec(k_k.shape), _const_spec(k_a.shape)],
        out_specs=[out_spec] * n_out,
        out_shape=out_sds,
        scratch_shapes=[pltpu.VMEM((8, mu.shape[1]), F32)],
        compiler_params=pltpu.CompilerParams(dimension_semantics=("arbitrary", "arbitrary"),
                                             vmem_limit_bytes=VMEM_LIMIT_BYTES),
        name="inproj_prep",
    )(x3d, w_in, wlo, wda, wg, cos, sin, mu, w0, a0, k_k, k_a)


def _split_hi_lo(x):
    hi = x.astype(BF16)
    lo = (x - hi.astype(F32)).astype(BF16)
    return hi, lo


def _mix_kernel(q_ref, k_ref, v_ref, r_ref, km_ref, vr_ref, lwh_ref, lwl_ref, a_ref, kku_ref,
                g_ref, gate_ref, rl_ref, kml_ref, vrl_ref,
                dmask_ref, qdec_ref, kdec_ref, cdec_ref, rgn_g_ref, rgn_b_ref, wgn_g_ref, wgn_b_ref,
                rk_ref, o_ref,
                sret_ref, srw_ref, kkn_ref, p_ref, yret_ref, yrw_ref,
                pw_ref, tinv_ref, brp_ref, av_ref, pkt_ref, dec_ref, qr_ref, inner_ref, qd_ref, kv_ref):
    bsz, _, width = q_ref.shape
    n_tiles = width // MXU_DIM
    n_chains = bsz * n_tiles
    hpt = HEADS_PER_TILE
    n_rows = bsz * CHUNK

    def full(ref):
        return ref[...].reshape(n_rows, ref.shape[-1]).astype(F32)

    @pl.when(pl.program_id(0) == 0)
    def _():
        sret_ref[...] = jnp.zeros_like(sret_ref)
        srw_ref[...] = jnp.zeros_like(srw_ref)
        yret_ref[...] = jnp.zeros_like(yret_ref)
        yrw_ref[...] = jnp.zeros_like(yrw_ref)

    ri = lax.broadcasted_iota(jnp.int32, (MXU_DIM, MXU_DIM), 0)
    ci = lax.broadcasted_iota(jnp.int32, (MXU_DIM, MXU_DIM), 1)
    same_head = (ri >> HEAD_SHIFT) == (ci >> HEAD_SHIFT)
    ones_blk = same_head.astype(BF16)
    cr = lax.broadcasted_iota(jnp.int32, (CHUNK, MXU_DIM), 0)
    cj = lax.broadcasted_iota(jnp.int32, (CHUNK, MXU_DIM), 1) & (HEAD_DIM - 1)
    strict_lower, incl_lower = cr > cj, cr >= cj
    eye_c = (cr == cj).astype(F32)
    ti = lax.broadcasted_iota(jnp.int32, (2 * CHUNK, CHUNK), 0)
    tj = lax.broadcasted_iota(jnp.int32, (2 * CHUNK, CHUNK), 1)
    tri = (((ti < CHUNK) & (ti >= tj)) | ((ti >= CHUNK) & (ti - CHUNK > tj))).astype(BF16)

    heads_per_lane_tile = LANES // HEAD_DIM
    diag_tiles = [(slice(h * HEAD_DIM, (h + 1) * HEAD_DIM),
                   slice((h // heads_per_lane_tile) * LANES, (h // heads_per_lane_tile + 1) * LANES))
                  for h in range(hpt)]

    def state_bf16(ref, c):
        zeros = jnp.zeros((HEAD_DIM, LANES), BF16)
        rows = []
        for rs, ls in diag_tiles:
            parts = [zeros] * (MXU_DIM // LANES)
            parts[ls.start // LANES] = ref[c, rs, ls].astype(BF16)
            rows.append(jnp.concatenate(parts, axis=1))
        return jnp.concatenate(rows, axis=0)

    def block_diag(x):
        return jnp.where(same_head, jnp.concatenate([x] * hpt, axis=0), jnp.zeros((), x.dtype))

    def group_sum(x):
        return _group_sum(x, ones_blk)

    kku = full(kku_ref)
    kkn = kku * lax.rsqrt(jnp.maximum(group_sum(kku * kku), 1e-24))
    kkn_ref[...] = kkn
    p_ref[...] = -(kkn * full(a_ref))

    def prepare(b, carry):
        rows = pl.ds(b * CHUNK, CHUNK)
        cums = _dot(tri, lwh_ref[b]) + _dot(tri, lwl_ref[b])
        cum = cums[0:CHUNK]
        e_in = jnp.exp(cum)
        e_ex = jnp.exp(cums[CHUNK:])
        e_neg = jnp.exp(-cum)
        e_last = e_in[CHUNK - 1:CHUNK, :]
        qt_all = kkn_ref[rows, :] * e_ex
        rt_all = r_ref[b].astype(F32) * e_in
        pt_all = p_ref[rows, :] * e_neg
        kt_all = km_ref[b].astype(F32) * e_neg
        ph_all = pt_all * e_last
        kh_all = kt_all * e_last
        vr_all = vr_ref[b].astype(BF16)
        q_all = q_ref[b]
        k_all = k_ref[b]
        v_all = v_ref[b].astype(BF16)
        qd_all = (q_all.astype(F32) * qdec_ref[...]).astype(BF16)
        kd_all = (k_all.astype(F32) * kdec_ref[...]).astype(BF16)
        qt_b, rt_b = qt_all.astype(BF16), rt_all.astype(BF16)
        pt_b, kt_b = pt_all.astype(BF16), kt_all.astype(BF16)
        q_b, k_b = q_all.astype(BF16), k_all.astype(BF16)

        for t in range(n_tiles):
            c = b * n_tiles + t
            sl = slice(t * MXU_DIM, (t + 1) * MXU_DIM)
            qr = jnp.concatenate([qt_b[:, sl], rt_b[:, sl]], axis=0)
            qr_ref[c] = qr
            g_p = _dot_nt(qr, block_diag(pt_b[:, sl]))
            g_k = _dot_nt(qr, block_diag(kt_b[:, sl]))
            a_qp = jnp.where(strict_lower, g_p[0:CHUNK], 0.0)
            pw_ref[c] = a_qp.astype(BF16)
            tinv_ref[c] = a_qp + eye_c
            brp_ref[c] = jnp.where(incl_lower, g_p[CHUNK:], 0.0).astype(BF16)
            ab = jnp.concatenate([jnp.where(strict_lower, g_k[0:CHUNK], 0.0),
                                  jnp.where(incl_lower, g_k[CHUNK:], 0.0)], axis=0).astype(BF16)
            av_ref[c] = _dot(ab, block_diag(vr_all[:, sl]))
            pkt_ref[c] = jnp.transpose(jnp.concatenate([ph_all[:, sl], kh_all[:, sl]], axis=0)).astype(BF16)
            dec_ref[c] = jnp.transpose(jnp.broadcast_to(e_last[:, sl], (LANES, MXU_DIM)))
            scores = (_dot_nt(q_b[:, sl], block_diag(k_b[:, sl])) * dmask_ref[t]).astype(BF16)
            inner_ref[c] = _dot(scores, block_diag(v_all[:, sl]))
            qd_ref[c] = qd_all[:, sl]
            kv = _dot_tn(kd_all[:, sl], v_all[:, sl])
            for rs, ls in diag_tiles:
                kv_ref[c, rs, ls] = jnp.where(same_head[rs, ls], kv[rs, ls], 0.0)
        return carry

    for b in range(bsz):
        prepare(b, 0)

    inv_hd = 1.0 / HEAD_DIM

    def group_norm(y, eps):
        mu = group_sum(y) * inv_hd
        yc = y - mu
        var = group_sum(yc * yc) * inv_hd
        return yc * lax.rsqrt(var + eps)

    def norm_retention():
        yret_ref[...] = group_norm(yret_ref[...], RET_GN_EPS) * rgn_g_ref[...] + rgn_b_ref[...]

    def norm_rwkv():
        bonus = group_sum(full(rl_ref) * full(kml_ref) * rk_ref[...]) * full(vrl_ref)
        yrw_ref[...] = group_norm(yrw_ref[...], RWKV_GN_EPS) * wgn_g_ref[...] + wgn_b_ref[...] + bonus

    def gate_and_store(b_lo, b_hi):
        for b in range(b_lo, b_hi):
            rows = pl.ds(b * CHUNK, CHUNK)
            g = g_ref[b].astype(F32)
            o_ref[b, :, 0:width] = (g * jax.nn.sigmoid(g) * yret_ref[rows, :]).astype(o_ref.dtype)
            o_ref[b, :, width:2 * width] = (yrw_ref[rows, :] * gate_ref[b].astype(F32)).astype(o_ref.dtype)

    filler = [norm_retention, norm_rwkv, functools.partial(gate_and_store, 0, bsz // 2),
              functools.partial(gate_and_store, bsz // 2, bsz)]

    for c in range(n_chains):
        pwb = pw_ref[c]
        pw_ref[c] = _dot(pwb, block_diag(pwb)).astype(BF16)
    for rnd in range(4):
        for c in range(n_chains):
            pwb, t_acc = pw_ref[c], tinv_ref[c]
            res = _dot(jnp.concatenate([pwb, t_acc.astype(BF16)], axis=0), block_diag(pwb))
            pw_ref[c] = res[0:CHUNK].astype(BF16)
            tinv_ref[c] = t_acc + res[CHUNK:]
        filler[rnd]()
    for c in range(n_chains):
        t_acc = tinv_ref[c]
        tinv_ref[c] = t_acc + _dot(t_acc.astype(BF16), block_diag(pw_ref[c]))

    chains = [(c, c // n_tiles, pl.ds((c // n_tiles) * CHUNK, CHUNK),
               slice((c % n_tiles) * MXU_DIM, (c % n_tiles + 1) * MXU_DIM)) for c in range(n_chains)]
    qrm = [_dot(qr_ref[c], state_bf16(srw_ref, c)) for c, _, _, _ in chains]
    u = [_dot(tinv_ref[c].astype(BF16), block_diag((qrm[c][0:CHUNK] + av_ref[c, 0:CHUNK, :]).astype(BF16)))
         for c, _, _, _ in chains]
    for c, b, rows, sl in chains:
        ub = u[c].astype(BF16)
        yrw_ref[rows, sl] = qrm[c][CHUNK:] + av_ref[c, CHUNK:, :] + _dot(brp_ref[c], block_diag(ub))
        dec = dec_ref[c]
        upd = _dot(pkt_ref[c], jnp.concatenate([ub, vr_ref[b][:, sl]], axis=0))
        yret_ref[rows, sl] = inner_ref[c] + _dot(qd_ref[c], state_bf16(sret_ref, c))
        for rs, ls in diag_tiles:
            srw_ref[c, rs, ls] = (srw_ref[c, rs, ls] * dec[rs, :]
                                  + jnp.where(same_head[rs, ls], upd[rs, ls], 0.0))
            sret_ref[c, rs, ls] = sret_ref[c, rs, ls] * cdec_ref[c % n_tiles, rs, ls] + kv_ref[c, rs, ls]


def _mix_call(streams, dmask, qdec, kdec, cdec, rgn_g, rgn_b, wgn_g, wgn_b, rk):
    q, k, v, g, r, km, vr, lwh, lwl, a, kku, gate = streams
    bsz, s, width = q.shape
    n_tiles = width // MXU_DIM
    n_steps = s // CHUNK
    cur = pl.BlockSpec((bsz, CHUNK, width), lambda j: (0, jnp.minimum(j, n_steps - 1), 0))
    prev_idx = lambda j: (0, jnp.maximum(j - 1, 0), 0)
    prev = pl.BlockSpec((bsz, CHUNK, width), prev_idx)
    current_streams = [q, k, v, r, km, vr, lwh, lwl, a, kku]
    previous_streams = [g, gate, r, km, vr]
    consts = [dmask, qdec, kdec, cdec, rgn_g, rgn_b, wgn_g, wgn_b, rk]
    n_chains = bsz * n_tiles
    tc = bsz * CHUNK
    tile = (n_chains, MXU_DIM, MXU_DIM)
    flat = (n_chains, CHUNK, MXU_DIM)
    pair = (n_chains, 2 * CHUNK, MXU_DIM)
    return pl.pallas_call(
        _mix_kernel,
        grid=(n_steps + 1,),
        in_specs=([cur] * len(current_streams) + [prev] * len(previous_streams)
                  + [_const_spec(c.shape) for c in consts]),
        out_specs=pl.BlockSpec((bsz, CHUNK, 2 * width), prev_idx),
        out_shape=jax.ShapeDtypeStruct((bsz, s, 2 * width), BF16),
        scratch_shapes=[pltpu.VMEM(tile, F32),
                        pltpu.VMEM(tile, F32),
                        pltpu.VMEM((tc, width), F32), pltpu.VMEM((tc, width), F32),
                        pltpu.VMEM((tc, width), F32), pltpu.VMEM((tc, width), F32),
                        pltpu.VMEM(flat, BF16), pltpu.VMEM(flat, F32), pltpu.VMEM(flat, BF16),
                        pltpu.VMEM(pair, F32),
                        pltpu.VMEM((n_chains, MXU_DIM, 2 * CHUNK), BF16),
                        pltpu.VMEM((n_chains, MXU_DIM, LANES), F32),
                        pltpu.VMEM(pair, BF16),
                        pltpu.VMEM(flat, F32), pltpu.VMEM(flat, BF16), pltpu.VMEM(tile, F32)],
        compiler_params=pltpu.CompilerParams(dimension_semantics=("arbitrary",),
                                             vmem_limit_bytes=VMEM_LIMIT_BYTES),
        name="mix_recurrences",
    )(*current_streams, *previous_streams, *consts)


def _retention_tables(n_heads):
    h = jnp.arange(n_heads, dtype=F32)
    log_gamma = jnp.log1p(-jnp.exp2(-5.0 - h))
    lg_lane = jnp.repeat(log_gamma, HEAD_DIM)[None, :]
    idx = jnp.arange(CHUNK, dtype=F32)[:, None]
    qdec = jnp.exp((idx + 1.0) * lg_lane)
    kdec = jnp.exp((CHUNK - 1.0 - idx) * lg_lane)
    n_tiles = n_heads // HEADS_PER_TILE
    cdec = jnp.broadcast_to(jnp.exp(CHUNK * lg_lane).reshape(n_tiles, MXU_DIM, 1),
                            (n_tiles, MXU_DIM, MXU_DIM))
    rel = idx - (jnp.arange(MXU_DIM) % HEAD_DIM).astype(F32)[None, :]
    lg_tiles = lg_lane.reshape(n_tiles, 1, MXU_DIM)
    dmask = jnp.where((rel >= 0)[None], jnp.exp(jnp.where(rel >= 0, rel, 0.0)[None] * lg_tiles), 0.0)
    return dmask, qdec, kdec, cdec


def _rotary_tables(s):
    pos = jnp.arange(s, dtype=F32)
    inv_freq = ROPE_BASE ** (-jnp.arange(0, HEAD_DIM, 2, dtype=F32) / HEAD_DIM)
    ang = pos[:, None] * inv_freq[None, :]
    cos, sin = jnp.cos(ang), jnp.sin(ang)
    reps = LANES // HEAD_DIM
    cos_t = jnp.tile(jnp.concatenate([cos, cos], axis=1), (1, reps))
    sin_t = jnp.tile(jnp.concatenate([-sin, sin], axis=1), (1, reps))
    return cos_t, sin_t


def kernel(x, p, ffn1_w_gu, ffn1_w_down, ln1_g, ln1_b, w_in, ret_gn_g, ret_gn_b, rw_mu, rw_w0, rw_w_up, rw_a0, rw_a_up, rw_g_up, rw_k_k, rw_k_a, rw_r_k, rw_gn_g, rw_gn_b, w_out, ln2_g, ln2_b, ffn2_w_gu, ffn2_w_down, ln3_g, ln3_b, ple_w_proj, ple_w_gate, ple_b_gate):
    bsz, s, d = x.shape
    depth = ffn1_w_gu.shape[0]
    alpha = (2.0 * depth) ** 0.25
    width = rw_w0.shape[1]
    n_heads = width // HEAD_DIM
    ret_cols = 4 * width
    lora = DECAY_LORA + AAA_LORA + GATE_LORA
    lora_pad = -(-lora // LANES) * LANES

    dmask, qdec, kdec, cdec = _retention_tables(n_heads)
    cos_t, sin_t = _rotary_tables(s)
    row = lambda v: v.reshape(1, -1)

    h = x.reshape(bsz * s, d)
    for i in range(depth):
        h = _ffn_call(h, ffn1_w_gu[i], ffn1_w_down[i],
                      row(ln1_g[i]), row(ln1_b[i]), alpha)

        wlo = jnp.pad(w_in[i][:, ret_cols + 3 * width:].astype(BF16), ((0, 0), (0, lora_pad - lora)))
        mu = row(jnp.pad(rw_mu[i], (0, lora_pad - lora)))
        n_da = DECAY_LORA + AAA_LORA
        wda = jnp.zeros((n_da, 2 * width), BF16)
        wda = wda.at[:DECAY_LORA, :width].set(rw_w_up[i].astype(BF16))
        wda = wda.at[DECAY_LORA:, width:].set(rw_a_up[i].astype(BF16))
        wg = jnp.pad(rw_g_up[i].astype(BF16), ((0, lora_pad - lora), (0, 0)))
        streams = _inproj_call(h.reshape(bsz, s, d), w_in[i], wlo, wda, wg, cos_t, sin_t, mu,
                               row(rw_w0[i]), row(rw_a0[i]), row(rw_k_k[i]), row(rw_k_a[i]))
        mixed = _mix_call(streams, dmask, qdec, kdec, cdec, row(ret_gn_g[i]), row(ret_gn_b[i]),
                          row(rw_gn_g[i]), row(rw_gn_b[i]), row(rw_r_k[i]))
        pre = (mixed.reshape(bsz * s, 2 * width), w_out[i].astype(BF16), row(ln2_g[i]), row(ln2_b[i]))
        ple = (p[i].reshape(bsz * s, -1), ple_w_proj[i].astype(BF16), ple_w_gate[i].astype(BF16),
               row(ple_b_gate[i]))
        h = _ffn_call(h, ffn2_w_gu[i], ffn2_w_down[i].astype(BF16),
                      row(ln3_g[i]), row(ln3_b[i]), alpha, pre=pre, ple=ple)
    return h.reshape(bsz, s, d)
```

```python
import functools
import math

import jax
import jax.numpy as jnp
from jax import lax
from jax.experimental import pallas as pl
from jax.experimental.pallas import tpu as pltpu

D_PLE = 256
HEAD_DIM = 64
ROPE_BASE = 10000.0
DECAY_LORA = 64
AAA_LORA = 64
GATE_LORA = 160
LN_EPS = 1e-5
RET_GN_EPS = 1e-5
RWKV_GN_EPS = 64e-5
DECAY_SCALE = math.exp(-0.5)
HEAD_SHIFT = HEAD_DIM.bit_length() - 1

LANES = 128
MXU_DIM = 256
HEADS_PER_TILE = MXU_DIM // HEAD_DIM
CHUNK = 64
CHUNKS_PER_STEP = 2
VMEM_LIMIT_BYTES = 56 * 1024 * 1024

F32 = jnp.float32
BF16 = jnp.bfloat16


def _dot(a, b):
    return jnp.dot(a, b, preferred_element_type=F32)


def _dot_nt(a, b):
    return lax.dot_general(a, b, (((1,), (1,)), ((), ())), preferred_element_type=F32)


def _dot_tn(a, b):
    return lax.dot_general(a, b, (((0,), (0,)), ((), ())), preferred_element_type=F32)


def _layer_norm(y, g, b):
    mu = jnp.mean(y, axis=-1, keepdims=True)
    yc = y - mu
    var = jnp.mean(yc * yc, axis=-1, keepdims=True)
    return yc * lax.rsqrt(var + LN_EPS) * g + b


def _const_spec(shape):
    nd = len(shape)
    return pl.BlockSpec(shape, lambda *_: (0,) * nd, pipeline_mode=pl.Buffered(1))


def _ffn_kernel(*refs, alpha, fc, with_pre, with_ple):
    refs = list(refs)
    x_ref = refs.pop(0)
    if with_pre:
        m_ref, wo_ref, gp_ref, bp_ref = refs[:4]
        refs = refs[4:]
    wgu_ref, wd_ref, g_ref, b_ref = refs[:4]
    refs = refs[4:]
    if with_ple:
        p_ref, wp_ref, wpg_ref, bpg_ref = refs[:4]
        refs = refs[4:]
    o_ref, act_ref = refs[:2]
    d_ff = wd_ref.shape[0]
    if with_pre:
        xs_ref = refs[2]
        xs_ref[...] = _layer_norm(alpha * x_ref[...] + _dot(m_ref[...], wo_ref[...]), gp_ref[...], bp_ref[...])
    else:
        xs_ref = x_ref
    xb = xs_ref[...].astype(BF16)
    for c in range(d_ff // fc):
        hg = _dot(xb, wgu_ref[:, c * fc:(c + 1) * fc].astype(BF16))
        hu = _dot(xb, wgu_ref[:, d_ff + c * fc:d_ff + (c + 1) * fc].astype(BF16))
        act_ref[:, c * fc:(c + 1) * fc] = (hg * jax.nn.sigmoid(hg) * hu).astype(BF16)
    down = _dot(act_ref[...], wd_ref[...].astype(BF16))
    y = _layer_norm(alpha * xs_ref[...] + 0.5 * down, g_ref[...], b_ref[...])
    if with_ple:
        gate = jax.nn.sigmoid(_dot(y.astype(BF16), wpg_ref[...]) + bpg_ref[...])
        y = y + gate * _dot(p_ref[...].astype(BF16), wp_ref[...])
    o_ref[...] = y


def _ffn_call(x2d, wgu, wd, ln_g, ln_b, alpha, pre=None, ple=None, tm=512, fc=MXU_DIM):
    n, d = x2d.shape
    d_ff = wd.shape[0]
    tm = min(tm, n)
    row = lambda i: (i, 0)
    in_specs = [pl.BlockSpec((tm, d), row)]
    args = [x2d]
    if pre is not None:
        m2d, wo, gp, bp = pre
        in_specs += [pl.BlockSpec((tm, m2d.shape[1]), row), _const_spec(wo.shape),
                     _const_spec((1, d)), _const_spec((1, d))]
        args += [m2d, wo, gp, bp]
    in_specs += [_const_spec(wgu.shape), _const_spec(wd.shape), _const_spec((1, d)), _const_spec((1, d))]
    args += [wgu, wd, ln_g, ln_b]
    if ple is not None:
        p2d, wp, wpg, bpg = ple
        in_specs += [pl.BlockSpec((tm, p2d.shape[1]), row), _const_spec(wp.shape),
                     _const_spec(wpg.shape), _const_spec((1, d))]
        args += [p2d, wp, wpg, bpg]
    return pl.pallas_call(
        functools.partial(_ffn_kernel, alpha=alpha, fc=fc, with_pre=pre is not None,
                          with_ple=ple is not None),
        grid=(n // tm,),
        in_specs=in_specs,
        out_specs=pl.BlockSpec((tm, d), row),
        out_shape=jax.ShapeDtypeStruct((n, d), F32),
        scratch_shapes=[pltpu.VMEM((tm, d_ff), BF16)] + ([pltpu.VMEM((tm, d), F32)] if pre is not None else []),
        compiler_params=pltpu.CompilerParams(dimension_semantics=("arbitrary",),
                                             vmem_limit_bytes=VMEM_LIMIT_BYTES),
        name="ffn_ln",
    )(*args)


def _group_sum(x, ones_blk):
    outs = [_dot(x[:, t * MXU_DIM:(t + 1) * MXU_DIM].astype(BF16), ones_blk)
            for t in range(x.shape[1] // MXU_DIM)]
    return jnp.concatenate(outs, axis=1)


def _inproj_kernel(x_ref, win_ref, wlo_ref, wda_ref, wg_ref, cos_ref, sin_ref, mu_ref, w0_ref, a0_ref,
                   kk_ref, ka_ref,
                   q_ref, k_ref, v_ref, g_ref, r_ref, km_ref, vr_ref, lwh_ref, lwl_ref, a_ref, kku_ref,
                   gate_ref, carry_ref, *, width):
    tm = x_ref.shape[0]

    @pl.when(pl.program_id(1) == 0)
    def _():
        carry_ref[...] = jnp.zeros_like(carry_ref)

    xb = x_ref[...].astype(BF16)

    lane = lax.broadcasted_iota(jnp.int32, (tm, width), 1)
    first_half = (lane & (HEAD_DIM // 2)) == 0
    reps = width // cos_ref.shape[1]
    cos = jnp.concatenate([cos_ref[...]] * reps, axis=1)
    sin = jnp.concatenate([sin_ref[...]] * reps, axis=1)

    def rotary(t):
        swapped = jnp.where(first_half, pltpu.roll(t, width - HEAD_DIM // 2, 1),
                            pltpu.roll(t, HEAD_DIM // 2, 1))
        return t * cos + swapped * sin

    def put(ref, val):
        ref[...] = val.astype(ref.dtype)

    ret_cols = 4 * width

    def w_cols(lo_col, hi_col):
        return win_ref[:, lo_col:hi_col].astype(BF16)

    def shifted(lo_col, hi_col, w):
        z = _dot(xb, w)
        first = carry_ref[0:1, lo_col:hi_col]
        carry_ref[0:1, lo_col:hi_col] = z[tm - 1:tm, :]
        rowid = lax.broadcasted_iota(jnp.int32, z.shape, 0)
        prev = jnp.where(rowid == 0, first, pltpu.roll(z, 1, 0))
        return z + (prev - z) * mu_ref[:, lo_col:hi_col]

    lo = shifted(3 * width, 3 * width + wlo_ref.shape[1], wlo_ref[...])
    n_da = DECAY_LORA + AAA_LORA
    lo_da = lo[:, 0:n_da]
    ll = lax.broadcasted_iota(jnp.int32, lo_da.shape, 1)
    act_da = jnp.where(ll < DECAY_LORA, jnp.tanh(lo_da), lo_da).astype(BF16)
    up = _dot(act_da, wda_ref[...])
    put(gate_ref, _dot(jax.nn.sigmoid(lo[:, n_da:]).astype(BF16), wg_ref[...]))
    lw = -DECAY_SCALE * jax.nn.sigmoid(w0_ref[...] + up[:, 0 * width:1 * width])
    lw_hi, lw_lo = _split_hi_lo(lw)
    lwh_ref[...] = lw_hi
    lwl_ref[...] = lw_lo
    a = jax.nn.sigmoid(a0_ref[...] + up[:, 1 * width:2 * width])
    put(a_ref, a)
    kr = shifted(1 * width, 2 * width, w_cols(ret_cols + 1 * width, ret_cols + 2 * width))
    put(kku_ref, kr * kk_ref[...])
    put(km_ref, kr * (1.0 + (a - 1.0) * ka_ref[...]))
    put(r_ref, shifted(0 * width, 1 * width, w_cols(ret_cols + 0 * width, ret_cols + 1 * width)))
    put(vr_ref, shifted(2 * width, 3 * width, w_cols(ret_cols + 2 * width, ret_cols + 3 * width)))

    put(q_ref, rotary(_dot(xb, w_cols(0 * width, 1 * width))))
    put(k_ref, rotary(_dot(xb, w_cols(1 * width, 2 * width))) * (HEAD_DIM ** -0.5))
    put(v_ref, _dot(xb, w_cols(2 * width, 3 * width)))
    put(g_ref, _dot(xb, w_cols(3 * width, 4 * width)))


def _inproj_call(x3d, w_in, wlo, wda, wg, cos, sin, mu, w0, a0, k_k, k_a, tm=512):
    bsz, s, d = x3d.shape
    width = w0.shape[1]
    tm = min(tm, s)
    blk = lambda b, j: (b, j, 0)
    out_spec = pl.BlockSpec((None, tm, width), blk)
    n_out = 12
    out_sds = [jax.ShapeDtypeStruct((bsz, s, width), BF16)] * n_out
    return pl.pallas_call(
        functools.partial(_inproj_kernel, width=width),
        grid=(bsz, s // tm),
        in_specs=[pl.BlockSpec((None, tm, d), blk), _const_spec(w_in.shape), _const_spec(wlo.shape),
                  _const_spec(wda.shape), _const_spec(wg.shape),
                  pl.BlockSpec((tm, cos.shape[1]), lambda b, j: (j, 0)),
                  pl.BlockSpec((tm, sin.shape[1]), lambda b, j: (j, 0)),
                  _const_spec(mu.shape), _const_spec(w0.shape), _const_spec(a0.shape),
                  _const_spec(k_k.shape), _const_spec(k_a.shape)],
        out_specs=[out_spec] * n_out,
        out_shape=out_sds,
        scratch_shapes=[pltpu.VMEM((8, mu.shape[1]), F32)],
        compiler_params=pltpu.CompilerParams(dimension_semantics=("arbitrary", "arbitrary"),
                                             vmem_limit_bytes=VMEM_LIMIT_BYTES),
        name="inproj_prep",
    )(x3d, w_in, wlo, wda, wg, cos, sin, mu, w0, a0, k_k, k_a)


def _split_hi_lo(x):
    hi = x.astype(BF16)
    lo = (x - hi.astype(F32)).astype(BF16)
    return hi, lo


def _mix_kernel(q_ref, k_ref, v_ref, r_ref, km_ref, vr_ref, lwh_ref, lwl_ref, a_ref, kku_ref,
                g_ref, gate_ref, rl_ref, kml_ref, vrl_ref,
                dmask_ref, qdec_ref, kdec_ref, cdec_ref, rgn_g_ref, rgn_b_ref, wgn_g_ref, wgn_b_ref,
                rk_ref, o_ref,
                sret_ref, srw_ref, kkn_ref, p_ref, yret_ref, yrw_ref,
                pw_ref, tinv_ref, brp_ref, av_ref, pkt_ref, dec_ref, qr_ref, inner_ref, qd_ref, kv_ref):
    bsz, blk_rows, width = q_ref.shape
    n_sub = blk_rows // CHUNK
    n_tiles = width // MXU_DIM
    n_states = bsz * n_tiles
    n_chains = n_sub * n_states
    hpt = HEADS_PER_TILE
    n_rows = bsz * blk_rows

    def full(ref):
        return ref[...].reshape(n_rows, ref.shape[-1]).astype(F32)

    @pl.when(pl.program_id(0) == 0)
    def _():
        sret_ref[...] = jnp.zeros_like(sret_ref)
        srw_ref[...] = jnp.zeros_like(srw_ref)
        yret_ref[...] = jnp.zeros_like(yret_ref)
        yrw_ref[...] = jnp.zeros_like(yrw_ref)

    ri = lax.broadcasted_iota(jnp.int32, (MXU_DIM, MXU_DIM), 0)
    ci = lax.broadcasted_iota(jnp.int32, (MXU_DIM, MXU_DIM), 1)
    same_head = (ri >> HEAD_SHIFT) == (ci >> HEAD_SHIFT)
    ones_blk = same_head.astype(BF16)
    cr = lax.broadcasted_iota(jnp.int32, (CHUNK, MXU_DIM), 0)
    cj = lax.broadcasted_iota(jnp.int32, (CHUNK, MXU_DIM), 1) & (HEAD_DIM - 1)
    strict_lower, incl_lower = cr > cj, cr >= cj
    eye_c = (cr == cj).astype(F32)
    ti = lax.broadcasted_iota(jnp.int32, (2 * CHUNK, CHUNK), 0)
    tj = lax.broadcasted_iota(jnp.int32, (2 * CHUNK, CHUNK), 1)
    tri = (((ti < CHUNK) & (ti >= tj)) | ((ti >= CHUNK) & (ti - CHUNK > tj))).astype(BF16)

    heads_per_lane_tile = LANES // HEAD_DIM
    diag_tiles = [(slice(h * HEAD_DIM, (h + 1) * HEAD_DIM),
                   slice((h // heads_per_lane_tile) * LANES, (h // heads_per_lane_tile + 1) * LANES))
                  for h in range(hpt)]

    def state_bf16(ref, c):
        zeros = jnp.zeros((HEAD_DIM, LANES), BF16)
        rows = []
        for rs, ls in diag_tiles:
            parts = [zeros] * (MXU_DIM // LANES)
            parts[ls.start // LANES] = ref[c, rs, ls].astype(BF16)
            rows.append(jnp.concatenate(parts, axis=1))
        return jnp.concatenate(rows, axis=0)

    def block_diag(x):
        return jnp.where(same_head, jnp.concatenate([x] * hpt, axis=0), jnp.zeros((), x.dtype))

    def group_sum(x):
        return _group_sum(x, ones_blk)

    kku = full(kku_ref)
    kkn = kku * lax.rsqrt(jnp.maximum(group_sum(kku * kku), 1e-24))
    kkn_ref[...] = kkn
    p_ref[...] = -(kkn * full(a_ref))

    def prepare(sub, b):
        rows = pl.ds(b * blk_rows + sub * CHUNK, CHUNK)
        part = (b, slice(sub * CHUNK, (sub + 1) * CHUNK))
        cums = _dot(tri, lwh_ref[part]) + _dot(tri, lwl_ref[part])
        cum = cums[0:CHUNK]
        e_in = jnp.exp(cum)
        e_ex = jnp.exp(cums[CHUNK:])
        e_neg = jnp.exp(-cum)
        e_last = e_in[CHUNK - 1:CHUNK, :]
        qt_all = kkn_ref[rows, :] * e_ex
        rt_all = r_ref[part].astype(F32) * e_in
        pt_all = p_ref[rows, :] * e_neg
        kt_all = km_ref[part].astype(F32) * e_neg
        ph_all = pt_all * e_last
        kh_all = kt_all * e_last
        vr_all = vr_ref[part].astype(BF16)
        q_all = q_ref[part]
        k_all = k_ref[part]
        v_all = v_ref[part].astype(BF16)
        qd_all = (q_all.astype(F32) * qdec_ref[...]).astype(BF16)
        kd_all = (k_all.astype(F32) * kdec_ref[...]).astype(BF16)
        qt_b, rt_b = qt_all.astype(BF16), rt_all.astype(BF16)
        pt_b, kt_b = pt_all.astype(BF16), kt_all.astype(BF16)
        q_b, k_b = q_all.astype(BF16), k_all.astype(BF16)

        for t in range(n_tiles):
            c = sub * n_states + b * n_tiles + t
            sl = slice(t * MXU_DIM, (t + 1) * MXU_DIM)
            qr = jnp.concatenate([qt_b[:, sl], rt_b[:, sl]], axis=0)
            qr_ref[c] = qr
            g_p = _dot_nt(qr, block_diag(pt_b[:, sl]))
            g_k = _dot_nt(qr, block_diag(kt_b[:, sl]))
            a_qp = jnp.where(strict_lower, g_p[0:CHUNK], 0.0)
            pw_ref[c] = a_qp.astype(BF16)
            tinv_ref[c] = a_qp + eye_c
            brp_ref[c] = jnp.where(incl_lower, g_p[CHUNK:], 0.0).astype(BF16)
            ab = jnp.concatenate([jnp.where(strict_lower, g_k[0:CHUNK], 0.0),
                                  jnp.where(incl_lower, g_k[CHUNK:], 0.0)], axis=0).astype(BF16)
            av_ref[c] = _dot(ab, block_diag(vr_all[:, sl]))
            pkt_ref[c] = jnp.transpose(jnp.concatenate([ph_all[:, sl], kh_all[:, sl]], axis=0)).astype(BF16)
            dec_ref[c] = jnp.transpose(jnp.broadcast_to(e_last[:, sl], (LANES, MXU_DIM)))
            scores = (_dot_nt(q_b[:, sl], block_diag(k_b[:, sl])) * dmask_ref[t]).astype(BF16)
            inner_ref[c] = _dot(scores, block_diag(v_all[:, sl]))
            qd_ref[c] = qd_all[:, sl]
            kv = _dot_tn(kd_all[:, sl], v_all[:, sl])
            for rs, ls in diag_tiles:
                kv_ref[c, rs, ls] = jnp.where(same_head[rs, ls], kv[rs, ls], 0.0)

    for sub in range(n_sub):
        for b in range(bsz):
            prepare(sub, b)

    inv_hd = 1.0 / HEAD_DIM

    def group_norm(y, eps):
        mu = group_sum(y) * inv_hd
        yc = y - mu
        var = group_sum(yc * yc) * inv_hd
        return yc * lax.rsqrt(var + eps)

    def norm_retention():
        yret_ref[...] = group_norm(yret_ref[...], RET_GN_EPS) * rgn_g_ref[...] + rgn_b_ref[...]

    def norm_rwkv():
        bonus = group_sum(full(rl_ref) * full(kml_ref) * rk_ref[...]) * full(vrl_ref)
        yrw_ref[...] = group_norm(yrw_ref[...], RWKV_GN_EPS) * wgn_g_ref[...] + wgn_b_ref[...] + bonus

    def gate_and_store(b_lo, b_hi):
        for b in range(b_lo, b_hi):
            rows = pl.ds(b * blk_rows, blk_rows)
            g = g_ref[b].astype(F32)
            o_ref[b, :, 0:width] = (g * jax.nn.sigmoid(g) * yret_ref[rows, :]).astype(o_ref.dtype)
            o_ref[b, :, width:2 * width] = (yrw_ref[rows, :] * gate_ref[b].astype(F32)).astype(o_ref.dtype)

    filler = [norm_retention, norm_rwkv, functools.partial(gate_and_store, 0, bsz // 2),
              functools.partial(gate_and_store, bsz // 2, bsz)]

    for c in range(n_chains):
        pwb = pw_ref[c]
        pw_ref[c] = _dot(pwb, block_diag(pwb)).astype(BF16)
    for rnd in range(4):
        for c in range(n_chains):
            pwb, t_acc = pw_ref[c], tinv_ref[c]
            res = _dot(jnp.concatenate([pwb, t_acc.astype(BF16)], axis=0), block_diag(pwb))
            pw_ref[c] = res[0:CHUNK].astype(BF16)
            tinv_ref[c] = t_acc + res[CHUNK:]
        filler[rnd]()
    for c in range(n_chains):
        t_acc = tinv_ref[c]
        tinv_ref[c] = t_acc + _dot(t_acc.astype(BF16), block_diag(pw_ref[c]))

    for sub in range(n_sub):
        chains = [(sub * n_states + st, st, st // n_tiles, (st // n_tiles, slice(sub * CHUNK, (sub + 1) * CHUNK)),
                   pl.ds((st // n_tiles) * blk_rows + sub * CHUNK, CHUNK),
                   slice((st % n_tiles) * MXU_DIM, (st % n_tiles + 1) * MXU_DIM)) for st in range(n_states)]
        qrm = {c: _dot(qr_ref[c], state_bf16(srw_ref, st)) for c, st, _, _, _, _ in chains}
        u = {c: _dot(tinv_ref[c].astype(BF16),
                     block_diag((qrm[c][0:CHUNK] + av_ref[c, 0:CHUNK, :]).astype(BF16)))
             for c, _, _, _, _, _ in chains}
        for c, st, b, part, rows, sl in chains:
            ub = u[c].astype(BF16)
            yrw_ref[rows, sl] = qrm[c][CHUNK:] + av_ref[c, CHUNK:, :] + _dot(brp_ref[c], block_diag(ub))
            dec = dec_ref[c]
            upd = _dot(pkt_ref[c], jnp.concatenate([ub, vr_ref[part][:, sl]], axis=0))
            yret_ref[rows, sl] = inner_ref[c] + _dot(qd_ref[c], state_bf16(sret_ref, st))
            for rs, ls in diag_tiles:
                srw_ref[st, rs, ls] = (srw_ref[st, rs, ls] * dec[rs, :]
                                       + jnp.where(same_head[rs, ls], upd[rs, ls], 0.0))
                sret_ref[st, rs, ls] = (sret_ref[st, rs, ls] * cdec_ref[st % n_tiles, rs, ls]
                                        + kv_ref[c, rs, ls])


def _mix_call(streams, dmask, qdec, kdec, cdec, rgn_g, rgn_b, wgn_g, wgn_b, rk):
    q, k, v, g, r, km, vr, lwh, lwl, a, kku, gate = streams
    bsz, s, width = q.shape
    n_tiles = width // MXU_DIM
    blk_rows = min(CHUNKS_PER_STEP * CHUNK, s)
    n_steps = s // blk_rows
    cur = pl.BlockSpec((bsz, blk_rows, width), lambda j: (0, jnp.minimum(j, n_steps - 1), 0))
    prev_idx = lambda j: (0, jnp.maximum(j - 1, 0), 0)
    prev = pl.BlockSpec((bsz, blk_rows, width), prev_idx)
    current_streams = [q, k, v, r, km, vr, lwh, lwl, a, kku]
    previous_streams = [g, gate, r, km, vr]
    consts = [dmask, qdec, kdec, cdec, rgn_g, rgn_b, wgn_g, wgn_b, rk]
    n_states = bsz * n_tiles
    n_chains = (blk_rows // CHUNK) * n_states
    tc = bsz * blk_rows
    state = (n_states, MXU_DIM, MXU_DIM)
    tile = (n_chains, MXU_DIM, MXU_DIM)
    flat = (n_chains, CHUNK, MXU_DIM)
    pair = (n_chains, 2 * CHUNK, MXU_DIM)
    return pl.pallas_call(
        _mix_kernel,
        grid=(n_steps + 1,),
        in_specs=([cur] * len(current_streams) + [prev] * len(previous_streams)
                  + [_const_spec(c.shape) for c in consts]),
        out_specs=pl.BlockSpec((bsz, blk_rows, 2 * width), prev_idx),
        out_shape=jax.ShapeDtypeStruct((bsz, s, 2 * width), BF16),
        scratch_shapes=[pltpu.VMEM(state, F32),
                        pltpu.VMEM(state, F32),
                        pltpu.VMEM((tc, width), F32), pltpu.VMEM((tc, width), F32),
                        pltpu.VMEM((tc, width), F32), pltpu.VMEM((tc, width), F32),
                        pltpu.VMEM(flat, BF16), pltpu.VMEM(flat, F32), pltpu.VMEM(flat, BF16),
                        pltpu.VMEM(pair, F32),
                        pltpu.VMEM((n_chains, MXU_DIM, 2 * CHUNK), BF16),
                        pltpu.VMEM((n_chains, MXU_DIM, LANES), F32),
                        pltpu.VMEM(pair, BF16),
                        pltpu.VMEM(flat, F32), pltpu.VMEM(flat, BF16), pltpu.VMEM(tile, F32)],
        compiler_params=pltpu.CompilerParams(dimension_semantics=("arbitrary",),
                                             vmem_limit_bytes=VMEM_LIMIT_BYTES),
        name="mix_recurrences",
    )(*current_streams, *previous_streams, *consts)


def _retention_tables(n_heads):
    h = jnp.arange(n_heads, dtype=F32)
    log_gamma = jnp.log1p(-jnp.exp2(-5.0 - h))
    lg_lane = jnp.repeat(log_gamma, HEAD_DIM)[None, :]
    idx = jnp.arange(CHUNK, dtype=F32)[:, None]
    qdec = jnp.exp((idx + 1.0) * lg_lane)
    kdec = jnp.exp((CHUNK - 1.0 - idx) * lg_lane)
    n_tiles = n_heads // HEADS_PER_TILE
    cdec = jnp.broadcast_to(jnp.exp(CHUNK * lg_lane).reshape(n_tiles, MXU_DIM, 1),
                            (n_tiles, MXU_DIM, MXU_DIM))
    rel = idx - (jnp.arange(MXU_DIM) % HEAD_DIM).astype(F32)[None, :]
    lg_tiles = lg_lane.reshape(n_tiles, 1, MXU_DIM)
    dmask = jnp.where((rel >= 0)[None], jnp.exp(jnp.where(rel >= 0, rel, 0.0)[None] * lg_tiles), 0.0)
    return dmask, qdec, kdec, cdec


def _rotary_tables(s):
    pos = jnp.arange(s, dtype=F32)
    inv_freq = ROPE_BASE ** (-jnp.arange(0, HEAD_DIM, 2, dtype=F32) / HEAD_DIM)
    ang = pos[:, None] * inv_freq[None, :]
    cos, sin = jnp.cos(ang), jnp.sin(ang)
    reps = LANES // HEAD_DIM
    cos_t = jnp.tile(jnp.concatenate([cos, cos], axis=1), (1, reps))
    sin_t = jnp.tile(jnp.concatenate([-sin, sin], axis=1), (1, reps))
    return cos_t, sin_t


def kernel(x, p, ffn1_w_gu, ffn1_w_down, ln1_g, ln1_b, w_in, ret_gn_g, ret_gn_b, rw_mu, rw_w0, rw_w_up, rw_a0, rw_a_up, rw_g_up, rw_k_k, rw_k_a, rw_r_k, rw_gn_g, rw_gn_b, w_out, ln2_g, ln2_b, ffn2_w_gu, ffn2_w_down, ln3_g, ln3_b, ple_w_proj, ple_w_gate, ple_b_gate):
    bsz, s, d = x.shape
    depth = ffn1_w_gu.shape[0]
    alpha = (2.0 * depth) ** 0.25
    width = rw_w0.shape[1]
    n_heads = width // HEAD_DIM
    ret_cols = 4 * width
    lora = DECAY_LORA + AAA_LORA + GATE_LORA
    lora_pad = -(-lora // LANES) * LANES

    dmask, qdec, kdec, cdec = _retention_tables(n_heads)
    cos_t, sin_t = _rotary_tables(s)
    row = lambda v: v.reshape(1, -1)

    h = x.reshape(bsz * s, d)
    for i in range(depth):
        h = _ffn_call(h, ffn1_w_gu[i], ffn1_w_down[i],
                      row(ln1_g[i]), row(ln1_b[i]), alpha)

        wlo = jnp.pad(w_in[i][:, ret_cols + 3 * width:].astype(BF16), ((0, 0), (0, lora_pad - lora)))
        mu = row(jnp.pad(rw_mu[i], (0, lora_pad - lora)))
        n_da = DECAY_LORA + AAA_LORA
        wda = jnp.zeros((n_da, 2 * width), BF16)
        wda = wda.at[:DECAY_LORA, :width].set(rw_w_up[i].astype(BF16))
        wda = wda.at[DECAY_LORA:, width:].set(rw_a_up[i].astype(BF16))
        wg = jnp.pad(rw_g_up[i].astype(BF16), ((0, lora_pad - lora), (0, 0)))
        streams = _inproj_call(h.reshape(bsz, s, d), w_in[i], wlo, wda, wg, cos_t, sin_t, mu,
                               row(rw_w0[i]), row(rw_a0[i]), row(rw_k_k[i]), row(rw_k_a[i]))
        mixed = _mix_call(streams, dmask, qdec, kdec, cdec, row(ret_gn_g[i]), row(ret_gn_b[i]),
                          row(rw_gn_g[i]), row(rw_gn_b[i]), row(rw_r_k[i]))
        pre = (mixed.reshape(bsz * s, 2 * width), w_out[i].astype(BF16), row(ln2_g[i]), row(ln2_b[i]))
        ple = (p[i].reshape(bsz * s, -1), ple_w_proj[i].astype(BF16), ple_w_gate[i].astype(BF16),
               row(ple_b_gate[i]))
        h = _ffn_call(h, ffn2_w_gu[i], ffn2_w_down[i].astype(BF16),
                      row(ln3_g[i]), row(ln3_b[i]), alpha, pre=pre, ple=ple)
    return h.reshape(bsz, s, d)
```

```python
import functools
import math

import jax
import jax.numpy as jnp
from jax import lax
from jax.experimental import pallas as pl
from jax.experimental.pallas import tpu as pltpu

D_PLE = 256
HEAD_DIM = 64
ROPE_BASE = 10000.0
DECAY_LORA = 64
AAA_LORA = 64
GATE_LORA = 160
LN_EPS = 1e-5
RET_GN_EPS = 1e-5
RWKV_GN_EPS = 64e-5
DECAY_SCALE = math.exp(-0.5)
HEAD_SHIFT = HEAD_DIM.bit_length() - 1

LANES = 128
MXU_DIM = 256
HEADS_PER_TILE = MXU_DIM // HEAD_DIM
CHUNK = 64
CHUNKS_PER_STEP = 2
VMEM_LIMIT_BYTES = 56 * 1024 * 1024

F32 = jnp.float32
BF16 = jnp.bfloat16


def _dot(a, b):
    return jnp.dot(a, b, preferred_element_type=F32)


def _dot_nt(a, b):
    return lax.dot_general(a, b, (((1,), (1,)), ((), ())), preferred_element_type=F32)


def _dot_tn(a, b):
    return lax.dot_general(a, b, (((0,), (0,)), ((), ())), preferred_element_type=F32)


def _layer_norm(y, g, b):
    mu = jnp.mean(y, axis=-1, keepdims=True)
    yc = y - mu
    var = jnp.mean(yc * yc, axis=-1, keepdims=True)
    return yc * lax.rsqrt(var + LN_EPS) * g + b


def _const_spec(shape):
    nd = len(shape)
    return pl.BlockSpec(shape, lambda *_: (0,) * nd, pipeline_mode=pl.Buffered(1))


def _ffn_kernel(*refs, alpha, fc, with_pre, with_ple):
    refs = list(refs)
    x_ref = refs.pop(0)
    if with_pre:
        m_ref, wo_ref, gp_ref, bp_ref = refs[:4]
        refs = refs[4:]
    wgu_ref, wd_ref, g_ref, b_ref = refs[:4]
    refs = refs[4:]
    if with_ple:
        p_ref, wp_ref, wpg_ref, bpg_ref = refs[:4]
        refs = refs[4:]
    o_ref, act_ref = refs[:2]
    d_ff = wd_ref.shape[0]
    if with_pre:
        xs_ref = refs[2]
        xs_ref[...] = _layer_norm(alpha * x_ref[...] + _dot(m_ref[...], wo_ref[...]), gp_ref[...], bp_ref[...])
    else:
        xs_ref = x_ref
    xb = xs_ref[...].astype(BF16)
    for c in range(d_ff // fc):
        hg = _dot(xb, wgu_ref[:, c * fc:(c + 1) * fc].astype(BF16))
        hu = _dot(xb, wgu_ref[:, d_ff + c * fc:d_ff + (c + 1) * fc].astype(BF16))
        act_ref[:, c * fc:(c + 1) * fc] = (hg * jax.nn.sigmoid(hg) * hu).astype(BF16)
    down = _dot(act_ref[...], wd_ref[...].astype(BF16))
    y = _layer_norm(alpha * xs_ref[...] + 0.5 * down, g_ref[...], b_ref[...])
    if with_ple:
        gate = jax.nn.sigmoid(_dot(y.astype(BF16), wpg_ref[...]) + bpg_ref[...])
        y = y + gate * _dot(p_ref[...].astype(BF16), wp_ref[...])
    o_ref[...] = y


def _ffn_call(x2d, wgu, wd, ln_g, ln_b, alpha, pre=None, ple=None, tm=512, fc=MXU_DIM):
    n, d = x2d.shape
    d_ff = wd.shape[0]
    tm = min(tm, n)
    row = lambda i: (i, 0)
    in_specs = [pl.BlockSpec((tm, d), row)]
    args = [x2d]
    if pre is not None:
        m2d, wo, gp, bp = pre
        in_specs += [pl.BlockSpec((tm, m2d.shape[1]), row), _const_spec(wo.shape),
                     _const_spec((1, d)), _const_spec((1, d))]
        args += [m2d, wo, gp, bp]
    in_specs += [_const_spec(wgu.shape), _const_spec(wd.shape), _const_spec((1, d)), _const_spec((1, d))]
    args += [wgu, wd, ln_g, ln_b]
    if ple is not None:
        p2d, wp, wpg, bpg = ple
        in_specs += [pl.BlockSpec((tm, p2d.shape[1]), row), _const_spec(wp.shape),
                     _const_spec(wpg.shape), _const_spec((1, d))]
        args += [p2d, wp, wpg, bpg]
    return pl.pallas_call(
        functools.partial(_ffn_kernel, alpha=alpha, fc=fc, with_pre=pre is not None,
                          with_ple=ple is not None),
        grid=(n // tm,),
        in_specs=in_specs,
        out_specs=pl.BlockSpec((tm, d), row),
        out_shape=jax.ShapeDtypeStruct((n, d), F32),
        scratch_shapes=[pltpu.VMEM((tm, d_ff), BF16)] + ([pltpu.VMEM((tm, d), F32)] if pre is not None else []),
        compiler_params=pltpu.CompilerParams(dimension_semantics=("arbitrary",),
                                             vmem_limit_bytes=VMEM_LIMIT_BYTES),
        name="ffn_ln",
    )(*args)


def _group_sum(x, ones_blk):
    outs = [_dot(x[:, t * MXU_DIM:(t + 1) * MXU_DIM].astype(BF16), ones_blk)
            for t in range(x.shape[1] // MXU_DIM)]
    return jnp.concatenate(outs, axis=1)


def _inproj_kernel(x_ref, win_ref, wda_ref, wg_ref, cos_ref, sin_ref, mu_ref, w0_ref, a0_ref,
                   kk_ref, ka_ref,
                   q_ref, k_ref, v_ref, g_ref, r_ref, km_ref, vr_ref, lwh_ref, lwl_ref, a_ref, kku_ref,
                   gate_ref, carry_ref, *, width):
    tm = x_ref.shape[0]

    @pl.when(pl.program_id(1) == 0)
    def _():
        carry_ref[...] = jnp.zeros_like(carry_ref)

    xb = x_ref[...].astype(BF16)

    lane = lax.broadcasted_iota(jnp.int32, (tm, width), 1)
    first_half = (lane & (HEAD_DIM // 2)) == 0
    reps = width // cos_ref.shape[1]
    cos = jnp.concatenate([cos_ref[...]] * reps, axis=1)
    sin = jnp.concatenate([sin_ref[...]] * reps, axis=1)

    def rotary(t):
        swapped = jnp.where(first_half, pltpu.roll(t, width - HEAD_DIM // 2, 1),
                            pltpu.roll(t, HEAD_DIM // 2, 1))
        return t * cos + swapped * sin

    def put(ref, val):
        ref[...] = val.astype(ref.dtype)

    ret_cols = 4 * width
    n_lora = win_ref.shape[0] - ret_cols - 3 * width
    lora_pad = mu_ref.shape[1] - 3 * width

    def project(lo_col, hi_col, pad=0):
        w = win_ref[lo_col:hi_col, :].astype(BF16)
        if pad:
            w = jnp.concatenate([w, jnp.zeros((pad, w.shape[1]), BF16)], axis=0)
        return _dot_nt(xb, w)

    def shifted(lo_col, hi_col, z):
        first = carry_ref[0:1, lo_col:hi_col]
        carry_ref[0:1, lo_col:hi_col] = z[tm - 1:tm, :]
        rowid = lax.broadcasted_iota(jnp.int32, z.shape, 0)
        prev = jnp.where(rowid == 0, first, pltpu.roll(z, 1, 0))
        return z + (prev - z) * mu_ref[:, lo_col:hi_col]

    lo = shifted(3 * width, 3 * width + lora_pad,
                 project(ret_cols + 3 * width, ret_cols + 3 * width + n_lora, pad=lora_pad - n_lora))
    n_da = DECAY_LORA + AAA_LORA
    lo_da = lo[:, 0:n_da]
    ll = lax.broadcasted_iota(jnp.int32, lo_da.shape, 1)
    act_da = jnp.where(ll < DECAY_LORA, jnp.tanh(lo_da), lo_da).astype(BF16)
    up = _dot(act_da, wda_ref[...])
    put(gate_ref, _dot(jax.nn.sigmoid(lo[:, n_da:]).astype(BF16), wg_ref[...]))
    lw = -DECAY_SCALE * jax.nn.sigmoid(w0_ref[...] + up[:, 0 * width:1 * width])
    lw_hi, lw_lo = _split_hi_lo(lw)
    lwh_ref[...] = lw_hi
    lwl_ref[...] = lw_lo
    a = jax.nn.sigmoid(a0_ref[...] + up[:, 1 * width:2 * width])
    put(a_ref, a)
    kr = shifted(1 * width, 2 * width, project(ret_cols + 1 * width, ret_cols + 2 * width))
    put(kku_ref, kr * kk_ref[...])
    put(km_ref, kr * (1.0 + (a - 1.0) * ka_ref[...]))
    put(r_ref, shifted(0 * width, 1 * width, project(ret_cols + 0 * width, ret_cols + 1 * width)))
    put(vr_ref, shifted(2 * width, 3 * width, project(ret_cols + 2 * width, ret_cols + 3 * width)))

    put(q_ref, rotary(project(0 * width, 1 * width)))
    put(k_ref, rotary(project(1 * width, 2 * width)) * (HEAD_DIM ** -0.5))
    put(v_ref, project(2 * width, 3 * width))
    put(g_ref, project(3 * width, 4 * width))


def _inproj_call(x3d, w_in_t, wda, wg, cos, sin, mu, w0, a0, k_k, k_a, tm=512):
    bsz, s, d = x3d.shape
    width = w0.shape[1]
    tm = min(tm, s)
    blk = lambda b, j: (b, j, 0)
    out_spec = pl.BlockSpec((None, tm, width), blk)
    n_out = 12
    out_sds = [jax.ShapeDtypeStruct((bsz, s, width), BF16)] * n_out
    return pl.pallas_call(
        functools.partial(_inproj_kernel, width=width),
        grid=(bsz, s // tm),
        in_specs=[pl.BlockSpec((None, tm, d), blk), _const_spec(w_in_t.shape),
                  _const_spec(wda.shape), _const_spec(wg.shape),
                  pl.BlockSpec((tm, cos.shape[1]), lambda b, j: (j, 0)),
                  pl.BlockSpec((tm, sin.shape[1]), lambda b, j: (j, 0)),
                  _const_spec(mu.shape), _const_spec(w0.shape), _const_spec(a0.shape),
                  _const_spec(k_k.shape), _const_spec(k_a.shape)],
        out_specs=[out_spec] * n_out,
        out_shape=out_sds,
        scratch_shapes=[pltpu.VMEM((8, mu.shape[1]), F32)],
        compiler_params=pltpu.CompilerParams(dimension_semantics=("arbitrary", "arbitrary"),
                                             vmem_limit_bytes=VMEM_LIMIT_BYTES),
        name="inproj_prep",
    )(x3d, w_in_t, wda, wg, cos, sin, mu, w0, a0, k_k, k_a)


def _split_hi_lo(x):
    hi = x.astype(BF16)
    lo = (x - hi.astype(F32)).astype(BF16)
    return hi, lo


def _mix_kernel(q_ref, k_ref, v_ref, r_ref, km_ref, vr_ref, lwh_ref, lwl_ref, a_ref, kku_ref,
                g_ref, gate_ref, rl_ref, kml_ref, vrl_ref,
                dmask_ref, qdec_ref, kdec_ref, cdec_ref, rgn_g_ref, rgn_b_ref, wgn_g_ref, wgn_b_ref,
                rk_ref, o_ref,
                sret_ref, srw_ref, kkn_ref, p_ref, yret_ref, yrw_ref,
                pw_ref, tinv_ref, brp_ref, av_ref, pkt_ref, dec_ref, qr_ref, inner_ref, qd_ref, kv_ref):
    bsz, blk_rows, width = q_ref.shape
    n_sub = blk_rows // CHUNK
    n_tiles = width // MXU_DIM
    n_states = bsz * n_tiles
    n_chains = n_sub * n_states
    hpt = HEADS_PER_TILE
    n_rows = bsz * blk_rows

    def full(ref):
        return ref[...].reshape(n_rows, ref.shape[-1]).astype(F32)

    @pl.when(pl.program_id(0) == 0)
    def _():
        sret_ref[...] = jnp.zeros_like(sret_ref)
        srw_ref[...] = jnp.zeros_like(srw_ref)
        yret_ref[...] = jnp.zeros_like(yret_ref)
        yrw_ref[...] = jnp.zeros_like(yrw_ref)

    ri = lax.broadcasted_iota(jnp.int32, (MXU_DIM, MXU_DIM), 0)
    ci = lax.broadcasted_iota(jnp.int32, (MXU_DIM, MXU_DIM), 1)
    same_head = (ri >> HEAD_SHIFT) == (ci >> HEAD_SHIFT)
    ones_blk = same_head.astype(BF16)
    cr = lax.broadcasted_iota(jnp.int32, (CHUNK, MXU_DIM), 0)
    cj = lax.broadcasted_iota(jnp.int32, (CHUNK, MXU_DIM), 1) & (HEAD_DIM - 1)
    strict_lower, incl_lower = cr > cj, cr >= cj
    eye_c = (cr == cj).astype(F32)
    ti = lax.broadcasted_iota(jnp.int32, (2 * CHUNK, CHUNK), 0)
    tj = lax.broadcasted_iota(jnp.int32, (2 * CHUNK, CHUNK), 1)
    tri = (((ti < CHUNK) & (ti >= tj)) | ((ti >= CHUNK) & (ti - CHUNK > tj))).astype(BF16)

    heads_per_lane_tile = LANES // HEAD_DIM
    diag_tiles = [(slice(h * HEAD_DIM, (h + 1) * HEAD_DIM),
                   slice((h // heads_per_lane_tile) * LANES, (h // heads_per_lane_tile + 1) * LANES))
                  for h in range(hpt)]

    def state_bf16(ref, c):
        zeros = jnp.zeros((HEAD_DIM, LANES), BF16)
        rows = []
        for rs, ls in diag_tiles:
            parts = [zeros] * (MXU_DIM // LANES)
            parts[ls.start // LANES] = ref[c, rs, ls].astype(BF16)
            rows.append(jnp.concatenate(parts, axis=1))
        return jnp.concatenate(rows, axis=0)

    def block_diag(x):
        return jnp.where(same_head, jnp.concatenate([x] * hpt, axis=0), jnp.zeros((), x.dtype))

    def group_sum(x):
        return _group_sum(x, ones_blk)

    kku = full(kku_ref)
    kkn = kku * lax.rsqrt(jnp.maximum(group_sum(kku * kku), 1e-24))
    kkn_ref[...] = kkn
    p_ref[...] = -(kkn * full(a_ref))

    def prepare(sub, b):
        rows = pl.ds(b * blk_rows + sub * CHUNK, CHUNK)
        part = (b, slice(sub * CHUNK, (sub + 1) * CHUNK))
        cums = _dot(tri, lwh_ref[part]) + _dot(tri, lwl_ref[part])
        cum = cums[0:CHUNK]
        e_in = jnp.exp(cum)
        e_ex = jnp.exp(cums[CHUNK:])
        e_neg = jnp.exp(-cum)
        e_last = e_in[CHUNK - 1:CHUNK, :]
        qt_all = kkn_ref[rows, :] * e_ex
        rt_all = r_ref[part].astype(F32) * e_in
        pt_all = p_ref[rows, :] * e_neg
        kt_all = km_ref[part].astype(F32) * e_neg
        ph_all = pt_all * e_last
        kh_all = kt_all * e_last
        vr_all = vr_ref[part].astype(BF16)
        q_all = q_ref[part]
        k_all = k_ref[part]
        v_all = v_ref[part].astype(BF16)
        qd_all = (q_all.astype(F32) * qdec_ref[...]).astype(BF16)
        kd_all = (k_all.astype(F32) * kdec_ref[...]).astype(BF16)
        qt_b, rt_b = qt_all.astype(BF16), rt_all.astype(BF16)
        pt_b, kt_b = pt_all.astype(BF16), kt_all.astype(BF16)
        q_b, k_b = q_all.astype(BF16), k_all.astype(BF16)

        for t in range(n_tiles):
            c = sub * n_states + b * n_tiles + t
            sl = slice(t * MXU_DIM, (t + 1) * MXU_DIM)
            qr = jnp.concatenate([qt_b[:, sl], rt_b[:, sl]], axis=0)
            qr_ref[c] = qr
            g_p = _dot_nt(qr, block_diag(pt_b[:, sl]))
            g_k = _dot_nt(qr, block_diag(kt_b[:, sl]))
            a_qp = jnp.where(strict_lower, g_p[0:CHUNK], 0.0)
            pw_ref[c] = a_qp.astype(BF16)
            tinv_ref[c] = a_qp + eye_c
            brp_ref[c] = jnp.where(incl_lower, g_p[CHUNK:], 0.0).astype(BF16)
            ab = jnp.concatenate([jnp.where(strict_lower, g_k[0:CHUNK], 0.0),
                                  jnp.where(incl_lower, g_k[CHUNK:], 0.0)], axis=0).astype(BF16)
            av_ref[c] = _dot(ab, block_diag(vr_all[:, sl]))
            pkt_ref[c] = jnp.transpose(jnp.concatenate([ph_all[:, sl], kh_all[:, sl]], axis=0)).astype(BF16)
            dec_ref[c] = jnp.transpose(jnp.broadcast_to(e_last[:, sl], (LANES, MXU_DIM)))
            scores = (_dot_nt(q_b[:, sl], block_diag(k_b[:, sl])) * dmask_ref[t]).astype(BF16)
            inner_ref[c] = _dot(scores, block_diag(v_all[:, sl]))
            qd_ref[c] = qd_all[:, sl]
            kv = _dot_tn(kd_all[:, sl], v_all[:, sl])
            for rs, ls in diag_tiles:
                kv_ref[c, rs, ls] = jnp.where(same_head[rs, ls], kv[rs, ls], 0.0)

    for sub in range(n_sub):
        for b in range(bsz):
            prepare(sub, b)

    inv_hd = 1.0 / HEAD_DIM

    def group_norm(y, eps):
        mu = group_sum(y) * inv_hd
        yc = y - mu
        var = group_sum(yc * yc) * inv_hd
        return yc * lax.rsqrt(var + eps)

    def norm_retention():
        yret_ref[...] = group_norm(yret_ref[...], RET_GN_EPS) * rgn_g_ref[...] + rgn_b_ref[...]

    def norm_rwkv():
        bonus = group_sum(full(rl_ref) * full(kml_ref) * rk_ref[...]) * full(vrl_ref)
        yrw_ref[...] = group_norm(yrw_ref[...], RWKV_GN_EPS) * wgn_g_ref[...] + wgn_b_ref[...] + bonus

    def gate_and_store(b_lo, b_hi):
        for b in range(b_lo, b_hi):
            rows = pl.ds(b * blk_rows, blk_rows)
            g = g_ref[b].astype(F32)
            o_ref[b, :, 0:width] = (g * jax.nn.sigmoid(g) * yret_ref[rows, :]).astype(o_ref.dtype)
            o_ref[b, :, width:2 * width] = (yrw_ref[rows, :] * gate_ref[b].astype(F32)).astype(o_ref.dtype)

    filler = [norm_retention, norm_rwkv, functools.partial(gate_and_store, 0, bsz // 2),
              functools.partial(gate_and_store, bsz // 2, bsz)]

    for c in range(n_chains):
        pwb = pw_ref[c]
        pw_ref[c] = _dot(pwb, block_diag(pwb)).astype(BF16)
    for rnd in range(4):
        for c in range(n_chains):
            pwb, t_acc = pw_ref[c], tinv_ref[c]
            res = _dot(jnp.concatenate([pwb, t_acc.astype(BF16)], axis=0), block_diag(pwb))
            pw_ref[c] = res[0:CHUNK].astype(BF16)
            tinv_ref[c] = t_acc + res[CHUNK:]
        filler[rnd]()
    for c in range(n_chains):
        t_acc = tinv_ref[c]
        tinv_ref[c] = t_acc + _dot(t_acc.astype(BF16), block_diag(pw_ref[c]))

    for sub in range(n_sub):
        chains = [(sub * n_states + st, st, st // n_tiles, (st // n_tiles, slice(sub * CHUNK, (sub + 1) * CHUNK)),
                   pl.ds((st // n_tiles) * blk_rows + sub * CHUNK, CHUNK),
                   slice((st % n_tiles) * MXU_DIM, (st % n_tiles + 1) * MXU_DIM)) for st in range(n_states)]
        qrm = {c: _dot(qr_ref[c], state_bf16(srw_ref, st)) for c, st, _, _, _, _ in chains}
        u = {c: _dot(tinv_ref[c].astype(BF16),
                     block_diag((qrm[c][0:CHUNK] + av_ref[c, 0:CHUNK, :]).astype(BF16)))
             for c, _, _, _, _, _ in chains}
        for c, st, b, part, rows, sl in chains:
            ub = u[c].astype(BF16)
            yrw_ref[rows, sl] = qrm[c][CHUNK:] + av_ref[c, CHUNK:, :] + _dot(brp_ref[c], block_diag(ub))
            dec = dec_ref[c]
            upd = _dot(pkt_ref[c], jnp.concatenate([ub, vr_ref[part][:, sl]], axis=0))
            yret_ref[rows, sl] = inner_ref[c] + _dot(qd_ref[c], state_bf16(sret_ref, st))
            for rs, ls in diag_tiles:
                srw_ref[st, rs, ls] = (srw_ref[st, rs, ls] * dec[rs, :]
                                       + jnp.where(same_head[rs, ls], upd[rs, ls], 0.0))
                sret_ref[st, rs, ls] = (sret_ref[st, rs, ls] * cdec_ref[st % n_tiles, rs, ls]
                                        + kv_ref[c, rs, ls])


def _mix_call(streams, dmask, qdec, kdec, cdec, rgn_g, rgn_b, wgn_g, wgn_b, rk):
    q, k, v, g, r, km, vr, lwh, lwl, a, kku, gate = streams
    bsz, s, width = q.shape
    n_tiles = width // MXU_DIM
    blk_rows = min(CHUNKS_PER_STEP * CHUNK, s)
    n_steps = s // blk_rows
    cur = pl.BlockSpec((bsz, blk_rows, width), lambda j: (0, jnp.minimum(j, n_steps - 1), 0))
    prev_idx = lambda j: (0, jnp.maximum(j - 1, 0), 0)
    prev = pl.BlockSpec((bsz, blk_rows, width), prev_idx)
    current_streams = [q, k, v, r, km, vr, lwh, lwl, a, kku]
    previous_streams = [g, gate, r, km, vr]
    consts = [dmask, qdec, kdec, cdec, rgn_g, rgn_b, wgn_g, wgn_b, rk]
    n_states = bsz * n_tiles
    n_chains = (blk_rows // CHUNK) * n_states
    tc = bsz * blk_rows
    state = (n_states, MXU_DIM, MXU_DIM)
    tile = (n_chains, MXU_DIM, MXU_DIM)
    flat = (n_chains, CHUNK, MXU_DIM)
    pair = (n_chains, 2 * CHUNK, MXU_DIM)
    return pl.pallas_call(
        _mix_kernel,
        grid=(n_steps + 1,),
        in_specs=([cur] * len(current_streams) + [prev] * len(previous_streams)
                  + [_const_spec(c.shape) for c in consts]),
        out_specs=pl.BlockSpec((bsz, blk_rows, 2 * width), prev_idx),
        out_shape=jax.ShapeDtypeStruct((bsz, s, 2 * width), BF16),
        scratch_shapes=[pltpu.VMEM(state, F32),
                        pltpu.VMEM(state, F32),
                        pltpu.VMEM((tc, width), F32), pltpu.VMEM((tc, width), F32),
                        pltpu.VMEM((tc, width), F32), pltpu.VMEM((tc, width), F32),
                        pltpu.VMEM(flat, BF16), pltpu.VMEM(flat, F32), pltpu.VMEM(flat, BF16),
                        pltpu.VMEM(pair, F32),
                        pltpu.VMEM((n_chains, MXU_DIM, 2 * CHUNK), BF16),
                        pltpu.VMEM((n_chains, MXU_DIM, LANES), F32),
                        pltpu.VMEM(pair, BF16),
                        pltpu.VMEM(flat, F32), pltpu.VMEM(flat, BF16), pltpu.VMEM(tile, F32)],
        compiler_params=pltpu.CompilerParams(dimension_semantics=("arbitrary",),
                                             vmem_limit_bytes=VMEM_LIMIT_BYTES),
        name="mix_recurrences",
    )(*current_streams, *previous_streams, *consts)


def _retention_tables(n_heads):
    h = jnp.arange(n_heads, dtype=F32)
    log_gamma = jnp.log1p(-jnp.exp2(-5.0 - h))
    lg_lane = jnp.repeat(log_gamma, HEAD_DIM)[None, :]
    idx = jnp.arange(CHUNK, dtype=F32)[:, None]
    qdec = jnp.exp((idx + 1.0) * lg_lane)
    kdec = jnp.exp((CHUNK - 1.0 - idx) * lg_lane)
    n_tiles = n_heads // HEADS_PER_TILE
    cdec = jnp.broadcast_to(jnp.exp(CHUNK * lg_lane).reshape(n_tiles, MXU_DIM, 1),
                            (n_tiles, MXU_DIM, MXU_DIM))
    rel = idx - (jnp.arange(MXU_DIM) % HEAD_DIM).astype(F32)[None, :]
    lg_tiles = lg_lane.reshape(n_tiles, 1, MXU_DIM)
    dmask = jnp.where((rel >= 0)[None], jnp.exp(jnp.where(rel >= 0, rel, 0.0)[None] * lg_tiles), 0.0)
    return dmask, qdec, kdec, cdec


def _rotary_tables(s):
    pos = jnp.arange(s, dtype=F32)
    inv_freq = ROPE_BASE ** (-jnp.arange(0, HEAD_DIM, 2, dtype=F32) / HEAD_DIM)
    ang = pos[:, None] * inv_freq[None, :]
    cos, sin = jnp.cos(ang), jnp.sin(ang)
    reps = LANES // HEAD_DIM
    cos_t = jnp.tile(jnp.concatenate([cos, cos], axis=1), (1, reps))
    sin_t = jnp.tile(jnp.concatenate([-sin, sin], axis=1), (1, reps))
    return cos_t, sin_t


def kernel(x, p, ffn1_w_gu, ffn1_w_down, ln1_g, ln1_b, w_in, ret_gn_g, ret_gn_b, rw_mu, rw_w0, rw_w_up, rw_a0, rw_a_up, rw_g_up, rw_k_k, rw_k_a, rw_r_k, rw_gn_g, rw_gn_b, w_out, ln2_g, ln2_b, ffn2_w_gu, ffn2_w_down, ln3_g, ln3_b, ple_w_proj, ple_w_gate, ple_b_gate):
    bsz, s, d = x.shape
    depth = ffn1_w_gu.shape[0]
    alpha = (2.0 * depth) ** 0.25
    width = rw_w0.shape[1]
    n_heads = width // HEAD_DIM
    ret_cols = 4 * width
    lora = DECAY_LORA + AAA_LORA + GATE_LORA
    lora_pad = -(-lora // LANES) * LANES

    dmask, qdec, kdec, cdec = _retention_tables(n_heads)
    cos_t, sin_t = _rotary_tables(s)
    row = lambda v: v.reshape(1, -1)

    h = x.reshape(bsz * s, d)
    for i in range(depth):
        h = _ffn_call(h, ffn1_w_gu[i], ffn1_w_down[i],
                      row(ln1_g[i]), row(ln1_b[i]), alpha)

        mu = row(jnp.pad(rw_mu[i], (0, lora_pad - lora)))
        n_da = DECAY_LORA + AAA_LORA
        wda = jnp.zeros((n_da, 2 * width), BF16)
        wda = wda.at[:DECAY_LORA, :width].set(rw_w_up[i].astype(BF16))
        wda = wda.at[DECAY_LORA:, width:].set(rw_a_up[i].astype(BF16))
        wg = jnp.pad(rw_g_up[i].astype(BF16), ((0, lora_pad - lora), (0, 0)))
        streams = _inproj_call(h.reshape(bsz, s, d), jnp.swapaxes(w_in[i], 0, 1), wda, wg, cos_t, sin_t, mu,
                               row(rw_w0[i]), row(rw_a0[i]), row(rw_k_k[i]), row(rw_k_a[i]))
        mixed = _mix_call(streams, dmask, qdec, kdec, cdec, row(ret_gn_g[i]), row(ret_gn_b[i]),
                          row(rw_gn_g[i]), row(rw_gn_b[i]), row(rw_r_k[i]))
        pre = (mixed.reshape(bsz * s, 2 * width), w_out[i].astype(BF16), row(ln2_g[i]), row(ln2_b[i]))
        ple = (p[i].reshape(bsz * s, -1), ple_w_proj[i].astype(BF16), ple_w_gate[i].astype(BF16),
               row(ple_b_gate[i]))
        h = _ffn_call(h, ffn2_w_gu[i], ffn2_w_down[i].astype(BF16),
                      row(ln3_g[i]), row(ln3_b[i]), alpha, pre=pre, ple=ple)
    return h.reshape(bsz, s, d)
```

```python
import functools
import math

import jax
import jax.numpy as jnp
from jax import lax
from jax.experimental import pallas as pl
from jax.experimental.pallas import tpu as pltpu

D_PLE = 256
HEAD_DIM = 64
ROPE_BASE = 10000.0
DECAY_LORA = 64
AAA_LORA = 64
GATE_LORA = 160
LN_EPS = 1e-5
RET_GN_EPS = 1e-5
RWKV_GN_EPS = 64e-5
DECAY_SCALE = math.exp(-0.5)
HEAD_SHIFT = HEAD_DIM.bit_length() - 1

LANES = 128
MXU_DIM = 256
HEADS_PER_TILE = MXU_DIM // HEAD_DIM
CHUNK = 64
CHUNKS_PER_STEP = 2
VMEM_LIMIT_BYTES = 56 * 1024 * 1024

F32 = jnp.float32
BF16 = jnp.bfloat16


def _dot(a, b):
    return jnp.dot(a, b, preferred_element_type=F32)


def _dot_nt(a, b):
    return lax.dot_general(a, b, (((1,), (1,)), ((), ())), preferred_element_type=F32)


def _dot_tn(a, b):
    return lax.dot_general(a, b, (((0,), (0,)), ((), ())), preferred_element_type=F32)


def _layer_norm(y, g, b):
    mu = jnp.mean(y, axis=-1, keepdims=True)
    yc = y - mu
    var = jnp.mean(yc * yc, axis=-1, keepdims=True)
    return yc * lax.rsqrt(var + LN_EPS) * g + b


def _const_spec(shape):
    nd = len(shape)
    return pl.BlockSpec(shape, lambda *_: (0,) * nd, pipeline_mode=pl.Buffered(1))


def _ffn_kernel(*refs, alpha, fc, with_pre, with_ple):
    refs = list(refs)
    x_ref = refs.pop(0)
    if with_pre:
        m_ref, wo_ref, gp_ref, bp_ref = refs[:4]
        refs = refs[4:]
    wgu_ref, wd_ref, g_ref, b_ref = refs[:4]
    refs = refs[4:]
    if with_ple:
        p_ref, wp_ref, wpg_ref, bpg_ref = refs[:4]
        refs = refs[4:]
    o_ref, act_ref = refs[:2]
    d_ff = wd_ref.shape[0]
    if with_pre:
        xs_ref = refs[2]
        xs_ref[...] = _layer_norm(alpha * x_ref[...] + _dot(m_ref[...], wo_ref[...]), gp_ref[...], bp_ref[...])
    else:
        xs_ref = x_ref
    xb = xs_ref[...].astype(BF16)
    for c in range(d_ff // fc):
        hg = _dot(xb, wgu_ref[:, c * fc:(c + 1) * fc].astype(BF16))
        hu = _dot(xb, wgu_ref[:, d_ff + c * fc:d_ff + (c + 1) * fc].astype(BF16))
        act_ref[:, c * fc:(c + 1) * fc] = (hg * jax.nn.sigmoid(hg) * hu).astype(BF16)
    down = _dot(act_ref[...], wd_ref[...].astype(BF16))
    y = _layer_norm(alpha * xs_ref[...] + 0.5 * down, g_ref[...], b_ref[...])
    if with_ple:
        gate = jax.nn.sigmoid(_dot(y.astype(BF16), wpg_ref[...]) + bpg_ref[...])
        y = y + gate * _dot(p_ref[...].astype(BF16), wp_ref[...])
    o_ref[...] = y


def _ffn_call(x2d, wgu, wd, ln_g, ln_b, alpha, pre=None, ple=None, tm=512, fc=MXU_DIM):
    n, d = x2d.shape
    d_ff = wd.shape[0]
    tm = min(tm, n)
    row = lambda i: (i, 0)
    in_specs = [pl.BlockSpec((tm, d), row)]
    args = [x2d]
    if pre is not None:
        m2d, wo, gp, bp = pre
        in_specs += [pl.BlockSpec((tm, m2d.shape[1]), row), _const_spec(wo.shape),
                     _const_spec((1, d)), _const_spec((1, d))]
        args += [m2d, wo, gp, bp]
    in_specs += [_const_spec(wgu.shape), _const_spec(wd.shape), _const_spec((1, d)), _const_spec((1, d))]
    args += [wgu, wd, ln_g, ln_b]
    if ple is not None:
        p2d, wp, wpg, bpg = ple
        in_specs += [pl.BlockSpec((tm, p2d.shape[1]), row), _const_spec(wp.shape),
                     _const_spec(wpg.shape), _const_spec((1, d))]
        args += [p2d, wp, wpg, bpg]
    return pl.pallas_call(
        functools.partial(_ffn_kernel, alpha=alpha, fc=fc, with_pre=pre is not None,
                          with_ple=ple is not None),
        grid=(n // tm,),
        in_specs=in_specs,
        out_specs=pl.BlockSpec((tm, d), row),
        out_shape=jax.ShapeDtypeStruct((n, d), F32),
        scratch_shapes=[pltpu.VMEM((tm, d_ff), BF16)] + ([pltpu.VMEM((tm, d), F32)] if pre is not None else []),
        compiler_params=pltpu.CompilerParams(dimension_semantics=("arbitrary",),
                                             vmem_limit_bytes=VMEM_LIMIT_BYTES),
        name="ffn_ln",
    )(*args)


def _group_sum(x, ones_blk):
    outs = [_dot(x[:, t * MXU_DIM:(t + 1) * MXU_DIM].astype(BF16), ones_blk)
            for t in range(x.shape[1] // MXU_DIM)]
    return jnp.concatenate(outs, axis=1)


def _inproj_kernel(x_ref, win_ref, wda_ref, wg_ref, cos_ref, sin_ref, mu_ref, w0_ref, a0_ref,
                   kk_ref, ka_ref,
                   q_ref, k_ref, v_ref, g_ref, r_ref, km_ref, vr_ref, lwh_ref, lwl_ref, a_ref, kku_ref,
                   gate_ref, carry_ref, *, width):
    tm = x_ref.shape[0]

    @pl.when(pl.program_id(1) == 0)
    def _():
        carry_ref[...] = jnp.zeros_like(carry_ref)

    xb = x_ref[...].astype(BF16)

    lane = lax.broadcasted_iota(jnp.int32, (tm, width), 1)
    first_half = (lane & (HEAD_DIM // 2)) == 0
    reps = width // cos_ref.shape[1]
    cos = jnp.concatenate([cos_ref[...]] * reps, axis=1)
    sin = jnp.concatenate([sin_ref[...]] * reps, axis=1)

    def rotary(t):
        swapped = jnp.where(first_half, pltpu.roll(t, width - HEAD_DIM // 2, 1),
                            pltpu.roll(t, HEAD_DIM // 2, 1))
        return t * cos + swapped * sin

    def put(ref, val):
        ref[...] = val.astype(ref.dtype)

    ret_cols = 4 * width
    n_lora = win_ref.shape[0] - ret_cols - 3 * width
    lora_pad = mu_ref.shape[1] - 3 * width

    def project(lo_col, hi_col, pad=0):
        w = win_ref[lo_col:hi_col, :].astype(BF16)
        if pad:
            w = jnp.concatenate([w, jnp.zeros((pad, w.shape[1]), BF16)], axis=0)
        return _dot_nt(xb, w)

    def shifted(lo_col, hi_col, z):
        first = carry_ref[0:1, lo_col:hi_col]
        carry_ref[0:1, lo_col:hi_col] = z[tm - 1:tm, :]
        rowid = lax.broadcasted_iota(jnp.int32, z.shape, 0)
        prev = jnp.where(rowid == 0, first, pltpu.roll(z, 1, 0))
        return z + (prev - z) * mu_ref[:, lo_col:hi_col]

    lo = shifted(3 * width, 3 * width + lora_pad,
                 project(ret_cols + 3 * width, ret_cols + 3 * width + n_lora, pad=lora_pad - n_lora))
    n_da = DECAY_LORA + AAA_LORA
    lo_da = lo[:, 0:n_da]
    ll = lax.broadcasted_iota(jnp.int32, lo_da.shape, 1)
    act_da = jnp.where(ll < DECAY_LORA, jnp.tanh(lo_da), lo_da).astype(BF16)
    up = _dot(act_da, wda_ref[...])
    put(gate_ref, _dot(jax.nn.sigmoid(lo[:, n_da:]).astype(BF16), wg_ref[...]))
    lw = -DECAY_SCALE * jax.nn.sigmoid(w0_ref[...] + up[:, 0 * width:1 * width])
    lw_hi, lw_lo = _split_hi_lo(lw)
    lwh_ref[...] = lw_hi
    lwl_ref[...] = lw_lo
    a = jax.nn.sigmoid(a0_ref[...] + up[:, 1 * width:2 * width])
    put(a_ref, a)
    kr = shifted(1 * width, 2 * width, project(ret_cols + 1 * width, ret_cols + 2 * width))
    put(kku_ref, kr * kk_ref[...])
    put(km_ref, kr * (1.0 + (a - 1.0) * ka_ref[...]))
    put(r_ref, shifted(0 * width, 1 * width, project(ret_cols + 0 * width, ret_cols + 1 * width)))
    put(vr_ref, shifted(2 * width, 3 * width, project(ret_cols + 2 * width, ret_cols + 3 * width)))

    put(q_ref, rotary(project(0 * width, 1 * width)))
    put(k_ref, rotary(project(1 * width, 2 * width)) * (HEAD_DIM ** -0.5))
    put(v_ref, project(2 * width, 3 * width))
    put(g_ref, project(3 * width, 4 * width))


def _inproj_call(x3d, w_in_t, wda, wg, cos, sin, mu, w0, a0, k_k, k_a, tm=512):
    bsz, s, d = x3d.shape
    width = w0.shape[1]
    tm = min(tm, s)
    blk = lambda b, j: (b, j, 0)
    out_spec = pl.BlockSpec((None, tm, width), blk)
    n_out = 12
    out_sds = [jax.ShapeDtypeStruct((bsz, s, width), BF16)] * n_out
    return pl.pallas_call(
        functools.partial(_inproj_kernel, width=width),
        grid=(bsz, s // tm),
        in_specs=[pl.BlockSpec((None, tm, d), blk), _const_spec(w_in_t.shape),
                  _const_spec(wda.shape), _const_spec(wg.shape),
                  pl.BlockSpec((tm, cos.shape[1]), lambda b, j: (j, 0)),
                  pl.BlockSpec((tm, sin.shape[1]), lambda b, j: (j, 0)),
                  _const_spec(mu.shape), _const_spec(w0.shape), _const_spec(a0.shape),
                  _const_spec(k_k.shape), _const_spec(k_a.shape)],
        out_specs=[out_spec] * n_out,
        out_shape=out_sds,
        scratch_shapes=[pltpu.VMEM((8, mu.shape[1]), F32)],
        compiler_params=pltpu.CompilerParams(dimension_semantics=("arbitrary", "arbitrary"),
                                             vmem_limit_bytes=VMEM_LIMIT_BYTES),
        name="inproj_prep",
    )(x3d, w_in_t, wda, wg, cos, sin, mu, w0, a0, k_k, k_a)


def _split_hi_lo(x):
    hi = x.astype(BF16)
    lo = (x - hi.astype(F32)).astype(BF16)
    return hi, lo


def _mix_kernel(q_ref, k_ref, v_ref, r_ref, km_ref, vr_ref, lwh_ref, lwl_ref, a_ref, kku_ref,
                g_ref, gate_ref, rl_ref, kml_ref, vrl_ref,
                dmask_ref, qdec_ref, kdec_ref, cdec_ref, rgn_g_ref, rgn_b_ref, wgn_g_ref, wgn_b_ref,
                rk_ref, o_ref,
                sret_ref, srw_ref, kkn_ref, p_ref, yret_ref, yrw_ref,
                pw_ref, tinv_ref, brp_ref, av_ref, pkt_ref, dec_ref, qr_ref, inner_ref, qd_ref, kv_ref):
    bsz, blk_rows, width = q_ref.shape
    n_sub = blk_rows // CHUNK
    n_tiles = width // MXU_DIM
    n_states = bsz * n_tiles
    n_chains = n_sub * n_states
    hpt = HEADS_PER_TILE
    n_rows = bsz * blk_rows

    def full(ref):
        return ref[...].reshape(n_rows, ref.shape[-1]).astype(F32)

    @pl.when(pl.program_id(0) == 0)
    def _():
        sret_ref[...] = jnp.zeros_like(sret_ref)
        srw_ref[...] = jnp.zeros_like(srw_ref)
        yret_ref[...] = jnp.zeros_like(yret_ref)
        yrw_ref[...] = jnp.zeros_like(yrw_ref)

    ri = lax.broadcasted_iota(jnp.int32, (MXU_DIM, MXU_DIM), 0)
    ci = lax.broadcasted_iota(jnp.int32, (MXU_DIM, MXU_DIM), 1)
    same_head = (ri >> HEAD_SHIFT) == (ci >> HEAD_SHIFT)
    ones_blk = same_head.astype(BF16)
    cr = lax.broadcasted_iota(jnp.int32, (CHUNK, MXU_DIM), 0)
    cj = lax.broadcasted_iota(jnp.int32, (CHUNK, MXU_DIM), 1) & (HEAD_DIM - 1)
    strict_lower, incl_lower = cr > cj, cr >= cj
    eye_c = (cr == cj).astype(F32)
    ti = lax.broadcasted_iota(jnp.int32, (2 * CHUNK, CHUNK), 0)
    tj = lax.broadcasted_iota(jnp.int32, (2 * CHUNK, CHUNK), 1)
    tri = (((ti < CHUNK) & (ti >= tj)) | ((ti >= CHUNK) & (ti - CHUNK > tj))).astype(BF16)

    heads_per_lane_tile = LANES // HEAD_DIM
    diag_tiles = [(slice(h * HEAD_DIM, (h + 1) * HEAD_DIM),
                   slice((h // heads_per_lane_tile) * LANES, (h // heads_per_lane_tile + 1) * LANES))
                  for h in range(hpt)]

    def state_bf16(ref, c):
        zeros = jnp.zeros((HEAD_DIM, LANES), BF16)
        rows = []
        for rs, ls in diag_tiles:
            parts = [zeros] * (MXU_DIM // LANES)
            parts[ls.start // LANES] = ref[c, rs, ls].astype(BF16)
            rows.append(jnp.concatenate(parts, axis=1))
        return jnp.concatenate(rows, axis=0)

    def block_diag(x):
        return jnp.where(same_head, jnp.concatenate([x] * hpt, axis=0), jnp.zeros((), x.dtype))

    def group_sum(x):
        return _group_sum(x, ones_blk)

    kku = full(kku_ref)
    kkn = kku * lax.rsqrt(jnp.maximum(group_sum(kku * kku), 1e-24))
    kkn_ref[...] = kkn
    p_ref[...] = -(kkn * full(a_ref))

    def prepare(sub, b):
        rows = pl.ds(b * blk_rows + sub * CHUNK, CHUNK)
        part = (b, slice(sub * CHUNK, (sub + 1) * CHUNK))
        cums = _dot(tri, lwh_ref[part]) + _dot(tri, lwl_ref[part])
        cum = cums[0:CHUNK]
        e_in = jnp.exp(cum)
        e_ex = jnp.exp(cums[CHUNK:])
        e_neg = jnp.exp(-cum)
        e_last = e_in[CHUNK - 1:CHUNK, :]
        qt_all = kkn_ref[rows, :] * e_ex
        rt_all = r_ref[part].astype(F32) * e_in
        pt_all = p_ref[rows, :] * e_neg
        kt_all = km_ref[part].astype(F32) * e_neg
        ph_all = pt_all * e_last
        kh_all = kt_all * e_last
        vr_all = vr_ref[part].astype(BF16)
        q_all = q_ref[part]
        k_all = k_ref[part]
        v_all = v_ref[part].astype(BF16)
        qd_all = (q_all.astype(F32) * qdec_ref[...]).astype(BF16)
        kd_all = (k_all.astype(F32) * kdec_ref[...]).astype(BF16)
        qt_b, rt_b = qt_all.astype(BF16), rt_all.astype(BF16)
        pt_b, kt_b = pt_all.astype(BF16), kt_all.astype(BF16)
        q_b, k_b = q_all.astype(BF16), k_all.astype(BF16)

        for t in range(n_tiles):
            c = sub * n_states + b * n_tiles + t
            sl = slice(t * MXU_DIM, (t + 1) * MXU_DIM)
            qr = jnp.concatenate([qt_b[:, sl], rt_b[:, sl]], axis=0)
            qr_ref[c] = qr
            g_p = _dot_nt(qr, block_diag(pt_b[:, sl]))
            g_k = _dot_nt(qr, block_diag(kt_b[:, sl]))
            a_qp = jnp.where(strict_lower, g_p[0:CHUNK], 0.0)
            pw_ref[c] = a_qp.astype(BF16)
            tinv_ref[c] = a_qp + eye_c
            brp_ref[c] = jnp.where(incl_lower, g_p[CHUNK:], 0.0).astype(BF16)
            ab = jnp.concatenate([jnp.where(strict_lower, g_k[0:CHUNK], 0.0),
                                  jnp.where(incl_lower, g_k[CHUNK:], 0.0)], axis=0).astype(BF16)
            av_ref[c] = _dot(ab, block_diag(vr_all[:, sl]))
            pkt_ref[c] = jnp.transpose(jnp.concatenate([ph_all[:, sl], kh_all[:, sl]], axis=0)).astype(BF16)
            dec_ref[c] = jnp.transpose(jnp.broadcast_to(e_last[:, sl], (LANES, MXU_DIM)))
            scores = (_dot_nt(q_b[:, sl], block_diag(k_b[:, sl])) * dmask_ref[t]).astype(BF16)
            inner_ref[c] = _dot(scores, block_diag(v_all[:, sl]))
            qd_ref[c] = qd_all[:, sl]
            kv = _dot_tn(kd_all[:, sl], v_all[:, sl])
            for rs, ls in diag_tiles:
                kv_ref[c, rs, ls] = jnp.where(same_head[rs, ls], kv[rs, ls], 0.0)

    for sub in range(n_sub):
        for b in range(bsz):
            prepare(sub, b)

    inv_hd = 1.0 / HEAD_DIM

    def group_norm(y, eps):
        mu = group_sum(y) * inv_hd
        yc = y - mu
        var = group_sum(yc * yc) * inv_hd
        return yc * lax.rsqrt(var + eps)

    def norm_retention():
        yret_ref[...] = group_norm(yret_ref[...], RET_GN_EPS) * rgn_g_ref[...] + rgn_b_ref[...]

    def norm_rwkv():
        bonus = group_sum(full(rl_ref) * full(kml_ref) * rk_ref[...]) * full(vrl_ref)
        yrw_ref[...] = group_norm(yrw_ref[...], RWKV_GN_EPS) * wgn_g_ref[...] + wgn_b_ref[...] + bonus

    def gate_and_store(b_lo, b_hi):
        for b in range(b_lo, b_hi):
            rows = pl.ds(b * blk_rows, blk_rows)
            g = g_ref[b].astype(F32)
            o_ref[b, :, 0:width] = (g * jax.nn.sigmoid(g) * yret_ref[rows, :]).astype(o_ref.dtype)
            o_ref[b, :, width:2 * width] = (yrw_ref[rows, :] * gate_ref[b].astype(F32)).astype(o_ref.dtype)

    filler = [norm_retention, norm_rwkv, functools.partial(gate_and_store, 0, bsz // 2),
              functools.partial(gate_and_store, bsz // 2, bsz)]

    for c in range(n_chains):
        pwb = pw_ref[c]
        pw_ref[c] = _dot(pwb, block_diag(pwb)).astype(BF16)
    for rnd in range(4):
        for c in range(n_chains):
            pwb, t_acc = pw_ref[c], tinv_ref[c]
            res = _dot(jnp.concatenate([pwb, t_acc.astype(BF16)], axis=0), block_diag(pwb))
            pw_ref[c] = res[0:CHUNK].astype(BF16)
            tinv_ref[c] = t_acc + res[CHUNK:]
        filler[rnd]()
    for c in range(n_chains):
        t_acc = tinv_ref[c]
        tinv_ref[c] = t_acc + _dot(t_acc.astype(BF16), block_diag(pw_ref[c]))

    for sub in range(n_sub):
        chains = [(sub * n_states + st, st, st // n_tiles, (st // n_tiles, slice(sub * CHUNK, (sub + 1) * CHUNK)),
                   pl.ds((st // n_tiles) * blk_rows + sub * CHUNK, CHUNK),
                   slice((st % n_tiles) * MXU_DIM, (st % n_tiles + 1) * MXU_DIM)) for st in range(n_states)]
        qrm = {c: _dot(qr_ref[c], state_bf16(srw_ref, st)) for c, st, _, _, _, _ in chains}
        u = {c: _dot(tinv_ref[c].astype(BF16),
                     block_diag((qrm[c][0:CHUNK] + av_ref[c, 0:CHUNK, :]).astype(BF16)))
             for c, _, _, _, _, _ in chains}
        for c, st, b, part, rows, sl in chains:
            ub = u[c].astype(BF16)
            yrw_ref[rows, sl] = qrm[c][CHUNK:] + av_ref[c, CHUNK:, :] + _dot(brp_ref[c], block_diag(ub))
            dec = dec_ref[c]
            upd = _dot(pkt_ref[c], jnp.concatenate([ub, vr_ref[part][:, sl]], axis=0))
            yret_ref[rows, sl] = inner_ref[c] + _dot(qd_ref[c], state_bf16(sret_ref, st))
            for rs, ls in diag_tiles:
                srw_ref[st, rs, ls] = (srw_ref[st, rs, ls] * dec[rs, :]
                                       + jnp.where(same_head[rs, ls], upd[rs, ls], 0.0))
                sret_ref[st, rs, ls] = (sret_ref[st, rs, ls] * cdec_ref[st % n_tiles, rs, ls]
                                        + kv_ref[c, rs, ls])


def _mix_call(streams, dmask, qdec, kdec, cdec, rgn_g, rgn_b, wgn_g, wgn_b, rk):
    q, k, v, g, r, km, vr, lwh, lwl, a, kku, gate = streams
    bsz, s, width = q.shape
    n_tiles = width // MXU_DIM
    blk_rows = min(CHUNKS_PER_STEP * CHUNK, s)
    n_steps = s // blk_rows
    cur = pl.BlockSpec((bsz, blk_rows, width), lambda j: (0, jnp.minimum(j, n_steps - 1), 0))
    prev_idx = lambda j: (0, jnp.maximum(j - 1, 0), 0)
    prev = pl.BlockSpec((bsz, blk_rows, width), prev_idx)
    current_streams = [q, k, v, r, km, vr, lwh, lwl, a, kku]
    previous_streams = [g, gate, r, km, vr]
    consts = [dmask, qdec, kdec, cdec, rgn_g, rgn_b, wgn_g, wgn_b, rk]
    n_states = bsz * n_tiles
    n_chains = (blk_rows // CHUNK) * n_states
    tc = bsz * blk_rows
    state = (n_states, MXU_DIM, MXU_DIM)
    tile = (n_chains, MXU_DIM, MXU_DIM)
    flat = (n_chains, CHUNK, MXU_DIM)
    pair = (n_chains, 2 * CHUNK, MXU_DIM)
    return pl.pallas_call(
        _mix_kernel,
        grid=(n_steps + 1,),
        in_specs=([cur] * len(current_streams) + [prev] * len(previous_streams)
                  + [_const_spec(c.shape) for c in consts]),
        out_specs=pl.BlockSpec((bsz, blk_rows, 2 * width), prev_idx),
        out_shape=jax.ShapeDtypeStruct((bsz, s, 2 * width), BF16),
        scratch_shapes=[pltpu.VMEM(state, F32),
                        pltpu.VMEM(state, F32),
                        pltpu.VMEM((tc, width), F32), pltpu.VMEM((tc, width), F32),
                        pltpu.VMEM((tc, width), F32), pltpu.VMEM((tc, width), F32),
                        pltpu.VMEM(flat, BF16), pltpu.VMEM(flat, F32), pltpu.VMEM(flat, BF16),
                        pltpu.VMEM(pair, F32),
                        pltpu.VMEM((n_chains, MXU_DIM, 2 * CHUNK), BF16),
                        pltpu.VMEM((n_chains, MXU_DIM, LANES), F32),
                        pltpu.VMEM(pair, BF16),
                        pltpu.VMEM(flat, F32), pltpu.VMEM(flat, BF16), pltpu.VMEM(tile, F32)],
        compiler_params=pltpu.CompilerParams(dimension_semantics=("arbitrary",),
                                             vmem_limit_bytes=VMEM_LIMIT_BYTES),
        name="mix_recurrences",
    )(*current_streams, *previous_streams, *consts)


def _retention_tables(n_heads):
    h = jnp.arange(n_heads, dtype=F32)
    log_gamma = jnp.log1p(-jnp.exp2(-5.0 - h))
    lg_lane = jnp.repeat(log_gamma, HEAD_DIM)[None, :]
    idx = jnp.arange(CHUNK, dtype=F32)[:, None]
    qdec = jnp.exp((idx + 1.0) * lg_lane)
    kdec = jnp.exp((CHUNK - 1.0 - idx) * lg_lane)
    n_tiles = n_heads // HEADS_PER_TILE
    cdec = jnp.broadcast_to(jnp.exp(CHUNK * lg_lane).reshape(n_tiles, MXU_DIM, 1),
                            (n_tiles, MXU_DIM, MXU_DIM))
    rel = idx - (jnp.arange(MXU_DIM) % HEAD_DIM).astype(F32)[None, :]
    lg_tiles = lg_lane.reshape(n_tiles, 1, MXU_DIM)
    dmask = jnp.where((rel >= 0)[None], jnp.exp(jnp.where(rel >= 0, rel, 0.0)[None] * lg_tiles), 0.0)
    return dmask, qdec, kdec, cdec


def _rotary_tables(s):
    pos = jnp.arange(s, dtype=F32)
    inv_freq = ROPE_BASE ** (-jnp.arange(0, HEAD_DIM, 2, dtype=F32) / HEAD_DIM)
    ang = pos[:, None] * inv_freq[None, :]
    cos, sin = jnp.cos(ang), jnp.sin(ang)
    reps = LANES // HEAD_DIM
    cos_t = jnp.tile(jnp.concatenate([cos, cos], axis=1), (1, reps))
    sin_t = jnp.tile(jnp.concatenate([-sin, sin], axis=1), (1, reps))
    return cos_t, sin_t


def kernel(x, p, ffn1_w_gu, ffn1_w_down, ln1_g, ln1_b, w_in, ret_gn_g, ret_gn_b, rw_mu, rw_w0, rw_w_up, rw_a0, rw_a_up, rw_g_up, rw_k_k, rw_k_a, rw_r_k, rw_gn_g, rw_gn_b, w_out, ln2_g, ln2_b, ffn2_w_gu, ffn2_w_down, ln3_g, ln3_b, ple_w_proj, ple_w_gate, ple_b_gate):
    bsz, s, d = x.shape
    depth = ffn1_w_gu.shape[0]
    alpha = (2.0 * depth) ** 0.25
    width = rw_w0.shape[1]
    n_heads = width // HEAD_DIM
    ret_cols = 4 * width
    lora = DECAY_LORA + AAA_LORA + GATE_LORA
    lora_pad = -(-lora // LANES) * LANES

    dmask, qdec, kdec, cdec = _retention_tables(n_heads)
    cos_t, sin_t = _rotary_tables(s)
    row = lambda v: v.reshape(1, -1)

    h = x.reshape(bsz * s, d)
    for i in range(depth):
        h = _ffn_call(h, ffn1_w_gu[i], ffn1_w_down[i],
                      row(ln1_g[i]), row(ln1_b[i]), alpha)

        mu = row(jnp.pad(rw_mu[i], (0, lora_pad - lora)))
        n_da = DECAY_LORA + AAA_LORA
        wda = jnp.zeros((n_da, 2 * width), BF16)
        wda = wda.at[:DECAY_LORA, :width].set(rw_w_up[i].astype(BF16))
        wda = wda.at[DECAY_LORA:, width:].set(rw_a_up[i].astype(BF16))
        wg = jnp.pad(rw_g_up[i].astype(BF16), ((0, lora_pad - lora), (0, 0)))
        streams = _inproj_call(h.reshape(bsz, s, d), jnp.swapaxes(w_in[i], 0, 1), wda, wg, cos_t, sin_t, mu,
                               row(rw_w0[i]), row(rw_a0[i]), row(rw_k_k[i]), row(rw_k_a[i]))
        mixed = _mix_call(streams, dmask, qdec, kdec, cdec, row(ret_gn_g[i]), row(ret_gn_b[i]),
                          row(rw_gn_g[i]), row(rw_gn_b[i]), row(rw_r_k[i]))
        pre = (mixed.reshape(bsz * s, 2 * width), w_out[i].astype(BF16), row(ln2_g[i]), row(ln2_b[i]))
        ple = (p[i].reshape(bsz * s, -1), ple_w_proj[i].astype(BF16), ple_w_gate[i].astype(BF16),
               row(ple_b_gate[i]))
        h = _ffn_call(h, ffn2_w_gu[i], ffn2_w_down[i],
                      row(ln3_g[i]), row(ln3_b[i]), alpha, pre=pre, ple=ple)
    return h.reshape(bsz, s, d)
```

```python
import functools
import math

import jax
import jax.numpy as jnp
from jax import lax
from jax.experimental import pallas as pl
from jax.experimental.pallas import tpu as pltpu

D_PLE = 256
HEAD_DIM = 64
ROPE_BASE = 10000.0
DECAY_LORA = 64
AAA_LORA = 64
GATE_LORA = 160
LN_EPS = 1e-5
RET_GN_EPS = 1e-5
RWKV_GN_EPS = 64e-5
DECAY_SCALE = math.exp(-0.5)
HEAD_SHIFT = HEAD_DIM.bit_length() - 1

LANES = 128
MXU_DIM = 256
HEADS_PER_TILE = MXU_DIM // HEAD_DIM
CHUNK = 64
CHUNKS_PER_STEP = 2
VMEM_LIMIT_BYTES = 56 * 1024 * 1024

F32 = jnp.float32
BF16 = jnp.bfloat16


def _dot(a, b):
    return jnp.dot(a, b, preferred_element_type=F32)


def _dot_nt(a, b):
    return lax.dot_general(a, b, (((1,), (1,)), ((), ())), preferred_element_type=F32)


def _dot_tn(a, b):
    return lax.dot_general(a, b, (((0,), (0,)), ((), ())), preferred_element_type=F32)


def _layer_norm(y, g, b):
    mu = jnp.mean(y, axis=-1, keepdims=True)
    yc = y - mu
    var = jnp.mean(yc * yc, axis=-1, keepdims=True)
    return yc * lax.rsqrt(var + LN_EPS) * g + b


def _const_spec(shape):
    nd = len(shape)
    return pl.BlockSpec(shape, lambda *_: (0,) * nd, pipeline_mode=pl.Buffered(1))


def _ffn_kernel(*refs, alpha, fc, with_pre, with_ple):
    refs = list(refs)
    x_ref = refs.pop(0)
    if with_pre:
        m_ref, wo_ref, gp_ref, bp_ref = refs[:4]
        refs = refs[4:]
    wgu_hbm, wd_hbm, g_ref, b_ref = refs[:4]
    refs = refs[4:]
    if with_ple:
        p_ref, wp_ref, wpg_ref, bpg_ref = refs[:4]
        refs = refs[4:]
    o_ref = refs.pop(0)
    xs_ref = refs.pop(0) if with_pre else x_ref
    act_ref, wgu_ref, wd_ref, sem = refs
    d_ff = wd_ref.shape[0]
    n_chunks = d_ff // fc

    def cols(c, up):
        return pl.ds((d_ff if up else 0) + c * fc, fc)

    def copies():
        out = [pltpu.make_async_copy(wgu_hbm.at[:, cols(c, up)], wgu_ref.at[:, cols(c, up)], sem.at[2 * c + up])
               for c in range(n_chunks) for up in (0, 1)]
        return out + [pltpu.make_async_copy(wd_hbm, wd_ref, sem.at[2 * n_chunks])]

    def step(first):
        if first:
            pending = copies()
            for cp in pending:
                cp.start()
        if with_pre:
            xs_ref[...] = _layer_norm(alpha * x_ref[...] + _dot(m_ref[...], wo_ref[...]), gp_ref[...], bp_ref[...])
        xb = xs_ref[...].astype(BF16)
        for c in range(n_chunks):
            if first:
                pending[2 * c].wait()
                pending[2 * c + 1].wait()
            hg = _dot(xb, wgu_ref[:, cols(c, 0)].astype(BF16))
            hu = _dot(xb, wgu_ref[:, cols(c, 1)].astype(BF16))
            act_ref[:, c * fc:(c + 1) * fc] = (hg * jax.nn.sigmoid(hg) * hu).astype(BF16)
        if first:
            pending[2 * n_chunks].wait()
        down = _dot(act_ref[...], wd_ref[...].astype(BF16))
        y = _layer_norm(alpha * xs_ref[...] + 0.5 * down, g_ref[...], b_ref[...])
        if with_ple:
            gate = jax.nn.sigmoid(_dot(y.astype(BF16), wpg_ref[...]) + bpg_ref[...])
            y = y + gate * _dot(p_ref[...].astype(BF16), wp_ref[...])
        o_ref[...] = y

    pl.when(pl.program_id(0) == 0)(lambda: step(True))
    pl.when(pl.program_id(0) != 0)(lambda: step(False))


def _ffn_call(x2d, wgu, wd, ln_g, ln_b, alpha, pre=None, ple=None, tm=512, fc=MXU_DIM):
    n, d = x2d.shape
    d_ff = wd.shape[0]
    tm = min(tm, n)
    row = lambda i: (i, 0)
    in_specs = [pl.BlockSpec((tm, d), row)]
    args = [x2d]
    if pre is not None:
        m2d, wo, gp, bp = pre
        in_specs += [pl.BlockSpec((tm, m2d.shape[1]), row), _const_spec(wo.shape),
                     _const_spec((1, d)), _const_spec((1, d))]
        args += [m2d, wo, gp, bp]
    in_specs += [pl.BlockSpec(memory_space=pl.ANY), pl.BlockSpec(memory_space=pl.ANY),
                 _const_spec((1, d)), _const_spec((1, d))]
    args += [wgu, wd, ln_g, ln_b]
    if ple is not None:
        p2d, wp, wpg, bpg = ple
        in_specs += [pl.BlockSpec((tm, p2d.shape[1]), row), _const_spec(wp.shape),
                     _const_spec(wpg.shape), _const_spec((1, d))]
        args += [p2d, wp, wpg, bpg]
    return pl.pallas_call(
        functools.partial(_ffn_kernel, alpha=alpha, fc=fc, with_pre=pre is not None,
                          with_ple=ple is not None),
        grid=(n // tm,),
        in_specs=in_specs,
        out_specs=pl.BlockSpec((tm, d), row),
        out_shape=jax.ShapeDtypeStruct((n, d), F32),
        scratch_shapes=([pltpu.VMEM((tm, d), F32)] if pre is not None else [])
        + [pltpu.VMEM((tm, d_ff), BF16), pltpu.VMEM(wgu.shape, wgu.dtype), pltpu.VMEM(wd.shape, wd.dtype),
           pltpu.SemaphoreType.DMA((2 * (d_ff // fc) + 1,))],
        compiler_params=pltpu.CompilerParams(dimension_semantics=("arbitrary",),
                                             vmem_limit_bytes=VMEM_LIMIT_BYTES),
        name="ffn_ln",
    )(*args)


def _group_sum(x, ones_blk):
    outs = [_dot(x[:, t * MXU_DIM:(t + 1) * MXU_DIM].astype(BF16), ones_blk)
            for t in range(x.shape[1] // MXU_DIM)]
    return jnp.concatenate(outs, axis=1)


def _inproj_kernel(x_ref, win_ref, wda_ref, wg_ref, cos_ref, sin_ref, mu_ref, w0_ref, a0_ref,
                   kk_ref, ka_ref,
                   q_ref, k_ref, v_ref, g_ref, r_ref, km_ref, vr_ref, lwh_ref, lwl_ref, a_ref, kku_ref,
                   gate_ref, carry_ref, *, width):
    tm = x_ref.shape[0]

    @pl.when(pl.program_id(1) == 0)
    def _():
        carry_ref[...] = jnp.zeros_like(carry_ref)

    xb = x_ref[...].astype(BF16)

    lane = lax.broadcasted_iota(jnp.int32, (tm, width), 1)
    first_half = (lane & (HEAD_DIM // 2)) == 0
    reps = width // cos_ref.shape[1]
    cos = jnp.concatenate([cos_ref[...]] * reps, axis=1)
    sin = jnp.concatenate([sin_ref[...]] * reps, axis=1)

    def rotary(t):
        swapped = jnp.where(first_half, pltpu.roll(t, width - HEAD_DIM // 2, 1),
                            pltpu.roll(t, HEAD_DIM // 2, 1))
        return t * cos + swapped * sin

    def put(ref, val):
        ref[...] = val.astype(ref.dtype)

    ret_cols = 4 * width
    n_lora = win_ref.shape[0] - ret_cols - 3 * width
    lora_pad = mu_ref.shape[1] - 3 * width

    def project(lo_col, hi_col, pad=0):
        w = win_ref[lo_col:hi_col, :].astype(BF16)
        if pad:
            w = jnp.concatenate([w, jnp.zeros((pad, w.shape[1]), BF16)], axis=0)
        return _dot_nt(xb, w)

    def shifted(lo_col, hi_col, z):
        first = carry_ref[0:1, lo_col:hi_col]
        carry_ref[0:1, lo_col:hi_col] = z[tm - 1:tm, :]
        rowid = lax.broadcasted_iota(jnp.int32, z.shape, 0)
        prev = jnp.where(rowid == 0, first, pltpu.roll(z, 1, 0))
        return z + (prev - z) * mu_ref[:, lo_col:hi_col]

    lo = shifted(3 * width, 3 * width + lora_pad,
                 project(ret_cols + 3 * width, ret_cols + 3 * width + n_lora, pad=lora_pad - n_lora))
    n_da = DECAY_LORA + AAA_LORA
    lo_da = lo[:, 0:n_da]
    ll = lax.broadcasted_iota(jnp.int32, lo_da.shape, 1)
    act_da = jnp.where(ll < DECAY_LORA, jnp.tanh(lo_da), lo_da).astype(BF16)
    up = _dot(act_da, wda_ref[...])
    put(gate_ref, _dot(jax.nn.sigmoid(lo[:, n_da:]).astype(BF16), wg_ref[...]))
    lw = -DECAY_SCALE * jax.nn.sigmoid(w0_ref[...] + up[:, 0 * width:1 * width])
    lw_hi, lw_lo = _split_hi_lo(lw)
    lwh_ref[...] = lw_hi
    lwl_ref[...] = lw_lo
    a = jax.nn.sigmoid(a0_ref[...] + up[:, 1 * width:2 * width])
    put(a_ref, a)
    kr = shifted(1 * width, 2 * width, project(ret_cols + 1 * width, ret_cols + 2 * width))
    put(kku_ref, kr * kk_ref[...])
    put(km_ref, kr * (1.0 + (a - 1.0) * ka_ref[...]))
    put(r_ref, shifted(0 * width, 1 * width, project(ret_cols + 0 * width, ret_cols + 1 * width)))
    put(vr_ref, shifted(2 * width, 3 * width, project(ret_cols + 2 * width, ret_cols + 3 * width)))

    put(q_ref, rotary(project(0 * width, 1 * width)))
    put(k_ref, rotary(project(1 * width, 2 * width)) * (HEAD_DIM ** -0.5))
    put(v_ref, project(2 * width, 3 * width))
    put(g_ref, project(3 * width, 4 * width))


def _inproj_call(x3d, w_in_t, wda, wg, cos, sin, mu, w0, a0, k_k, k_a, tm=512):
    bsz, s, d = x3d.shape
    width = w0.shape[1]
    tm = min(tm, s)
    blk = lambda b, j: (b, j, 0)
    out_spec = pl.BlockSpec((None, tm, width), blk)
    n_out = 12
    out_sds = [jax.ShapeDtypeStruct((bsz, s, width), BF16)] * n_out
    return pl.pallas_call(
        functools.partial(_inproj_kernel, width=width),
        grid=(bsz, s // tm),
        in_specs=[pl.BlockSpec((None, tm, d), blk), _const_spec(w_in_t.shape),
                  _const_spec(wda.shape), _const_spec(wg.shape),
                  pl.BlockSpec((tm, cos.shape[1]), lambda b, j: (j, 0)),
                  pl.BlockSpec((tm, sin.shape[1]), lambda b, j: (j, 0)),
                  _const_spec(mu.shape), _const_spec(w0.shape), _const_spec(a0.shape),
                  _const_spec(k_k.shape), _const_spec(k_a.shape)],
        out_specs=[out_spec] * n_out,
        out_shape=out_sds,
        scratch_shapes=[pltpu.VMEM((8, mu.shape[1]), F32)],
        compiler_params=pltpu.CompilerParams(dimension_semantics=("arbitrary", "arbitrary"),
                                             vmem_limit_bytes=VMEM_LIMIT_BYTES),
        name="inproj_prep",
    )(x3d, w_in_t, wda, wg, cos, sin, mu, w0, a0, k_k, k_a)


def _split_hi_lo(x):
    hi = x.astype(BF16)
    lo = (x - hi.astype(F32)).astype(BF16)
    return hi, lo


def _mix_kernel(q_ref, k_ref, v_ref, r_ref, km_ref, vr_ref, lwh_ref, lwl_ref, a_ref, kku_ref,
                g_ref, gate_ref, rl_ref, kml_ref, vrl_ref,
                dmask_ref, qdec_ref, kdec_ref, cdec_ref, rgn_g_ref, rgn_b_ref, wgn_g_ref, wgn_b_ref,
                rk_ref, o_ref,
                sret_ref, srw_ref, kkn_ref, p_ref, yret_ref, yrw_ref,
                pw_ref, tinv_ref, brp_ref, av_ref, pkt_ref, dec_ref, qr_ref, inner_ref, qd_ref, kv_ref):
    bsz, blk_rows, width = q_ref.shape
    n_sub = blk_rows // CHUNK
    n_tiles = width // MXU_DIM
    n_states = bsz * n_tiles
    n_chains = n_sub * n_states
    hpt = HEADS_PER_TILE
    n_rows = bsz * blk_rows

    def full(ref):
        return ref[...].reshape(n_rows, ref.shape[-1]).astype(F32)

    @pl.when(pl.program_id(0) == 0)
    def _():
        sret_ref[...] = jnp.zeros_like(sret_ref)
        srw_ref[...] = jnp.zeros_like(srw_ref)
        yret_ref[...] = jnp.zeros_like(yret_ref)
        yrw_ref[...] = jnp.zeros_like(yrw_ref)

    ri = lax.broadcasted_iota(jnp.int32, (MXU_DIM, MXU_DIM), 0)
    ci = lax.broadcasted_iota(jnp.int32, (MXU_DIM, MXU_DIM), 1)
    same_head = (ri >> HEAD_SHIFT) == (ci >> HEAD_SHIFT)
    ones_blk = same_head.astype(BF16)
    cr = lax.broadcasted_iota(jnp.int32, (CHUNK, MXU_DIM), 0)
    cj = lax.broadcasted_iota(jnp.int32, (CHUNK, MXU_DIM), 1) & (HEAD_DIM - 1)
    strict_lower, incl_lower = cr > cj, cr >= cj
    eye_c = (cr == cj).astype(F32)
    ti = lax.broadcasted_iota(jnp.int32, (2 * CHUNK, CHUNK), 0)
    tj = lax.broadcasted_iota(jnp.int32, (2 * CHUNK, CHUNK), 1)
    tri = (((ti < CHUNK) & (ti >= tj)) | ((ti >= CHUNK) & (ti - CHUNK > tj))).astype(BF16)

    heads_per_lane_tile = LANES // HEAD_DIM
    diag_tiles = [(slice(h * HEAD_DIM, (h + 1) * HEAD_DIM),
                   slice((h // heads_per_lane_tile) * LANES, (h // heads_per_lane_tile + 1) * LANES))
                  for h in range(hpt)]

    def state_bf16(ref, c):
        zeros = jnp.zeros((HEAD_DIM, LANES), BF16)
        rows = []
        for rs, ls in diag_tiles:
            parts = [zeros] * (MXU_DIM // LANES)
            parts[ls.start // LANES] = ref[c, rs, ls].astype(BF16)
            rows.append(jnp.concatenate(parts, axis=1))
        return jnp.concatenate(rows, axis=0)

    def block_diag(x):
        return jnp.where(same_head, jnp.concatenate([x] * hpt, axis=0), jnp.zeros((), x.dtype))

    def group_sum(x):
        return _group_sum(x, ones_blk)

    kku = full(kku_ref)
    kkn = kku * lax.rsqrt(jnp.maximum(group_sum(kku * kku), 1e-24))
    kkn_ref[...] = kkn
    p_ref[...] = -(kkn * full(a_ref))

    def prepare(sub, b):
        rows = pl.ds(b * blk_rows + sub * CHUNK, CHUNK)
        part = (b, slice(sub * CHUNK, (sub + 1) * CHUNK))
        cums = _dot(tri, lwh_ref[part]) + _dot(tri, lwl_ref[part])
        cum = cums[0:CHUNK]
        e_in = jnp.exp(cum)
        e_ex = jnp.exp(cums[CHUNK:])
        e_neg = jnp.exp(-cum)
        e_last = e_in[CHUNK - 1:CHUNK, :]
        qt_all = kkn_ref[rows, :] * e_ex
        rt_all = r_ref[part].astype(F32) * e_in
        pt_all = p_ref[rows, :] * e_neg
        kt_all = km_ref[part].astype(F32) * e_neg
        ph_all = pt_all * e_last
        kh_all = kt_all * e_last
        vr_all = vr_ref[part].astype(BF16)
        q_all = q_ref[part]
        k_all = k_ref[part]
        v_all = v_ref[part].astype(BF16)
        qd_all = (q_all.astype(F32) * qdec_ref[...]).astype(BF16)
        kd_all = (k_all.astype(F32) * kdec_ref[...]).astype(BF16)
        qt_b, rt_b = qt_all.astype(BF16), rt_all.astype(BF16)
        pt_b, kt_b = pt_all.astype(BF16), kt_all.astype(BF16)
        q_b, k_b = q_all.astype(BF16), k_all.astype(BF16)

        for t in range(n_tiles):
            c = sub * n_states + b * n_tiles + t
            sl = slice(t * MXU_DIM, (t + 1) * MXU_DIM)
            qr = jnp.concatenate([qt_b[:, sl], rt_b[:, sl]], axis=0)
            qr_ref[c] = qr
            g_p = _dot_nt(qr, block_diag(pt_b[:, sl]))
            g_k = _dot_nt(qr, block_diag(kt_b[:, sl]))
            a_qp = jnp.where(strict_lower, g_p[0:CHUNK], 0.0)
            pw_ref[c] = a_qp.astype(BF16)
            tinv_ref[c] = a_qp + eye_c
            brp_ref[c] = jnp.where(incl_lower, g_p[CHUNK:], 0.0).astype(BF16)
            ab = jnp.concatenate([jnp.where(strict_lower, g_k[0:CHUNK], 0.0),
                                  jnp.where(incl_lower, g_k[CHUNK:], 0.0)], axis=0).astype(BF16)
            av_ref[c] = _dot(ab, block_diag(vr_all[:, sl]))
            pkt_ref[c] = jnp.transpose(jnp.concatenate([ph_all[:, sl], kh_all[:, sl]], axis=0)).astype(BF16)
            dec_ref[c] = jnp.transpose(jnp.broadcast_to(e_last[:, sl], (LANES, MXU_DIM)))
            scores = (_dot_nt(q_b[:, sl], block_diag(k_b[:, sl])) * dmask_ref[t]).astype(BF16)
            inner_ref[c] = _dot(scores, block_diag(v_all[:, sl]))
            qd_ref[c] = qd_all[:, sl]
            kv = _dot_tn(kd_all[:, sl], v_all[:, sl])
            for rs, ls in diag_tiles:
                kv_ref[c, rs, ls] = jnp.where(same_head[rs, ls], kv[rs, ls], 0.0)

    for sub in range(n_sub):
        for b in range(bsz):
            prepare(sub, b)

    inv_hd = 1.0 / HEAD_DIM

    def group_norm(y, eps):
        mu = group_sum(y) * inv_hd
        yc = y - mu
        var = group_sum(yc * yc) * inv_hd
        return yc * lax.rsqrt(var + eps)

    def norm_retention():
        yret_ref[...] = group_norm(yret_ref[...], RET_GN_EPS) * rgn_g_ref[...] + rgn_b_ref[...]

    def norm_rwkv():
        bonus = group_sum(full(rl_ref) * full(kml_ref) * rk_ref[...]) * full(vrl_ref)
        yrw_ref[...] = group_norm(yrw_ref[...], RWKV_GN_EPS) * wgn_g_ref[...] + wgn_b_ref[...] + bonus

    def gate_and_store(b_lo, b_hi):
        for b in range(b_lo, b_hi):
            rows = pl.ds(b * blk_rows, blk_rows)
            g = g_ref[b].astype(F32)
            o_ref[b, :, 0:width] = (g * jax.nn.sigmoid(g) * yret_ref[rows, :]).astype(o_ref.dtype)
            o_ref[b, :, width:2 * width] = (yrw_ref[rows, :] * gate_ref[b].astype(F32)).astype(o_ref.dtype)

    filler = [norm_retention, norm_rwkv, functools.partial(gate_and_store, 0, bsz // 2),
              functools.partial(gate_and_store, bsz // 2, bsz)]

    for c in range(n_chains):
        pwb = pw_ref[c]
        pw_ref[c] = _dot(pwb, block_diag(pwb)).astype(BF16)
    for rnd in range(4):
        for c in range(n_chains):
            pwb, t_acc = pw_ref[c], tinv_ref[c]
            res = _dot(jnp.concatenate([pwb, t_acc.astype(BF16)], axis=0), block_diag(pwb))
            pw_ref[c] = res[0:CHUNK].astype(BF16)
            tinv_ref[c] = t_acc + res[CHUNK:]
        filler[rnd]()
    for c in range(n_chains):
        t_acc = tinv_ref[c]
        tinv_ref[c] = t_acc + _dot(t_acc.astype(BF16), block_diag(pw_ref[c]))

    for sub in range(n_sub):
        chains = [(sub * n_states + st, st, st // n_tiles, (st // n_tiles, slice(sub * CHUNK, (sub + 1) * CHUNK)),
                   pl.ds((st // n_tiles) * blk_rows + sub * CHUNK, CHUNK),
                   slice((st % n_tiles) * MXU_DIM, (st % n_tiles + 1) * MXU_DIM)) for st in range(n_states)]
        qrm = {c: _dot(qr_ref[c], state_bf16(srw_ref, st)) for c, st, _, _, _, _ in chains}
        u = {c: _dot(tinv_ref[c].astype(BF16),
                     block_diag((qrm[c][0:CHUNK] + av_ref[c, 0:CHUNK, :]).astype(BF16)))
             for c, _, _, _, _, _ in chains}
        for c, st, b, part, rows, sl in chains:
            ub = u[c].astype(BF16)
            yrw_ref[rows, sl] = qrm[c][CHUNK:] + av_ref[c, CHUNK:, :] + _dot(brp_ref[c], block_diag(ub))
            dec = dec_ref[c]
            upd = _dot(pkt_ref[c], jnp.concatenate([ub, vr_ref[part][:, sl]], axis=0))
            yret_ref[rows, sl] = inner_ref[c] + _dot(qd_ref[c], state_bf16(sret_ref, st))
            for rs, ls in diag_tiles:
                srw_ref[st, rs, ls] = (srw_ref[st, rs, ls] * dec[rs, :]
                                       + jnp.where(same_head[rs, ls], upd[rs, ls], 0.0))
                sret_ref[st, rs, ls] = (sret_ref[st, rs, ls] * cdec_ref[st % n_tiles, rs, ls]
                                        + kv_ref[c, rs, ls])


def _mix_call(streams, dmask, qdec, kdec, cdec, rgn_g, rgn_b, wgn_g, wgn_b, rk):
    q, k, v, g, r, km, vr, lwh, lwl, a, kku, gate = streams
    bsz, s, width = q.shape
    n_tiles = width // MXU_DIM
    blk_rows = min(CHUNKS_PER_STEP * CHUNK, s)
    n_steps = s // blk_rows
    cur = pl.BlockSpec((bsz, blk_rows, width), lambda j: (0, jnp.minimum(j, n_steps - 1), 0))
    prev_idx = lambda j: (0, jnp.maximum(j - 1, 0), 0)
    prev = pl.BlockSpec((bsz, blk_rows, width), prev_idx)
    current_streams = [q, k, v, r, km, vr, lwh, lwl, a, kku]
    previous_streams = [g, gate, r, km, vr]
    consts = [dmask, qdec, kdec, cdec, rgn_g, rgn_b, wgn_g, wgn_b, rk]
    n_states = bsz * n_tiles
    n_chains = (blk_rows // CHUNK) * n_states
    tc = bsz * blk_rows
    state = (n_states, MXU_DIM, MXU_DIM)
    tile = (n_chains, MXU_DIM, MXU_DIM)
    flat = (n_chains, CHUNK, MXU_DIM)
    pair = (n_chains, 2 * CHUNK, MXU_DIM)
    return pl.pallas_call(
        _mix_kernel,
        grid=(n_steps + 1,),
        in_specs=([cur] * len(current_streams) + [prev] * len(previous_streams)
                  + [_const_spec(c.shape) for c in consts]),
        out_specs=pl.BlockSpec((bsz, blk_rows, 2 * width), prev_idx),
        out_shape=jax.ShapeDtypeStruct((bsz, s, 2 * width), BF16),
        scratch_shapes=[pltpu.VMEM(state, F32),
                        pltpu.VMEM(state, F32),
                        pltpu.VMEM((tc, width), F32), pltpu.VMEM((tc, width), F32),
                        pltpu.VMEM((tc, width), F32), pltpu.VMEM((tc, width), F32),
                        pltpu.VMEM(flat, BF16), pltpu.VMEM(flat, F32), pltpu.VMEM(flat, BF16),
                        pltpu.VMEM(pair, F32),
                        pltpu.VMEM((n_chains, MXU_DIM, 2 * CHUNK), BF16),
                        pltpu.VMEM((n_chains, MXU_DIM, LANES), F32),
                        pltpu.VMEM(pair, BF16),
                        pltpu.VMEM(flat, F32), pltpu.VMEM(flat, BF16), pltpu.VMEM(tile, F32)],
        compiler_params=pltpu.CompilerParams(dimension_semantics=("arbitrary",),
                                             vmem_limit_bytes=VMEM_LIMIT_BYTES),
        name="mix_recurrences",
    )(*current_streams, *previous_streams, *consts)


def _retention_tables(n_heads):
    h = jnp.arange(n_heads, dtype=F32)
    log_gamma = jnp.log1p(-jnp.exp2(-5.0 - h))
    lg_lane = jnp.repeat(log_gamma, HEAD_DIM)[None, :]
    idx = jnp.arange(CHUNK, dtype=F32)[:, None]
    qdec = jnp.exp((idx + 1.0) * lg_lane)
    kdec = jnp.exp((CHUNK - 1.0 - idx) * lg_lane)
    n_tiles = n_heads // HEADS_PER_TILE
    cdec = jnp.broadcast_to(jnp.exp(CHUNK * lg_lane).reshape(n_tiles, MXU_DIM, 1),
                            (n_tiles, MXU_DIM, MXU_DIM))
    rel = idx - (jnp.arange(MXU_DIM) % HEAD_DIM).astype(F32)[None, :]
    lg_tiles = lg_lane.reshape(n_tiles, 1, MXU_DIM)
    dmask = jnp.where((rel >= 0)[None], jnp.exp(jnp.where(rel >= 0, rel, 0.0)[None] * lg_tiles), 0.0)
    return dmask, qdec, kdec, cdec


def _rotary_tables(s):
    pos = jnp.arange(s, dtype=F32)
    inv_freq = ROPE_BASE ** (-jnp.arange(0, HEAD_DIM, 2, dtype=F32) / HEAD_DIM)
    ang = pos[:, None] * inv_freq[None, :]
    cos, sin = jnp.cos(ang), jnp.sin(ang)
    reps = LANES // HEAD_DIM
    cos_t = jnp.tile(jnp.concatenate([cos, cos], axis=1), (1, reps))
    sin_t = jnp.tile(jnp.concatenate([-sin, sin], axis=1), (1, reps))
    return cos_t, sin_t


def kernel(x, p, ffn1_w_gu, ffn1_w_down, ln1_g, ln1_b, w_in, ret_gn_g, ret_gn_b, rw_mu, rw_w0, rw_w_up, rw_a0, rw_a_up, rw_g_up, rw_k_k, rw_k_a, rw_r_k, rw_gn_g, rw_gn_b, w_out, ln2_g, ln2_b, ffn2_w_gu, ffn2_w_down, ln3_g, ln3_b, ple_w_proj, ple_w_gate, ple_b_gate):
    bsz, s, d = x.shape
    depth = ffn1_w_gu.shape[0]
    alpha = (2.0 * depth) ** 0.25
    width = rw_w0.shape[1]
    n_heads = width // HEAD_DIM
    ret_cols = 4 * width
    lora = DECAY_LORA + AAA_LORA + GATE_LORA
    lora_pad = -(-lora // LANES) * LANES

    dmask, qdec, kdec, cdec = _retention_tables(n_heads)
    cos_t, sin_t = _rotary_tables(s)
    row = lambda v: v.reshape(1, -1)

    h = x.reshape(bsz * s, d)
    for i in range(depth):
        h = _ffn_call(h, ffn1_w_gu[i], ffn1_w_down[i],
                      row(ln1_g[i]), row(ln1_b[i]), alpha)

        mu = row(jnp.pad(rw_mu[i], (0, lora_pad - lora)))
        n_da = DECAY_LORA + AAA_LORA
        wda = jnp.zeros((n_da, 2 * width), BF16)
        wda = wda.at[:DECAY_LORA, :width].set(rw_w_up[i].astype(BF16))
        wda = wda.at[DECAY_LORA:, width:].set(rw_a_up[i].astype(BF16))
        wg = jnp.pad(rw_g_up[i].astype(BF16), ((0, lora_pad - lora), (0, 0)))
        streams = _inproj_call(h.reshape(bsz, s, d), jnp.swapaxes(w_in[i], 0, 1), wda, wg, cos_t, sin_t, mu,
                               row(rw_w0[i]), row(rw_a0[i]), row(rw_k_k[i]), row(rw_k_a[i]))
        mixed = _mix_call(streams, dmask, qdec, kdec, cdec, row(ret_gn_g[i]), row(ret_gn_b[i]),
                          row(rw_gn_g[i]), row(rw_gn_b[i]), row(rw_r_k[i]))
        pre = (mixed.reshape(bsz * s, 2 * width), w_out[i].astype(BF16), row(ln2_g[i]), row(ln2_b[i]))
        ple = (p[i].reshape(bsz * s, -1), ple_w_proj[i].astype(BF16), ple_w_gate[i].astype(BF16),
               row(ple_b_gate[i]))
        h = _ffn_call(h, ffn2_w_gu[i], ffn2_w_down[i].astype(BF16),
                      row(ln3_g[i]), row(ln3_b[i]), alpha, pre=pre, ple=ple)
    return h.reshape(bsz, s, d)
```

```python
import functools
import math

import jax
import jax.numpy as jnp
from jax import lax
from jax.experimental import pallas as pl
from jax.experimental.pallas import tpu as pltpu

D_PLE = 256
HEAD_DIM = 64
ROPE_BASE = 10000.0
DECAY_LORA = 64
AAA_LORA = 64
GATE_LORA = 160
LN_EPS = 1e-5
RET_GN_EPS = 1e-5
RWKV_GN_EPS = 64e-5
DECAY_SCALE = math.exp(-0.5)
HEAD_SHIFT = HEAD_DIM.bit_length() - 1

LANES = 128
MXU_DIM = 256
HEADS_PER_TILE = MXU_DIM // HEAD_DIM
CHUNK = 64
CHUNKS_PER_STEP = 2
VMEM_LIMIT_BYTES = 56 * 1024 * 1024

F32 = jnp.float32
BF16 = jnp.bfloat16


def _dot(a, b):
    return jnp.dot(a, b, preferred_element_type=F32)


def _dot_nt(a, b):
    return lax.dot_general(a, b, (((1,), (1,)), ((), ())), preferred_element_type=F32)


def _dot_tn(a, b):
    return lax.dot_general(a, b, (((0,), (0,)), ((), ())), preferred_element_type=F32)


def _sigmoid(x):
    return 0.5 * jnp.tanh(0.5 * x) + 0.5


def _layer_norm(y, g, b):
    mu = jnp.mean(y, axis=-1, keepdims=True)
    yc = y - mu
    var = jnp.mean(yc * yc, axis=-1, keepdims=True)
    return yc * lax.rsqrt(var + LN_EPS) * g + b


def _const_spec(shape):
    nd = len(shape)
    return pl.BlockSpec(shape, lambda *_: (0,) * nd, pipeline_mode=pl.Buffered(1))


def _ffn_kernel(*refs, alpha, fc, with_pre, with_ple):
    refs = list(refs)
    x_ref = refs.pop(0)
    if with_pre:
        m_ref, wo_ref, gp_ref, bp_ref = refs[:4]
        refs = refs[4:]
    wgu_ref, wd_ref, g_ref, b_ref = refs[:4]
    refs = refs[4:]
    if with_ple:
        p_ref, wp_ref, wpg_ref, bpg_ref = refs[:4]
        refs = refs[4:]
    o_ref, act_ref = refs[:2]
    d_ff = wd_ref.shape[0]
    if with_pre:
        xs_ref = refs[2]
        xs_ref[...] = _layer_norm(alpha * x_ref[...] + _dot(m_ref[...], wo_ref[...]), gp_ref[...], bp_ref[...])
    else:
        xs_ref = x_ref
    xb = xs_ref[...].astype(BF16)
    for c in range(d_ff // fc):
        hg = _dot(xb, wgu_ref[:, c * fc:(c + 1) * fc].astype(BF16))
        hu = _dot(xb, wgu_ref[:, d_ff + c * fc:d_ff + (c + 1) * fc].astype(BF16))
        act_ref[:, c * fc:(c + 1) * fc] = (hg * jax.nn.sigmoid(hg) * hu).astype(BF16)
    down = _dot(act_ref[...], wd_ref[...].astype(BF16))
    y = _layer_norm(alpha * xs_ref[...] + 0.5 * down, g_ref[...], b_ref[...])
    if with_ple:
        gate = jax.nn.sigmoid(_dot(y.astype(BF16), wpg_ref[...]) + bpg_ref[...])
        y = y + gate * _dot(p_ref[...].astype(BF16), wp_ref[...])
    o_ref[...] = y


def _ffn_call(x2d, wgu, wd, ln_g, ln_b, alpha, pre=None, ple=None, tm=512, fc=MXU_DIM):
    n, d = x2d.shape
    d_ff = wd.shape[0]
    tm = min(tm, n)
    row = lambda i: (i, 0)
    in_specs = [pl.BlockSpec((tm, d), row)]
    args = [x2d]
    if pre is not None:
        m2d, wo, gp, bp = pre
        in_specs += [pl.BlockSpec((tm, m2d.shape[1]), row), _const_spec(wo.shape),
                     _const_spec((1, d)), _const_spec((1, d))]
        args += [m2d, wo, gp, bp]
    in_specs += [_const_spec(wgu.shape), _const_spec(wd.shape), _const_spec((1, d)), _const_spec((1, d))]
    args += [wgu, wd, ln_g, ln_b]
    if ple is not None:
        p2d, wp, wpg, bpg = ple
        in_specs += [pl.BlockSpec((tm, p2d.shape[1]), row), _const_spec(wp.shape),
                     _const_spec(wpg.shape), _const_spec((1, d))]
        args += [p2d, wp, wpg, bpg]
    return pl.pallas_call(
        functools.partial(_ffn_kernel, alpha=alpha, fc=fc, with_pre=pre is not None,
                          with_ple=ple is not None),
        grid=(n // tm,),
        in_specs=in_specs,
        out_specs=pl.BlockSpec((tm, d), row),
        out_shape=jax.ShapeDtypeStruct((n, d), F32),
        scratch_shapes=[pltpu.VMEM((tm, d_ff), BF16)] + ([pltpu.VMEM((tm, d), F32)] if pre is not None else []),
        compiler_params=pltpu.CompilerParams(dimension_semantics=("arbitrary",),
                                             vmem_limit_bytes=VMEM_LIMIT_BYTES),
        name="ffn_ln",
    )(*args)


def _group_sum(x, ones_blk):
    outs = [_dot(x[:, t * MXU_DIM:(t + 1) * MXU_DIM].astype(BF16), ones_blk)
            for t in range(x.shape[1] // MXU_DIM)]
    return jnp.concatenate(outs, axis=1)


def _inproj_kernel(x_ref, win_ref, wda_ref, wg_ref, cos_ref, sin_ref, mu_ref, w0_ref, a0_ref,
                   kk_ref, ka_ref,
                   q_ref, k_ref, v_ref, g_ref, r_ref, km_ref, vr_ref, lwh_ref, lwl_ref, a_ref, kku_ref,
                   gate_ref, carry_ref, *, width):
    tm = x_ref.shape[0]

    @pl.when(pl.program_id(1) == 0)
    def _():
        carry_ref[...] = jnp.zeros_like(carry_ref)

    xb = x_ref[...].astype(BF16)

    lane = lax.broadcasted_iota(jnp.int32, (tm, width), 1)
    first_half = (lane & (HEAD_DIM // 2)) == 0
    reps = width // cos_ref.shape[1]
    cos = jnp.concatenate([cos_ref[...]] * reps, axis=1)
    sin = jnp.concatenate([sin_ref[...]] * reps, axis=1)

    def rotary(t):
        swapped = jnp.where(first_half, pltpu.roll(t, width - HEAD_DIM // 2, 1),
                            pltpu.roll(t, HEAD_DIM // 2, 1))
        return t * cos + swapped * sin

    def put(ref, val):
        ref[...] = val.astype(ref.dtype)

    ret_cols = 4 * width
    n_lora = win_ref.shape[0] - ret_cols - 3 * width
    lora_pad = mu_ref.shape[1] - 3 * width

    def project(lo_col, hi_col, pad=0):
        w = win_ref[lo_col:hi_col, :].astype(BF16)
        if pad:
            w = jnp.concatenate([w, jnp.zeros((pad, w.shape[1]), BF16)], axis=0)
        return _dot_nt(xb, w)

    def shifted(lo_col, hi_col, z):
        first = carry_ref[0:1, lo_col:hi_col]
        carry_ref[0:1, lo_col:hi_col] = z[tm - 1:tm, :]
        rowid = lax.broadcasted_iota(jnp.int32, z.shape, 0)
        prev = jnp.where(rowid == 0, first, pltpu.roll(z, 1, 0))
        return z + (prev - z) * mu_ref[:, lo_col:hi_col]

    lo = shifted(3 * width, 3 * width + lora_pad,
                 project(ret_cols + 3 * width, ret_cols + 3 * width + n_lora, pad=lora_pad - n_lora))
    put(v_ref, project(2 * width, 3 * width))
    n_da = DECAY_LORA + AAA_LORA
    lo_da = lo[:, 0:n_da]
    ll = lax.broadcasted_iota(jnp.int32, lo_da.shape, 1)
    act_da = jnp.where(ll < DECAY_LORA, jnp.tanh(lo_da), lo_da).astype(BF16)
    up = _dot(act_da, wda_ref[...])
    put(g_ref, project(3 * width, 4 * width))
    put(gate_ref, _dot(_sigmoid(lo[:, n_da:]).astype(BF16), wg_ref[...]))
    put(q_ref, rotary(project(0 * width, 1 * width)))
    lw = -DECAY_SCALE * _sigmoid(w0_ref[...] + up[:, 0 * width:1 * width])
    lw_hi, lw_lo = _split_hi_lo(lw)
    lwh_ref[...] = lw_hi
    lwl_ref[...] = lw_lo
    put(k_ref, rotary(project(1 * width, 2 * width)) * (HEAD_DIM ** -0.5))
    a = _sigmoid(a0_ref[...] + up[:, 1 * width:2 * width])
    put(a_ref, a)
    kr = shifted(1 * width, 2 * width, project(ret_cols + 1 * width, ret_cols + 2 * width))
    put(kku_ref, kr * kk_ref[...])
    put(km_ref, kr * (1.0 + (a - 1.0) * ka_ref[...]))
    put(r_ref, shifted(0 * width, 1 * width, project(ret_cols + 0 * width, ret_cols + 1 * width)))
    put(vr_ref, shifted(2 * width, 3 * width, project(ret_cols + 2 * width, ret_cols + 3 * width)))


def _inproj_call(x3d, w_in_t, wda, wg, cos, sin, mu, w0, a0, k_k, k_a, tm=512):
    bsz, s, d = x3d.shape
    width = w0.shape[1]
    tm = min(tm, s)
    blk = lambda b, j: (b, j, 0)
    out_spec = pl.BlockSpec((None, tm, width), blk)
    n_out = 12
    out_sds = [jax.ShapeDtypeStruct((bsz, s, width), BF16)] * n_out
    return pl.pallas_call(
        functools.partial(_inproj_kernel, width=width),
        grid=(bsz, s // tm),
        in_specs=[pl.BlockSpec((None, tm, d), blk), _const_spec(w_in_t.shape),
                  _const_spec(wda.shape), _const_spec(wg.shape),
                  pl.BlockSpec((tm, cos.shape[1]), lambda b, j: (j, 0)),
                  pl.BlockSpec((tm, sin.shape[1]), lambda b, j: (j, 0)),
                  _const_spec(mu.shape), _const_spec(w0.shape), _const_spec(a0.shape),
                  _const_spec(k_k.shape), _const_spec(k_a.shape)],
        out_specs=[out_spec] * n_out,
        out_shape=out_sds,
        scratch_shapes=[pltpu.VMEM((8, mu.shape[1]), F32)],
        compiler_params=pltpu.CompilerParams(dimension_semantics=("arbitrary", "arbitrary"),
                                             vmem_limit_bytes=VMEM_LIMIT_BYTES),
        name="inproj_prep",
    )(x3d, w_in_t, wda, wg, cos, sin, mu, w0, a0, k_k, k_a)


def _split_hi_lo(x):
    hi = x.astype(BF16)
    lo = (x - hi.astype(F32)).astype(BF16)
    return hi, lo


def _mix_kernel(q_ref, k_ref, v_ref, r_ref, km_ref, vr_ref, lwh_ref, lwl_ref, a_ref, kku_ref,
                g_ref, gate_ref, rl_ref, kml_ref, vrl_ref,
                dmask_ref, qdec_ref, kdec_ref, cdec_ref, rgn_g_ref, rgn_b_ref, wgn_g_ref, wgn_b_ref,
                rk_ref, o_ref,
                sret_ref, srw_ref, kkn_ref, p_ref, yret_ref, yrw_ref,
                pw_ref, tinv_ref, brp_ref, av_ref, pkt_ref, dec_ref, qr_ref, inner_ref, qd_ref, kv_ref):
    bsz, blk_rows, width = q_ref.shape
    n_sub = blk_rows // CHUNK
    n_tiles = width // MXU_DIM
    n_states = bsz * n_tiles
    n_chains = n_sub * n_states
    hpt = HEADS_PER_TILE
    n_rows = bsz * blk_rows

    def full(ref):
        return ref[...].reshape(n_rows, ref.shape[-1]).astype(F32)

    @pl.when(pl.program_id(0) == 0)
    def _():
        sret_ref[...] = jnp.zeros_like(sret_ref)
        srw_ref[...] = jnp.zeros_like(srw_ref)
        yret_ref[...] = jnp.zeros_like(yret_ref)
        yrw_ref[...] = jnp.zeros_like(yrw_ref)

    ri = lax.broadcasted_iota(jnp.int32, (MXU_DIM, MXU_DIM), 0)
    ci = lax.broadcasted_iota(jnp.int32, (MXU_DIM, MXU_DIM), 1)
    same_head = (ri >> HEAD_SHIFT) == (ci >> HEAD_SHIFT)
    ones_blk = same_head.astype(BF16)
    cr = lax.broadcasted_iota(jnp.int32, (CHUNK, MXU_DIM), 0)
    cj = lax.broadcasted_iota(jnp.int32, (CHUNK, MXU_DIM), 1) & (HEAD_DIM - 1)
    strict_lower, incl_lower = cr > cj, cr >= cj
    eye_c = (cr == cj).astype(F32)
    ti = lax.broadcasted_iota(jnp.int32, (2 * CHUNK, CHUNK), 0)
    tj = lax.broadcasted_iota(jnp.int32, (2 * CHUNK, CHUNK), 1)
    tri = (((ti < CHUNK) & (ti >= tj)) | ((ti >= CHUNK) & (ti - CHUNK > tj))).astype(BF16)

    heads_per_lane_tile = LANES // HEAD_DIM
    diag_tiles = [(slice(h * HEAD_DIM, (h + 1) * HEAD_DIM),
                   slice((h // heads_per_lane_tile) * LANES, (h // heads_per_lane_tile + 1) * LANES))
                  for h in range(hpt)]

    def state_bf16(ref, c):
        zeros = jnp.zeros((HEAD_DIM, LANES), BF16)
        rows = []
        for rs, ls in diag_tiles:
            parts = [zeros] * (MXU_DIM // LANES)
            parts[ls.start // LANES] = ref[c, rs, ls].astype(BF16)
            rows.append(jnp.concatenate(parts, axis=1))
        return jnp.concatenate(rows, axis=0)

    def block_diag(x):
        return jnp.where(same_head, jnp.concatenate([x] * hpt, axis=0), jnp.zeros((), x.dtype))

    def group_sum(x):
        return _group_sum(x, ones_blk)

    kku = full(kku_ref)
    kkn = kku * lax.rsqrt(jnp.maximum(group_sum(kku * kku), 1e-24))
    kkn_ref[...] = kkn
    p_ref[...] = -(kkn * full(a_ref))

    def prepare(sub, b):
        rows = pl.ds(b * blk_rows + sub * CHUNK, CHUNK)
        part = (b, slice(sub * CHUNK, (sub + 1) * CHUNK))
        cums = _dot(tri, lwh_ref[part]) + _dot(tri, lwl_ref[part])
        cum = cums[0:CHUNK]
        e_in = jnp.exp(cum)
        e_ex = jnp.exp(cums[CHUNK:])
        e_neg = jnp.exp(-cum)
        e_last = e_in[CHUNK - 1:CHUNK, :]
        qt_all = kkn_ref[rows, :] * e_ex
        rt_all = r_ref[part].astype(F32) * e_in
        pt_all = p_ref[rows, :] * e_neg
        kt_all = km_ref[part].astype(F32) * e_neg
        ph_all = pt_all * e_last
        kh_all = kt_all * e_last
        vr_all = vr_ref[part].astype(BF16)
        q_all = q_ref[part]
        k_all = k_ref[part]
        v_all = v_ref[part].astype(BF16)
        qd_all = (q_all.astype(F32) * qdec_ref[...]).astype(BF16)
        kd_all = (k_all.astype(F32) * kdec_ref[...]).astype(BF16)
        qt_b, rt_b = qt_all.astype(BF16), rt_all.astype(BF16)
        pt_b, kt_b = pt_all.astype(BF16), kt_all.astype(BF16)
        q_b, k_b = q_all.astype(BF16), k_all.astype(BF16)

        for t in range(n_tiles):
            c = sub * n_states + b * n_tiles + t
            sl = slice(t * MXU_DIM, (t + 1) * MXU_DIM)
            qr = jnp.concatenate([qt_b[:, sl], rt_b[:, sl]], axis=0)
            qr_ref[c] = qr
            g_p = _dot_nt(qr, block_diag(pt_b[:, sl]))
            g_k = _dot_nt(qr, block_diag(kt_b[:, sl]))
            a_qp = jnp.where(strict_lower, g_p[0:CHUNK], 0.0)
            pw_ref[c] = a_qp.astype(BF16)
            tinv_ref[c] = a_qp + eye_c
            brp_ref[c] = jnp.where(incl_lower, g_p[CHUNK:], 0.0).astype(BF16)
            ab = jnp.concatenate([jnp.where(strict_lower, g_k[0:CHUNK], 0.0),
                                  jnp.where(incl_lower, g_k[CHUNK:], 0.0)], axis=0).astype(BF16)
            av_ref[c] = _dot(ab, block_diag(vr_all[:, sl]))
            pkt_ref[c] = jnp.transpose(jnp.concatenate([ph_all[:, sl], kh_all[:, sl]], axis=0)).astype(BF16)
            dec_ref[c] = jnp.transpose(jnp.broadcast_to(e_last[:, sl], (LANES, MXU_DIM)))
            scores = (_dot_nt(q_b[:, sl], block_diag(k_b[:, sl])) * dmask_ref[t]).astype(BF16)
            inner_ref[c] = _dot(scores, block_diag(v_all[:, sl]))
            qd_ref[c] = qd_all[:, sl]
            kv = _dot_tn(kd_all[:, sl], v_all[:, sl])
            for rs, ls in diag_tiles:
                kv_ref[c, rs, ls] = jnp.where(same_head[rs, ls], kv[rs, ls], 0.0)

    for sub in range(n_sub):
        for b in range(bsz):
            prepare(sub, b)

    inv_hd = 1.0 / HEAD_DIM

    def group_norm(y, eps):
        mu = group_sum(y) * inv_hd
        yc = y - mu
        var = group_sum(yc * yc) * inv_hd
        return yc * lax.rsqrt(var + eps)

    def norm_retention():
        yret_ref[...] = group_norm(yret_ref[...], RET_GN_EPS) * rgn_g_ref[...] + rgn_b_ref[...]

    def norm_rwkv():
        bonus = group_sum(full(rl_ref) * full(kml_ref) * rk_ref[...]) * full(vrl_ref)
        yrw_ref[...] = group_norm(yrw_ref[...], RWKV_GN_EPS) * wgn_g_ref[...] + wgn_b_ref[...] + bonus

    def gate_and_store(b_lo, b_hi):
        for b in range(b_lo, b_hi):
            rows = pl.ds(b * blk_rows, blk_rows)
            g = g_ref[b].astype(F32)
            o_ref[b, :, 0:width] = (g * jax.nn.sigmoid(g) * yret_ref[rows, :]).astype(o_ref.dtype)
            o_ref[b, :, width:2 * width] = (yrw_ref[rows, :] * gate_ref[b].astype(F32)).astype(o_ref.dtype)

    filler = [norm_retention, norm_rwkv, functools.partial(gate_and_store, 0, bsz // 2),
              functools.partial(gate_and_store, bsz // 2, bsz)]

    for c in range(n_chains):
        pwb = pw_ref[c]
        pw_ref[c] = _dot(pwb, block_diag(pwb)).astype(BF16)
    for rnd in range(4):
        for c in range(n_chains):
            pwb, t_acc = pw_ref[c], tinv_ref[c]
            res = _dot(jnp.concatenate([pwb, t_acc.astype(BF16)], axis=0), block_diag(pwb))
            pw_ref[c] = res[0:CHUNK].astype(BF16)
            tinv_ref[c] = t_acc + res[CHUNK:]
        filler[rnd]()
    for c in range(n_chains):
        t_acc = tinv_ref[c]
        tinv_ref[c] = t_acc + _dot(t_acc.astype(BF16), block_diag(pw_ref[c]))

    for sub in range(n_sub):
        chains = [(sub * n_states + st, st, st // n_tiles, (st // n_tiles, slice(sub * CHUNK, (sub + 1) * CHUNK)),
                   pl.ds((st // n_tiles) * blk_rows + sub * CHUNK, CHUNK),
                   slice((st % n_tiles) * MXU_DIM, (st % n_tiles + 1) * MXU_DIM)) for st in range(n_states)]
        qrm = {c: _dot(qr_ref[c], state_bf16(srw_ref, st)) for c, st, _, _, _, _ in chains}
        u = {c: _dot(tinv_ref[c].astype(BF16),
                     block_diag((qrm[c][0:CHUNK] + av_ref[c, 0:CHUNK, :]).astype(BF16)))
             for c, _, _, _, _, _ in chains}
        for c, st, b, part, rows, sl in chains:
            ub = u[c].astype(BF16)
            yrw_ref[rows, sl] = qrm[c][CHUNK:] + av_ref[c, CHUNK:, :] + _dot(brp_ref[c], block_diag(ub))
            dec = dec_ref[c]
            upd = _dot(pkt_ref[c], jnp.concatenate([ub, vr_ref[part][:, sl]], axis=0))
            yret_ref[rows, sl] = inner_ref[c] + _dot(qd_ref[c], state_bf16(sret_ref, st))
            for rs, ls in diag_tiles:
                srw_ref[st, rs, ls] = (srw_ref[st, rs, ls] * dec[rs, :]
                                       + jnp.where(same_head[rs, ls], upd[rs, ls], 0.0))
                sret_ref[st, rs, ls] = (sret_ref[st, rs, ls] * cdec_ref[st % n_tiles, rs, ls]
                                        + kv_ref[c, rs, ls])


def _mix_call(streams, dmask, qdec, kdec, cdec, rgn_g, rgn_b, wgn_g, wgn_b, rk):
    q, k, v, g, r, km, vr, lwh, lwl, a, kku, gate = streams
    bsz, s, width = q.shape
    n_tiles = width // MXU_DIM
    blk_rows = min(CHUNKS_PER_STEP * CHUNK, s)
    n_steps = s // blk_rows
    cur = pl.BlockSpec((bsz, blk_rows, width), lambda j: (0, jnp.minimum(j, n_steps - 1), 0))
    prev_idx = lambda j: (0, jnp.maximum(j - 1, 0), 0)
    prev = pl.BlockSpec((bsz, blk_rows, width), prev_idx)
    current_streams = [q, k, v, r, km, vr, lwh, lwl, a, kku]
    previous_streams = [g, gate, r, km, vr]
    consts = [dmask, qdec, kdec, cdec, rgn_g, rgn_b, wgn_g, wgn_b, rk]
    n_states = bsz * n_tiles
    n_chains = (blk_rows // CHUNK) * n_states
    tc = bsz * blk_rows
    state = (n_states, MXU_DIM, MXU_DIM)
    tile = (n_chains, MXU_DIM, MXU_DIM)
    flat = (n_chains, CHUNK, MXU_DIM)
    pair = (n_chains, 2 * CHUNK, MXU_DIM)
    return pl.pallas_call(
        _mix_kernel,
        grid=(n_steps + 1,),
        in_specs=([cur] * len(current_streams) + [prev] * len(previous_streams)
                  + [_const_spec(c.shape) for c in consts]),
        out_specs=pl.BlockSpec((bsz, blk_rows, 2 * width), prev_idx),
        out_shape=jax.ShapeDtypeStruct((bsz, s, 2 * width), BF16),
        scratch_shapes=[pltpu.VMEM(state, F32),
                        pltpu.VMEM(state, F32),
                        pltpu.VMEM((tc, width), F32), pltpu.VMEM((tc, width), F32),
                        pltpu.VMEM((tc, width), F32), pltpu.VMEM((tc, width), F32),
                        pltpu.VMEM(flat, BF16), pltpu.VMEM(flat, F32), pltpu.VMEM(flat, BF16),
                        pltpu.VMEM(pair, F32),
                        pltpu.VMEM((n_chains, MXU_DIM, 2 * CHUNK), BF16),
                        pltpu.VMEM((n_chains, MXU_DIM, LANES), F32),
                        pltpu.VMEM(pair, BF16),
                        pltpu.VMEM(flat, F32), pltpu.VMEM(flat, BF16), pltpu.VMEM(tile, F32)],
        compiler_params=pltpu.CompilerParams(dimension_semantics=("arbitrary",),
                                             vmem_limit_bytes=VMEM_LIMIT_BYTES),
        name="mix_recurrences",
    )(*current_streams, *previous_streams, *consts)


def _retention_tables(n_heads):
    h = jnp.arange(n_heads, dtype=F32)
    log_gamma = jnp.log1p(-jnp.exp2(-5.0 - h))
    lg_lane = jnp.repeat(log_gamma, HEAD_DIM)[None, :]
    idx = jnp.arange(CHUNK, dtype=F32)[:, None]
    qdec = jnp.exp((idx + 1.0) * lg_lane)
    kdec = jnp.exp((CHUNK - 1.0 - idx) * lg_lane)
    n_tiles = n_heads // HEADS_PER_TILE
    cdec = jnp.broadcast_to(jnp.exp(CHUNK * lg_lane).reshape(n_tiles, MXU_DIM, 1),
                            (n_tiles, MXU_DIM, MXU_DIM))
    rel = idx - (jnp.arange(MXU_DIM) % HEAD_DIM).astype(F32)[None, :]
    lg_tiles = lg_lane.reshape(n_tiles, 1, MXU_DIM)
    dmask = jnp.where((rel >= 0)[None], jnp.exp(jnp.where(rel >= 0, rel, 0.0)[None] * lg_tiles), 0.0)
    return dmask, qdec, kdec, cdec


def _rotary_tables(s):
    pos = jnp.arange(s, dtype=F32)
    inv_freq = ROPE_BASE ** (-jnp.arange(0, HEAD_DIM, 2, dtype=F32) / HEAD_DIM)
    ang = pos[:, None] * inv_freq[None, :]
    cos, sin = jnp.cos(ang), jnp.sin(ang)
    reps = LANES // HEAD_DIM
    cos_t = jnp.tile(jnp.concatenate([cos, cos], axis=1), (1, reps))
    sin_t = jnp.tile(jnp.concatenate([-sin, sin], axis=1), (1, reps))
    return cos_t, sin_t


def kernel(x, p, ffn1_w_gu, ffn1_w_down, ln1_g, ln1_b, w_in, ret_gn_g, ret_gn_b, rw_mu, rw_w0, rw_w_up, rw_a0, rw_a_up, rw_g_up, rw_k_k, rw_k_a, rw_r_k, rw_gn_g, rw_gn_b, w_out, ln2_g, ln2_b, ffn2_w_gu, ffn2_w_down, ln3_g, ln3_b, ple_w_proj, ple_w_gate, ple_b_gate):
    bsz, s, d = x.shape
    depth = ffn1_w_gu.shape[0]
    alpha = (2.0 * depth) ** 0.25
    width = rw_w0.shape[1]
    n_heads = width // HEAD_DIM
    ret_cols = 4 * width
    lora = DECAY_LORA + AAA_LORA + GATE_LORA
    lora_pad = -(-lora // LANES) * LANES

    dmask, qdec, kdec, cdec = _retention_tables(n_heads)
    cos_t, sin_t = _rotary_tables(s)
    row = lambda v: v.reshape(1, -1)

    h = x.reshape(bsz * s, d)
    for i in range(depth):
        h = _ffn_call(h, ffn1_w_gu[i], ffn1_w_down[i],
                      row(ln1_g[i]), row(ln1_b[i]), alpha)

        mu = row(jnp.pad(rw_mu[i], (0, lora_pad - lora)))
        n_da = DECAY_LORA + AAA_LORA
        wda = jnp.zeros((n_da, 2 * width), BF16)
        wda = wda.at[:DECAY_LORA, :width].set(rw_w_up[i].astype(BF16))
        wda = wda.at[DECAY_LORA:, width:].set(rw_a_up[i].astype(BF16))
        wg = jnp.pad(rw_g_up[i].astype(BF16), ((0, lora_pad - lora), (0, 0)))
        streams = _inproj_call(h.reshape(bsz, s, d), jnp.swapaxes(w_in[i], 0, 1), wda, wg, cos_t, sin_t, mu,
                               row(rw_w0[i]), row(rw_a0[i]), row(rw_k_k[i]), row(rw_k_a[i]))
        mixed = _mix_call(streams, dmask, qdec, kdec, cdec, row(ret_gn_g[i]), row(ret_gn_b[i]),
                          row(rw_gn_g[i]), row(rw_gn_b[i]), row(rw_r_k[i]))
        pre = (mixed.reshape(bsz * s, 2 * width), w_out[i].astype(BF16), row(ln2_g[i]), row(ln2_b[i]))
        ple = (p[i].reshape(bsz * s, -1), ple_w_proj[i].astype(BF16), ple_w_gate[i].astype(BF16),
               row(ple_b_gate[i]))
        h = _ffn_call(h, ffn2_w_gu[i], ffn2_w_down[i],
                      row(ln3_g[i]), row(ln3_b[i]), alpha, pre=pre, ple=ple)
    return h.reshape(bsz, s, d)
```

```python
import functools
import math

import jax
import jax.numpy as jnp
from jax import lax
from jax.experimental import pallas as pl
from jax.experimental.pallas import tpu as pltpu

HEAD_DIM = 64
ROPE_BASE = 10000.0
DECAY_LORA = 64
AAA_LORA = 64
GATE_LORA = 160
LN_EPS = 1e-5
RET_GN_EPS = 1e-5
RWKV_GN_EPS = 64e-5
DECAY_SCALE = math.exp(-0.5)
HEAD_SHIFT = HEAD_DIM.bit_length() - 1

LANES = 128
MXU_DIM = 256
HEADS_PER_TILE = MXU_DIM // HEAD_DIM
CHUNK = 64
CHUNKS_PER_STEP = 2
VMEM_LIMIT_BYTES = 56 * 1024 * 1024

F32 = jnp.float32
BF16 = jnp.bfloat16


def _dot(a, b):
    return jnp.dot(a, b, preferred_element_type=F32)


def _dot_nt(a, b):
    return lax.dot_general(a, b, (((1,), (1,)), ((), ())), preferred_element_type=F32)


def _dot_tn(a, b):
    return lax.dot_general(a, b, (((0,), (0,)), ((), ())), preferred_element_type=F32)


def _layer_norm(y, g, b):
    mu = jnp.mean(y, axis=-1, keepdims=True)
    yc = y - mu
    var = jnp.mean(yc * yc, axis=-1, keepdims=True)
    return yc * lax.rsqrt(var + LN_EPS) * g + b


def _const_spec(shape):
    nd = len(shape)
    return pl.BlockSpec(shape, lambda *_: (0,) * nd, pipeline_mode=pl.Buffered(1))


def _ffn_kernel(*refs, alpha, fc, with_pre, with_ple):
    refs = list(refs)
    x_ref = refs.pop(0)
    if with_pre:
        m_ref, wo_ref, gp_ref, bp_ref = refs[:4]
        refs = refs[4:]
    wgu_ref, wd_ref, g_ref, b_ref = refs[:4]
    refs = refs[4:]
    if with_ple:
        p_ref, wp_ref, wpg_ref, bpg_ref = refs[:4]
        refs = refs[4:]
    o_ref, act_ref = refs[:2]
    d_ff = wd_ref.shape[0]
    if with_pre:
        xs_ref = refs[2]
        xs_ref[...] = _layer_norm(alpha * x_ref[...] + _dot(m_ref[...], wo_ref[...]), gp_ref[...], bp_ref[...])
    else:
        xs_ref = x_ref
    xb = xs_ref[...].astype(BF16)
    for c in range(d_ff // fc):
        hg = _dot(xb, wgu_ref[:, c * fc:(c + 1) * fc].astype(BF16))
        hu = _dot(xb, wgu_ref[:, d_ff + c * fc:d_ff + (c + 1) * fc].astype(BF16))
        act_ref[:, c * fc:(c + 1) * fc] = (hg * jax.nn.sigmoid(hg) * hu).astype(BF16)
    down = _dot(act_ref[...], wd_ref[...].astype(BF16))
    y = _layer_norm(alpha * xs_ref[...] + 0.5 * down, g_ref[...], b_ref[...])
    if with_ple:
        gate = jax.nn.sigmoid(_dot(y.astype(BF16), wpg_ref[...]) + bpg_ref[...])
        y = y + gate * _dot(p_ref[...].astype(BF16), wp_ref[...])
    o_ref[...] = y


def _ffn_call(x2d, wgu, wd, ln_g, ln_b, alpha, pre=None, ple=None, tm=512, fc=MXU_DIM):
    n, d = x2d.shape
    d_ff = wd.shape[0]
    tm = min(tm, n)
    row = lambda i: (i, 0)
    in_specs = [pl.BlockSpec((tm, d), row)]
    args = [x2d]
    if pre is not None:
        m2d, wo, gp, bp = pre
        in_specs += [pl.BlockSpec((tm, m2d.shape[1]), row), _const_spec(wo.shape),
                     _const_spec((1, d)), _const_spec((1, d))]
        args += [m2d, wo, gp, bp]
    in_specs += [_const_spec(wgu.shape), _const_spec(wd.shape), _const_spec((1, d)), _const_spec((1, d))]
    args += [wgu, wd, ln_g, ln_b]
    if ple is not None:
        p2d, wp, wpg, bpg = ple
        in_specs += [pl.BlockSpec((tm, p2d.shape[1]), row), _const_spec(wp.shape),
                     _const_spec(wpg.shape), _const_spec((1, d))]
        args += [p2d, wp, wpg, bpg]
    return pl.pallas_call(
        functools.partial(_ffn_kernel, alpha=alpha, fc=fc, with_pre=pre is not None,
                          with_ple=ple is not None),
        grid=(n // tm,),
        in_specs=in_specs,
        out_specs=pl.BlockSpec((tm, d), row),
        out_shape=jax.ShapeDtypeStruct((n, d), F32),
        scratch_shapes=[pltpu.VMEM((tm, d_ff), BF16)] + ([pltpu.VMEM((tm, d), F32)] if pre is not None else []),
        compiler_params=pltpu.CompilerParams(dimension_semantics=("arbitrary",),
                                             vmem_limit_bytes=VMEM_LIMIT_BYTES),
        name="ffn_ln",
    )(*args)


def _group_sum(x, ones_blk):
    outs = [_dot(x[:, t * MXU_DIM:(t + 1) * MXU_DIM].astype(BF16), ones_blk)
            for t in range(x.shape[1] // MXU_DIM)]
    return jnp.concatenate(outs, axis=1)


def _inproj_kernel(x_ref, win_ref, wda_ref, wg_ref, cos_ref, sin_ref, mu_ref, w0_ref, a0_ref,
                   kk_ref, ka_ref,
                   q_ref, k_ref, v_ref, g_ref, r_ref, km_ref, vr_ref, lwh_ref, lwl_ref, a_ref, kku_ref,
                   gate_ref, carry_ref, *, width):
    tm = x_ref.shape[0]

    @pl.when(pl.program_id(1) == 0)
    def _():
        carry_ref[...] = jnp.zeros_like(carry_ref)

    xb = x_ref[...].astype(BF16)

    lane = lax.broadcasted_iota(jnp.int32, (tm, width), 1)
    first_half = (lane & (HEAD_DIM // 2)) == 0
    reps = width // cos_ref.shape[1]
    cos = jnp.concatenate([cos_ref[...]] * reps, axis=1)
    sin = jnp.concatenate([sin_ref[...]] * reps, axis=1)

    def rotary(t):
        swapped = jnp.where(first_half, pltpu.roll(t, width - HEAD_DIM // 2, 1),
                            pltpu.roll(t, HEAD_DIM // 2, 1))
        return t * cos + swapped * sin

    def put(ref, val):
        ref[...] = val.astype(ref.dtype)

    ret_cols = 4 * width
    n_lora = win_ref.shape[0] - ret_cols - 3 * width
    lora_pad = mu_ref.shape[1] - 3 * width

    def project(lo_col, hi_col, pad=0):
        w = win_ref[lo_col:hi_col, :].astype(BF16)
        if pad:
            w = jnp.concatenate([w, jnp.zeros((pad, w.shape[1]), BF16)], axis=0)
        return _dot_nt(xb, w)

    def shifted(lo_col, hi_col, z):
        first = carry_ref[0:1, lo_col:hi_col]
        carry_ref[0:1, lo_col:hi_col] = z[tm - 1:tm, :]
        rowid = lax.broadcasted_iota(jnp.int32, z.shape, 0)
        prev = jnp.where(rowid == 0, first, pltpu.roll(z, 1, 0))
        return z + (prev - z) * mu_ref[:, lo_col:hi_col]

    lo = shifted(3 * width, 3 * width + lora_pad,
                 project(ret_cols + 3 * width, ret_cols + 3 * width + n_lora, pad=lora_pad - n_lora))
    n_da = DECAY_LORA + AAA_LORA
    lo_da = lo[:, 0:n_da]
    ll = lax.broadcasted_iota(jnp.int32, lo_da.shape, 1)
    act_da = jnp.where(ll < DECAY_LORA, jnp.tanh(lo_da), lo_da).astype(BF16)
    up = _dot(act_da, wda_ref[...])
    put(gate_ref, _dot(jax.nn.sigmoid(lo[:, n_da:]).astype(BF16), wg_ref[...]))
    lw = -DECAY_SCALE * jax.nn.sigmoid(w0_ref[...] + up[:, 0 * width:1 * width])
    lw_hi, lw_lo = _split_hi_lo(lw)
    lwh_ref[...] = lw_hi
    lwl_ref[...] = lw_lo
    a = jax.nn.sigmoid(a0_ref[...] + up[:, 1 * width:2 * width])
    put(a_ref, a)
    kr = shifted(1 * width, 2 * width, project(ret_cols + 1 * width, ret_cols + 2 * width))
    put(kku_ref, kr * kk_ref[...])
    put(km_ref, kr * (1.0 + (a - 1.0) * ka_ref[...]))
    put(r_ref, shifted(0 * width, 1 * width, project(ret_cols + 0 * width, ret_cols + 1 * width)))
    put(vr_ref, shifted(2 * width, 3 * width, project(ret_cols + 2 * width, ret_cols + 3 * width)))

    put(q_ref, rotary(project(0 * width, 1 * width)))
    put(k_ref, rotary(project(1 * width, 2 * width)) * (HEAD_DIM ** -0.5))
    put(v_ref, project(2 * width, 3 * width))
    put(g_ref, project(3 * width, 4 * width))


def _inproj_call(x3d, w_in_t, wda, wg, cos, sin, mu, w0, a0, k_k, k_a, tm=512):
    bsz, s, d = x3d.shape
    width = w0.shape[1]
    tm = min(tm, s)
    blk = lambda b, j: (b, j, 0)
    out_spec = pl.BlockSpec((None, tm, width), blk)
    n_out = 12
    out_sds = [jax.ShapeDtypeStruct((bsz, s, width), BF16)] * n_out
    return pl.pallas_call(
        functools.partial(_inproj_kernel, width=width),
        grid=(bsz, s // tm),
        in_specs=[pl.BlockSpec((None, tm, d), blk), _const_spec(w_in_t.shape),
                  _const_spec(wda.shape), _const_spec(wg.shape),
                  pl.BlockSpec((tm, cos.shape[1]), lambda b, j: (j, 0)),
                  pl.BlockSpec((tm, sin.shape[1]), lambda b, j: (j, 0)),
                  _const_spec(mu.shape), _const_spec(w0.shape), _const_spec(a0.shape),
                  _const_spec(k_k.shape), _const_spec(k_a.shape)],
        out_specs=[out_spec] * n_out,
        out_shape=out_sds,
        scratch_shapes=[pltpu.VMEM((8, mu.shape[1]), F32)],
        compiler_params=pltpu.CompilerParams(dimension_semantics=("arbitrary", "arbitrary"),
                                             vmem_limit_bytes=VMEM_LIMIT_BYTES),
        name="inproj_prep",
    )(x3d, w_in_t, wda, wg, cos, sin, mu, w0, a0, k_k, k_a)


def _split_hi_lo(x):
    hi = x.astype(BF16)
    lo = (x - hi.astype(F32)).astype(BF16)
    return hi, lo


def _mix_kernel(q_ref, k_ref, v_ref, r_ref, km_ref, vr_ref, lwh_ref, lwl_ref, a_ref, kku_ref,
                g_ref, gate_ref, rl_ref, kml_ref, vrl_ref,
                dmask_ref, qdec_ref, kdec_ref, cdec_ref, rgn_g_ref, rgn_b_ref, wgn_g_ref, wgn_b_ref,
                rk_ref, o_ref,
                sret_ref, srw_ref, kkn_ref, p_ref, yret_ref, yrw_ref,
                pw_ref, tinv_ref, brp_ref, av_ref, pkt_ref, dec_ref, qr_ref, inner_ref, qd_ref, kv_ref):
    bsz, blk_rows, width = q_ref.shape
    n_sub = blk_rows // CHUNK
    n_tiles = width // MXU_DIM
    n_states = bsz * n_tiles
    n_chains = n_sub * n_states
    hpt = HEADS_PER_TILE
    n_rows = bsz * blk_rows

    def full(ref):
        return ref[...].reshape(n_rows, ref.shape[-1]).astype(F32)

    @pl.when(pl.program_id(0) == 0)
    def _():
        sret_ref[...] = jnp.zeros_like(sret_ref)
        srw_ref[...] = jnp.zeros_like(srw_ref)
        yret_ref[...] = jnp.zeros_like(yret_ref)
        yrw_ref[...] = jnp.zeros_like(yrw_ref)

    ri = lax.broadcasted_iota(jnp.int32, (MXU_DIM, MXU_DIM), 0)
    ci = lax.broadcasted_iota(jnp.int32, (MXU_DIM, MXU_DIM), 1)
    same_head = (ri >> HEAD_SHIFT) == (ci >> HEAD_SHIFT)
    ones_blk = same_head.astype(BF16)
    cr = lax.broadcasted_iota(jnp.int32, (CHUNK, MXU_DIM), 0)
    cj = lax.broadcasted_iota(jnp.int32, (CHUNK, MXU_DIM), 1) & (HEAD_DIM - 1)
    strict_lower, incl_lower = cr > cj, cr >= cj
    eye_c = (cr == cj).astype(F32)
    ti = lax.broadcasted_iota(jnp.int32, (2 * CHUNK, CHUNK), 0)
    tj = lax.broadcasted_iota(jnp.int32, (2 * CHUNK, CHUNK), 1)
    tri = (((ti < CHUNK) & (ti >= tj)) | ((ti >= CHUNK) & (ti - CHUNK > tj))).astype(BF16)

    heads_per_lane_tile = LANES // HEAD_DIM
    diag_tiles = [(slice(h * HEAD_DIM, (h + 1) * HEAD_DIM),
                   slice((h // heads_per_lane_tile) * LANES, (h // heads_per_lane_tile + 1) * LANES))
                  for h in range(hpt)]

    def state_bf16(ref, c):
        zeros = jnp.zeros((HEAD_DIM, LANES), BF16)
        rows = []
        for rs, ls in diag_tiles:
            parts = [zeros] * (MXU_DIM // LANES)
            parts[ls.start // LANES] = ref[c, rs, ls].astype(BF16)
            rows.append(jnp.concatenate(parts, axis=1))
        return jnp.concatenate(rows, axis=0)

    def block_diag(x):
        return jnp.where(same_head, jnp.concatenate([x] * hpt, axis=0), jnp.zeros((), x.dtype))

    def group_sum(x):
        return _group_sum(x, ones_blk)

    kku = full(kku_ref)
    kkn = kku * lax.rsqrt(jnp.maximum(group_sum(kku * kku), 1e-24))
    kkn_ref[...] = kkn
    p_ref[...] = -(kkn * full(a_ref))

    def prepare(sub, b):
        rows = pl.ds(b * blk_rows + sub * CHUNK, CHUNK)
        part = (b, slice(sub * CHUNK, (sub + 1) * CHUNK))
        cums = _dot(tri, lwh_ref[part]) + _dot(tri, lwl_ref[part])
        cum = cums[0:CHUNK]
        e_in = jnp.exp(cum)
        e_ex = jnp.exp(cums[CHUNK:])
        e_neg = jnp.exp(-cum)
        e_last = e_in[CHUNK - 1:CHUNK, :]
        qt_all = kkn_ref[rows, :] * e_ex
        rt_all = r_ref[part].astype(F32) * e_in
        pt_all = p_ref[rows, :] * e_neg
        kt_all = km_ref[part].astype(F32) * e_neg
        ph_all = pt_all * e_last
        kh_all = kt_all * e_last
        vr_all = vr_ref[part].astype(BF16)
        q_all = q_ref[part]
        k_all = k_ref[part]
        v_all = v_ref[part].astype(BF16)
        qd_all = (q_all.astype(F32) * qdec_ref[...]).astype(BF16)
        kd_all = (k_all.astype(F32) * kdec_ref[...]).astype(BF16)
        qt_b, rt_b = qt_all.astype(BF16), rt_all.astype(BF16)
        pt_b, kt_b = pt_all.astype(BF16), kt_all.astype(BF16)
        q_b, k_b = q_all.astype(BF16), k_all.astype(BF16)

        for t in range(n_tiles):
            c = sub * n_states + b * n_tiles + t
            sl = slice(t * MXU_DIM, (t + 1) * MXU_DIM)
            qr = jnp.concatenate([qt_b[:, sl], rt_b[:, sl]], axis=0)
            qr_ref[c] = qr
            g_p = _dot_nt(qr, block_diag(pt_b[:, sl]))
            g_k = _dot_nt(qr, block_diag(kt_b[:, sl]))
            a_qp = jnp.where(strict_lower, g_p[0:CHUNK], 0.0)
            pw_ref[c] = a_qp.astype(BF16)
            tinv_ref[c] = a_qp + eye_c
            brp_ref[c] = jnp.where(incl_lower, g_p[CHUNK:], 0.0).astype(BF16)
            ab = jnp.concatenate([jnp.where(strict_lower, g_k[0:CHUNK], 0.0),
                                  jnp.where(incl_lower, g_k[CHUNK:], 0.0)], axis=0).astype(BF16)
            av_ref[c] = _dot(ab, block_diag(vr_all[:, sl]))
            pkt_ref[c] = jnp.transpose(jnp.concatenate([ph_all[:, sl], kh_all[:, sl]], axis=0)).astype(BF16)
            dec_ref[c] = jnp.transpose(jnp.broadcast_to(e_last[:, sl], (LANES, MXU_DIM)))
            scores = (_dot_nt(q_b[:, sl], block_diag(k_b[:, sl])) * dmask_ref[t]).astype(BF16)
            inner_ref[c] = _dot(scores, block_diag(v_all[:, sl]))
            qd_ref[c] = qd_all[:, sl]
            kv = _dot_tn(kd_all[:, sl], v_all[:, sl])
            for rs, ls in diag_tiles:
                kv_ref[c, rs, ls] = jnp.where(same_head[rs, ls], kv[rs, ls], 0.0)

    for sub in range(n_sub):
        for b in range(bsz):
            prepare(sub, b)

    inv_hd = 1.0 / HEAD_DIM

    def group_norm(y, eps):
        mu = group_sum(y) * inv_hd
        yc = y - mu
        var = group_sum(yc * yc) * inv_hd
        return yc * lax.rsqrt(var + eps)

    def norm_retention():
        yret_ref[...] = group_norm(yret_ref[...], RET_GN_EPS) * rgn_g_ref[...] + rgn_b_ref[...]

    def norm_rwkv():
        bonus = group_sum(full(rl_ref) * full(kml_ref) * rk_ref[...]) * full(vrl_ref)
        yrw_ref[...] = group_norm(yrw_ref[...], RWKV_GN_EPS) * wgn_g_ref[...] + wgn_b_ref[...] + bonus

    def gate_and_store(b_lo, b_hi):
        for b in range(b_lo, b_hi):
            rows = pl.ds(b * blk_rows, blk_rows)
            g = g_ref[b].astype(F32)
            o_ref[b, :, 0:width] = (g * jax.nn.sigmoid(g) * yret_ref[rows, :]).astype(o_ref.dtype)
            o_ref[b, :, width:2 * width] = (yrw_ref[rows, :] * gate_ref[b].astype(F32)).astype(o_ref.dtype)

    filler = [norm_retention, norm_rwkv, functools.partial(gate_and_store, 0, bsz // 2),
              functools.partial(gate_and_store, bsz // 2, bsz)]

    for c in range(n_chains):
        pwb = pw_ref[c]
        pw_ref[c] = _dot(pwb, block_diag(pwb)).astype(BF16)
    for rnd in range(4):
        for c in range(n_chains):
            pwb, t_acc = pw_ref[c], tinv_ref[c]
            res = _dot(jnp.concatenate([pwb, t_acc.astype(BF16)], axis=0), block_diag(pwb))
            pw_ref[c] = res[0:CHUNK].astype(BF16)
            tinv_ref[c] = t_acc + res[CHUNK:]
        filler[rnd]()
    for c in range(n_chains):
        t_acc = tinv_ref[c]
        tinv_ref[c] = t_acc + _dot(t_acc.astype(BF16), block_diag(pw_ref[c]))

    for sub in range(n_sub):
        chains = [(sub * n_states + st, st, st // n_tiles, (st // n_tiles, slice(sub * CHUNK, (sub + 1) * CHUNK)),
                   pl.ds((st // n_tiles) * blk_rows + sub * CHUNK, CHUNK),
                   slice((st % n_tiles) * MXU_DIM, (st % n_tiles + 1) * MXU_DIM)) for st in range(n_states)]
        qrm = {c: _dot(qr_ref[c], state_bf16(srw_ref, st)) for c, st, _, _, _, _ in chains}
        u = {c: _dot(tinv_ref[c].astype(BF16),
                     block_diag((qrm[c][0:CHUNK] + av_ref[c, 0:CHUNK, :]).astype(BF16)))
             for c, _, _, _, _, _ in chains}
        for c, st, b, part, rows, sl in chains:
            ub = u[c].astype(BF16)
            yrw_ref[rows, sl] = qrm[c][CHUNK:] + av_ref[c, CHUNK:, :] + _dot(brp_ref[c], block_diag(ub))
            dec = dec_ref[c]
            upd = _dot(pkt_ref[c], jnp.concatenate([ub, vr_ref[part][:, sl]], axis=0))
            yret_ref[rows, sl] = inner_ref[c] + _dot(qd_ref[c], state_bf16(sret_ref, st))
            for rs, ls in diag_tiles:
                srw_ref[st, rs, ls] = (srw_ref[st, rs, ls] * dec[rs, :]
                                       + jnp.where(same_head[rs, ls], upd[rs, ls], 0.0))
                sret_ref[st, rs, ls] = (sret_ref[st, rs, ls] * cdec_ref[st % n_tiles, rs, ls]
                                        + kv_ref[c, rs, ls])


def _mix_call(streams, dmask, qdec, kdec, cdec, rgn_g, rgn_b, wgn_g, wgn_b, rk):
    q, k, v, g, r, km, vr, lwh, lwl, a, kku, gate = streams
    bsz, s, width = q.shape
    n_tiles = width // MXU_DIM
    blk_rows = min(CHUNKS_PER_STEP * CHUNK, s)
    n_steps = s // blk_rows
    cur = pl.BlockSpec((bsz, blk_rows, width), lambda j: (0, jnp.minimum(j, n_steps - 1), 0))
    prev_idx = lambda j: (0, jnp.maximum(j - 1, 0), 0)
    prev = pl.BlockSpec((bsz, blk_rows, width), prev_idx)
    current_streams = [q, k, v, r, km, vr, lwh, lwl, a, kku]
    previous_streams = [g, gate, r, km, vr]
    consts = [dmask, qdec, kdec, cdec, rgn_g, rgn_b, wgn_g, wgn_b, rk]
    n_states = bsz * n_tiles
    n_chains = (blk_rows // CHUNK) * n_states
    tc = bsz * blk_rows
    state = (n_states, MXU_DIM, MXU_DIM)
    tile = (n_chains, MXU_DIM, MXU_DIM)
    flat = (n_chains, CHUNK, MXU_DIM)
    pair = (n_chains, 2 * CHUNK, MXU_DIM)
    return pl.pallas_call(
        _mix_kernel,
        grid=(n_steps + 1,),
        in_specs=([cur] * len(current_streams) + [prev] * len(previous_streams)
                  + [_const_spec(c.shape) for c in consts]),
        out_specs=pl.BlockSpec((bsz, blk_rows, 2 * width), prev_idx),
        out_shape=jax.ShapeDtypeStruct((bsz, s, 2 * width), BF16),
        scratch_shapes=[pltpu.VMEM(state, F32),
                        pltpu.VMEM(state, F32),
                        pltpu.VMEM((tc, width), F32), pltpu.VMEM((tc, width), F32),
                        pltpu.VMEM((tc, width), F32), pltpu.VMEM((tc, width), F32),
                        pltpu.VMEM(flat, BF16), pltpu.VMEM(flat, F32), pltpu.VMEM(flat, BF16),
                        pltpu.VMEM(pair, F32),
                        pltpu.VMEM((n_chains, MXU_DIM, 2 * CHUNK), BF16),
                        pltpu.VMEM((n_chains, MXU_DIM, LANES), F32),
                        pltpu.VMEM(pair, BF16),
                        pltpu.VMEM(flat, F32), pltpu.VMEM(flat, BF16), pltpu.VMEM(tile, F32)],
        compiler_params=pltpu.CompilerParams(dimension_semantics=("arbitrary",),
                                             vmem_limit_bytes=VMEM_LIMIT_BYTES),
        name="mix_recurrences",
    )(*current_streams, *previous_streams, *consts)


def _retention_tables(n_heads):
    h = jnp.arange(n_heads, dtype=F32)
    log_gamma = jnp.log1p(-jnp.exp2(-5.0 - h))
    lg_lane = jnp.repeat(log_gamma, HEAD_DIM)[None, :]
    idx = jnp.arange(CHUNK, dtype=F32)[:, None]
    qdec = jnp.exp((idx + 1.0) * lg_lane)
    kdec = jnp.exp((CHUNK - 1.0 - idx) * lg_lane)
    n_tiles = n_heads // HEADS_PER_TILE
    cdec = jnp.broadcast_to(jnp.exp(CHUNK * lg_lane).reshape(n_tiles, MXU_DIM, 1),
                            (n_tiles, MXU_DIM, MXU_DIM))
    rel = idx - (jnp.arange(MXU_DIM) % HEAD_DIM).astype(F32)[None, :]
    lg_tiles = lg_lane.reshape(n_tiles, 1, MXU_DIM)
    dmask = jnp.where((rel >= 0)[None], jnp.exp(jnp.where(rel >= 0, rel, 0.0)[None] * lg_tiles), 0.0)
    return dmask, qdec, kdec, cdec


def _rotary_tables(s):
    pos = jnp.arange(s, dtype=F32)
    inv_freq = ROPE_BASE ** (-jnp.arange(0, HEAD_DIM, 2, dtype=F32) / HEAD_DIM)
    ang = pos[:, None] * inv_freq[None, :]
    cos, sin = jnp.cos(ang), jnp.sin(ang)
    reps = LANES // HEAD_DIM
    cos_t = jnp.tile(jnp.concatenate([cos, cos], axis=1), (1, reps))
    sin_t = jnp.tile(jnp.concatenate([-sin, sin], axis=1), (1, reps))
    return cos_t, sin_t


def kernel(x, p, ffn1_w_gu, ffn1_w_down, ln1_g, ln1_b, w_in, ret_gn_g, ret_gn_b, rw_mu, rw_w0, rw_w_up, rw_a0, rw_a_up, rw_g_up, rw_k_k, rw_k_a, rw_r_k, rw_gn_g, rw_gn_b, w_out, ln2_g, ln2_b, ffn2_w_gu, ffn2_w_down, ln3_g, ln3_b, ple_w_proj, ple_w_gate, ple_b_gate):
    bsz, s, d = x.shape
    depth = ffn1_w_gu.shape[0]
    alpha = (2.0 * depth) ** 0.25
    width = rw_w0.shape[1]
    n_heads = width // HEAD_DIM
    ret_cols = 4 * width
    lora = DECAY_LORA + AAA_LORA + GATE_LORA
    lora_pad = -(-lora // LANES) * LANES

    dmask, qdec, kdec, cdec = _retention_tables(n_heads)
    cos_t, sin_t = _rotary_tables(s)
    row = lambda v: v.reshape(1, -1)

    h = x.reshape(bsz * s, d)
    for i in range(depth):
        h = _ffn_call(h, ffn1_w_gu[i], ffn1_w_down[i],
                      row(ln1_g[i]), row(ln1_b[i]), alpha)

        mu = row(jnp.pad(rw_mu[i], (0, lora_pad - lora)))
        n_da = DECAY_LORA + AAA_LORA
        wda = jnp.zeros((n_da, 2 * width), BF16)
        wda = wda.at[:DECAY_LORA, :width].set(rw_w_up[i].astype(BF16))
        wda = wda.at[DECAY_LORA:, width:].set(rw_a_up[i].astype(BF16))
        wg = jnp.pad(rw_g_up[i].astype(BF16), ((0, lora_pad - lora), (0, 0)))
        streams = _inproj_call(h.reshape(bsz, s, d), jnp.swapaxes(w_in[i], 0, 1), wda, wg, cos_t, sin_t, mu,
                               row(rw_w0[i]), row(rw_a0[i]), row(rw_k_k[i]), row(rw_k_a[i]))
        mixed = _mix_call(streams, dmask, qdec, kdec, cdec, row(ret_gn_g[i]), row(ret_gn_b[i]),
                          row(rw_gn_g[i]), row(rw_gn_b[i]), row(rw_r_k[i]))
        pre = (mixed.reshape(bsz * s, 2 * width), w_out[i].astype(BF16), row(ln2_g[i]), row(ln2_b[i]))
        ple = (p[i].reshape(bsz * s, -1), ple_w_proj[i].astype(BF16), ple_w_gate[i].astype(BF16),
               row(ple_b_gate[i]))
        h = _ffn_call(h, ffn2_w_gu[i], ffn2_w_down[i],
                      row(ln3_g[i]), row(ln3_b[i]), alpha, pre=pre, ple=ple)
    return h.reshape(bsz, s, d)
```

```python
import functools
import math

import jax
import jax.numpy as jnp
from jax import lax
from jax.experimental import pallas as pl
from jax.experimental.pallas import tpu as pltpu

HEAD_DIM = 64
ROPE_BASE = 10000.0
DECAY_LORA = 64
AAA_LORA = 64
GATE_LORA = 160
LN_EPS = 1e-5
RET_GN_EPS = 1e-5
RWKV_GN_EPS = 64e-5
DECAY_SCALE = math.exp(-0.5)
HEAD_SHIFT = HEAD_DIM.bit_length() - 1

LANES = 128
MXU_DIM = 256
HEADS_PER_TILE = MXU_DIM // HEAD_DIM
CHUNK = 64
CHUNKS_PER_STEP = 2
VMEM_LIMIT_BYTES = 56 * 1024 * 1024

F32 = jnp.float32
BF16 = jnp.bfloat16


def _dot(a, b):
    return jnp.dot(a, b, preferred_element_type=F32)


def _dot_nt(a, b):
    return lax.dot_general(a, b, (((1,), (1,)), ((), ())), preferred_element_type=F32)


def _dot_tn(a, b):
    return lax.dot_general(a, b, (((0,), (0,)), ((), ())), preferred_element_type=F32)


def _layer_norm(y, g, b):
    mu = jnp.mean(y, axis=-1, keepdims=True)
    yc = y - mu
    var = jnp.mean(yc * yc, axis=-1, keepdims=True)
    return yc * lax.rsqrt(var + LN_EPS) * g + b


def _const_spec(shape):
    nd = len(shape)
    return pl.BlockSpec(shape, lambda *_: (0,) * nd, pipeline_mode=pl.Buffered(1))


def _ffn_kernel(*refs, alpha, fc, with_pre, with_ple):
    refs = list(refs)
    x_ref = refs.pop(0)
    if with_pre:
        m_ref, wo_ref, gp_ref, bp_ref = refs[:4]
        refs = refs[4:]
    wgu_ref, wd_ref, g_ref, b_ref = refs[:4]
    refs = refs[4:]
    if with_ple:
        p_ref, wp_ref, wpg_ref, bpg_ref = refs[:4]
        refs = refs[4:]
    o_ref, act_ref = refs[:2]
    d_ff = wd_ref.shape[0]
    if with_pre:
        xs_ref = refs[2]
        xs_ref[...] = _layer_norm(alpha * x_ref[...] + _dot(m_ref[...], wo_ref[...]), gp_ref[...], bp_ref[...])
    else:
        xs_ref = x_ref
    xb = xs_ref[...].astype(BF16)
    for c in range(d_ff // fc):
        hg = _dot(xb, wgu_ref[:, c * fc:(c + 1) * fc].astype(BF16))
        hu = _dot(xb, wgu_ref[:, d_ff + c * fc:d_ff + (c + 1) * fc].astype(BF16))
        act_ref[:, c * fc:(c + 1) * fc] = (hg * jax.nn.sigmoid(hg) * hu).astype(BF16)
    down = _dot(act_ref[...], wd_ref[...].astype(BF16))
    y = _layer_norm(alpha * xs_ref[...] + 0.5 * down, g_ref[...], b_ref[...])
    if with_ple:
        gate = jax.nn.sigmoid(_dot(y.astype(BF16), wpg_ref[...]) + bpg_ref[...])
        y = y + gate * _dot(p_ref[...].astype(BF16), wp_ref[...])
    o_ref[...] = y


def _ffn_call(x2d, wgu, wd, ln_g, ln_b, alpha, pre=None, ple=None, tm=512, fc=MXU_DIM):
    n, d = x2d.shape
    d_ff = wd.shape[0]
    tm = min(tm, n)
    row = lambda i: (i, 0)
    in_specs = [pl.BlockSpec((tm, d), row)]
    args = [x2d]
    if pre is not None:
        m2d, wo, gp, bp = pre
        in_specs += [pl.BlockSpec((tm, m2d.shape[1]), row), _const_spec(wo.shape),
                     _const_spec((1, d)), _const_spec((1, d))]
        args += [m2d, wo, gp, bp]
    in_specs += [_const_spec(wgu.shape), _const_spec(wd.shape), _const_spec((1, d)), _const_spec((1, d))]
    args += [wgu, wd, ln_g, ln_b]
    if ple is not None:
        p2d, wp, wpg, bpg = ple
        in_specs += [pl.BlockSpec((tm, p2d.shape[1]), row), _const_spec(wp.shape),
                     _const_spec(wpg.shape), _const_spec((1, d))]
        args += [p2d, wp, wpg, bpg]
    return pl.pallas_call(
        functools.partial(_ffn_kernel, alpha=alpha, fc=fc, with_pre=pre is not None,
                          with_ple=ple is not None),
        grid=(n // tm,),
        in_specs=in_specs,
        out_specs=pl.BlockSpec((tm, d), row),
        out_shape=jax.ShapeDtypeStruct((n, d), F32),
        scratch_shapes=[pltpu.VMEM((tm, d_ff), BF16)] + ([pltpu.VMEM((tm, d), F32)] if pre is not None else []),
        compiler_params=pltpu.CompilerParams(dimension_semantics=("arbitrary",),
                                             vmem_limit_bytes=VMEM_LIMIT_BYTES),
        name="ffn_ln",
    )(*args)


def _group_sum(x, ones_blk):
    outs = [_dot(x[:, t * MXU_DIM:(t + 1) * MXU_DIM].astype(BF16), ones_blk)
            for t in range(x.shape[1] // MXU_DIM)]
    return jnp.concatenate(outs, axis=1)


def _inproj_kernel(x_ref, win_ref, wda_ref, wg_ref, cos_ref, sin_ref, mu_ref, w0_ref, a0_ref,
                   kk_ref, ka_ref,
                   q_ref, k_ref, v_ref, g_ref, r_ref, km_ref, vr_ref, lwh_ref, lwl_ref, a_ref, kku_ref,
                   gate_ref, carry_ref, *, width):
    tm = x_ref.shape[0]

    @pl.when(pl.program_id(1) == 0)
    def _():
        carry_ref[...] = jnp.zeros_like(carry_ref)

    xb = x_ref[...].astype(BF16)

    lane = lax.broadcasted_iota(jnp.int32, (tm, width), 1)
    first_half = (lane & (HEAD_DIM // 2)) == 0
    reps = width // cos_ref.shape[1]
    cos = jnp.concatenate([cos_ref[...]] * reps, axis=1)
    sin = jnp.concatenate([sin_ref[...]] * reps, axis=1)

    def rotary(t):
        swapped = jnp.where(first_half, pltpu.roll(t, width - HEAD_DIM // 2, 1),
                            pltpu.roll(t, HEAD_DIM // 2, 1))
        return t * cos + swapped * sin

    def put(ref, val):
        ref[...] = val.astype(ref.dtype)

    ret_cols = 4 * width
    n_lora = win_ref.shape[0] - ret_cols - 3 * width
    lora_pad = mu_ref.shape[1] - 3 * width

    def project(lo_col, hi_col, pad=0):
        w = win_ref[lo_col:hi_col, :].astype(BF16)
        if pad:
            w = jnp.concatenate([w, jnp.zeros((pad, w.shape[1]), BF16)], axis=0)
        return _dot_nt(xb, w)

    def shifted(lo_col, hi_col, z):
        first = carry_ref[0:1, lo_col:hi_col]
        carry_ref[0:1, lo_col:hi_col] = z[tm - 1:tm, :]
        rowid = lax.broadcasted_iota(jnp.int32, z.shape, 0)
        prev = jnp.where(rowid == 0, first, pltpu.roll(z, 1, 0))
        return z + (prev - z) * mu_ref[:, lo_col:hi_col]

    lo = shifted(3 * width, 3 * width + lora_pad,
                 project(ret_cols + 3 * width, ret_cols + 3 * width + n_lora, pad=lora_pad - n_lora))
    n_da = DECAY_LORA + AAA_LORA
    lo_da = lo[:, 0:n_da]
    ll = lax.broadcasted_iota(jnp.int32, lo_da.shape, 1)
    act_da = jnp.where(ll < DECAY_LORA, jnp.tanh(lo_da), lo_da).astype(BF16)
    up = _dot(act_da, wda_ref[...])
    put(gate_ref, _dot(jax.nn.sigmoid(lo[:, n_da:]).astype(BF16), wg_ref[...]))
    lw = -DECAY_SCALE * jax.nn.sigmoid(w0_ref[...] + up[:, 0 * width:1 * width])
    lw_hi, lw_lo = _split_hi_lo(lw)
    lwh_ref[...] = lw_hi
    lwl_ref[...] = lw_lo
    a = jax.nn.sigmoid(a0_ref[...] + up[:, 1 * width:2 * width])
    put(a_ref, a)
    kr = shifted(1 * width, 2 * width, project(ret_cols + 1 * width, ret_cols + 2 * width))
    put(kku_ref, kr * kk_ref[...])
    put(km_ref, kr * (1.0 + (a - 1.0) * ka_ref[...]))
    put(r_ref, shifted(0 * width, 1 * width, project(ret_cols + 0 * width, ret_cols + 1 * width)))
    put(vr_ref, shifted(2 * width, 3 * width, project(ret_cols + 2 * width, ret_cols + 3 * width)))

    put(q_ref, rotary(project(0 * width, 1 * width)))
    put(k_ref, rotary(project(1 * width, 2 * width)) * (HEAD_DIM ** -0.5))
    put(v_ref, project(2 * width, 3 * width))
    put(g_ref, project(3 * width, 4 * width))


def _inproj_call(x3d, w_in_t, wda, wg, cos, sin, mu, w0, a0, k_k, k_a, tm=512):
    bsz, s, d = x3d.shape
    width = w0.shape[1]
    tm = min(tm, s)
    blk = lambda b, j: (b, j, 0)
    out_spec = pl.BlockSpec((None, tm, width), blk)
    n_out = 12
    out_sds = [jax.ShapeDtypeStruct((bsz, s, width), BF16)] * n_out
    return pl.pallas_call(
        functools.partial(_inproj_kernel, width=width),
        grid=(bsz, s // tm),
        in_specs=[pl.BlockSpec((None, tm, d), blk), _const_spec(w_in_t.shape),
                  _const_spec(wda.shape), _const_spec(wg.shape),
                  pl.BlockSpec((tm, cos.shape[1]), lambda b, j: (j, 0)),
                  pl.BlockSpec((tm, sin.shape[1]), lambda b, j: (j, 0)),
                  _const_spec(mu.shape), _const_spec(w0.shape), _const_spec(a0.shape),
                  _const_spec(k_k.shape), _const_spec(k_a.shape)],
        out_specs=[out_spec] * n_out,
        out_shape=out_sds,
        scratch_shapes=[pltpu.VMEM((8, mu.shape[1]), F32)],
        compiler_params=pltpu.CompilerParams(dimension_semantics=("arbitrary", "arbitrary"),
                                             vmem_limit_bytes=VMEM_LIMIT_BYTES),
        name="inproj_prep",
    )(x3d, w_in_t, wda, wg, cos, sin, mu, w0, a0, k_k, k_a)


def _split_hi_lo(x):
    hi = x.astype(BF16)
    lo = (x - hi.astype(F32)).astype(BF16)
    return hi, lo


def _mix_kernel(q_ref, k_ref, v_ref, r_ref, km_ref, vr_ref, lwh_ref, lwl_ref, a_ref, kku_ref,
                g_ref, gate_ref, rl_ref, kml_ref, vrl_ref,
                dmask_ref, qdec_ref, kdec_ref, cdec_ref, rgn_g_ref, rgn_b_ref, wgn_g_ref, wgn_b_ref,
                rk_ref, o_ref,
                sret_ref, srw_ref, kkn_ref, p_ref, yret_ref, yrw_ref,
                pw_ref, tinv_ref, brp_ref, av_ref, pkt_ref, dec_ref, qr_ref, inner_ref, qd_ref, kv_ref):
    bsz, blk_rows, width = q_ref.shape
    n_sub = blk_rows // CHUNK
    n_tiles = width // MXU_DIM
    n_states = bsz * n_tiles
    n_chains = n_sub * n_states
    hpt = HEADS_PER_TILE
    n_rows = bsz * blk_rows

    def full(ref):
        return ref[...].reshape(n_rows, ref.shape[-1]).astype(F32)

    @pl.when(pl.program_id(0) == 0)
    def _():
        sret_ref[...] = jnp.zeros_like(sret_ref)
        srw_ref[...] = jnp.zeros_like(srw_ref)
        yret_ref[...] = jnp.zeros_like(yret_ref)
        yrw_ref[...] = jnp.zeros_like(yrw_ref)

    ri = lax.broadcasted_iota(jnp.int32, (MXU_DIM, MXU_DIM), 0)
    ci = lax.broadcasted_iota(jnp.int32, (MXU_DIM, MXU_DIM), 1)
    same_head = (ri >> HEAD_SHIFT) == (ci >> HEAD_SHIFT)
    ones_blk = same_head.astype(BF16)
    cr = lax.broadcasted_iota(jnp.int32, (CHUNK, MXU_DIM), 0)
    cj = lax.broadcasted_iota(jnp.int32, (CHUNK, MXU_DIM), 1) & (HEAD_DIM - 1)
    strict_lower, incl_lower = cr > cj, cr >= cj
    eye_c = (cr == cj).astype(F32)
    ti = lax.broadcasted_iota(jnp.int32, (2 * CHUNK, CHUNK), 0)
    tj = lax.broadcasted_iota(jnp.int32, (2 * CHUNK, CHUNK), 1)
    tri = (((ti < CHUNK) & (ti >= tj)) | ((ti >= CHUNK) & (ti - CHUNK > tj))).astype(BF16)

    heads_per_lane_tile = LANES // HEAD_DIM
    diag_tiles = [(slice(h * HEAD_DIM, (h + 1) * HEAD_DIM),
                   slice((h // heads_per_lane_tile) * LANES, (h // heads_per_lane_tile + 1) * LANES))
                  for h in range(hpt)]

    def state_bf16(ref, c):
        zeros = jnp.zeros((HEAD_DIM, LANES), BF16)
        rows = []
        for rs, ls in diag_tiles:
            parts = [zeros] * (MXU_DIM // LANES)
            parts[ls.start // LANES] = ref[c, rs, ls].astype(BF16)
            rows.append(jnp.concatenate(parts, axis=1))
        return jnp.concatenate(rows, axis=0)

    def block_diag(x):
        return jnp.where(same_head, jnp.concatenate([x] * hpt, axis=0), jnp.zeros((), x.dtype))

    def group_sum(x):
        return _group_sum(x, ones_blk)

    def normalise_keys():
        kku = full(kku_ref)
        kkn = kku * lax.rsqrt(jnp.maximum(group_sum(kku * kku), 1e-24))
        kkn_ref[...] = kkn
        p_ref[...] = -(kkn * full(a_ref))

    def prepare(sub, b):
        rows = pl.ds(b * blk_rows + sub * CHUNK, CHUNK)
        part = (b, slice(sub * CHUNK, (sub + 1) * CHUNK))
        cums = _dot(tri, lwh_ref[part]) + _dot(tri, lwl_ref[part])
        cum = cums[0:CHUNK]
        e_in = jnp.exp(cum)
        e_ex = jnp.exp(cums[CHUNK:])
        e_neg = jnp.exp(-cum)
        e_last = e_in[CHUNK - 1:CHUNK, :]
        qt_all = kkn_ref[rows, :] * e_ex
        rt_all = r_ref[part].astype(F32) * e_in
        pt_all = p_ref[rows, :] * e_neg
        kt_all = km_ref[part].astype(F32) * e_neg
        ph_all = pt_all * e_last
        kh_all = kt_all * e_last
        vr_all = vr_ref[part].astype(BF16)
        q_all = q_ref[part]
        k_all = k_ref[part]
        v_all = v_ref[part].astype(BF16)
        qd_all = (q_all.astype(F32) * qdec_ref[...]).astype(BF16)
        kd_all = (k_all.astype(F32) * kdec_ref[...]).astype(BF16)
        qt_b, rt_b = qt_all.astype(BF16), rt_all.astype(BF16)
        pt_b, kt_b = pt_all.astype(BF16), kt_all.astype(BF16)
        q_b, k_b = q_all.astype(BF16), k_all.astype(BF16)

        for t in range(n_tiles):
            c = sub * n_states + b * n_tiles + t
            sl = slice(t * MXU_DIM, (t + 1) * MXU_DIM)
            qr = jnp.concatenate([qt_b[:, sl], rt_b[:, sl]], axis=0)
            qr_ref[c] = qr
            g_p = _dot_nt(qr, block_diag(pt_b[:, sl]))
            g_k = _dot_nt(qr, block_diag(kt_b[:, sl]))
            a_qp = jnp.where(strict_lower, g_p[0:CHUNK], 0.0)
            pw_ref[c] = a_qp.astype(BF16)
            tinv_ref[c] = a_qp + eye_c
            brp_ref[c] = jnp.where(incl_lower, g_p[CHUNK:], 0.0).astype(BF16)
            ab = jnp.concatenate([jnp.where(strict_lower, g_k[0:CHUNK], 0.0),
                                  jnp.where(incl_lower, g_k[CHUNK:], 0.0)], axis=0).astype(BF16)
            av_ref[c] = _dot(ab, block_diag(vr_all[:, sl]))
            pkt_ref[c] = jnp.transpose(jnp.concatenate([ph_all[:, sl], kh_all[:, sl]], axis=0)).astype(BF16)
            dec_ref[c] = jnp.transpose(jnp.broadcast_to(e_last[:, sl], (LANES, MXU_DIM)))
            scores = (_dot_nt(q_b[:, sl], block_diag(k_b[:, sl])) * dmask_ref[t]).astype(BF16)
            inner_ref[c] = _dot(scores, block_diag(v_all[:, sl]))
            qd_ref[c] = qd_all[:, sl]
            kv = _dot_tn(kd_all[:, sl], v_all[:, sl])
            for rs, ls in diag_tiles:
                kv_ref[c, rs, ls] = jnp.where(same_head[rs, ls], kv[rs, ls], 0.0)

    inv_hd = 1.0 / HEAD_DIM

    def group_norm(y, eps):
        mu = group_sum(y) * inv_hd
        yc = y - mu
        var = group_sum(yc * yc) * inv_hd
        return yc * lax.rsqrt(var + eps)

    def norm_retention():
        yret_ref[...] = group_norm(yret_ref[...], RET_GN_EPS) * rgn_g_ref[...] + rgn_b_ref[...]

    def norm_rwkv():
        bonus = group_sum(full(rl_ref) * full(kml_ref) * rk_ref[...]) * full(vrl_ref)
        yrw_ref[...] = group_norm(yrw_ref[...], RWKV_GN_EPS) * wgn_g_ref[...] + wgn_b_ref[...] + bonus

    def gate_and_store(b_lo, b_hi):
        for b in range(b_lo, b_hi):
            rows = pl.ds(b * blk_rows, blk_rows)
            g = g_ref[b].astype(F32)
            o_ref[b, :, 0:width] = (g * jax.nn.sigmoid(g) * yret_ref[rows, :]).astype(o_ref.dtype)
            o_ref[b, :, width:2 * width] = (yrw_ref[rows, :] * gate_ref[b].astype(F32)).astype(o_ref.dtype)

    filler = [norm_retention, norm_rwkv, functools.partial(gate_and_store, 0, bsz // 2),
              functools.partial(gate_and_store, bsz // 2, bsz)]

    def invert_and_finish_previous():
        for c in range(n_chains):
            pwb = pw_ref[c]
            pw_ref[c] = _dot(pwb, block_diag(pwb)).astype(BF16)
        for rnd in range(4):
            for c in range(n_chains):
                pwb, t_acc = pw_ref[c], tinv_ref[c]
                res = _dot(jnp.concatenate([pwb, t_acc.astype(BF16)], axis=0), block_diag(pwb))
                pw_ref[c] = res[0:CHUNK].astype(BF16)
                tinv_ref[c] = t_acc + res[CHUNK:]
            filler[rnd]()
        for c in range(n_chains):
            t_acc = tinv_ref[c]
            tinv_ref[c] = t_acc + _dot(t_acc.astype(BF16), block_diag(pw_ref[c]))

    def apply_states(sub):
        chains = [(sub * n_states + st, st, st // n_tiles, (st // n_tiles, slice(sub * CHUNK, (sub + 1) * CHUNK)),
                   pl.ds((st // n_tiles) * blk_rows + sub * CHUNK, CHUNK),
                   slice((st % n_tiles) * MXU_DIM, (st % n_tiles + 1) * MXU_DIM)) for st in range(n_states)]
        qrm = {c: _dot(qr_ref[c], state_bf16(srw_ref, st)) for c, st, _, _, _, _ in chains}
        u = {c: _dot(tinv_ref[c].astype(BF16),
                     block_diag((qrm[c][0:CHUNK] + av_ref[c, 0:CHUNK, :]).astype(BF16)))
             for c, _, _, _, _, _ in chains}
        for c, st, b, part, rows, sl in chains:
            ub = u[c].astype(BF16)
            yrw_ref[rows, sl] = qrm[c][CHUNK:] + av_ref[c, CHUNK:, :] + _dot(brp_ref[c], block_diag(ub))
            dec = dec_ref[c]
            upd = _dot(pkt_ref[c], jnp.concatenate([ub, vr_ref[part][:, sl]], axis=0))
            yret_ref[rows, sl] = inner_ref[c] + _dot(qd_ref[c], state_bf16(sret_ref, st))
            for rs, ls in diag_tiles:
                srw_ref[st, rs, ls] = (srw_ref[st, rs, ls] * dec[rs, :]
                                       + jnp.where(same_head[rs, ls], upd[rs, ls], 0.0))
                sret_ref[st, rs, ls] = (sret_ref[st, rs, ls] * cdec_ref[st % n_tiles, rs, ls]
                                        + kv_ref[c, rs, ls])

    last_step = pl.num_programs(0) - 1

    @pl.when(pl.program_id(0) < last_step)
    def _():
        normalise_keys()
        for sub in range(n_sub):
            for b in range(bsz):
                prepare(sub, b)
        invert_and_finish_previous()
        for sub in range(n_sub):
            apply_states(sub)

    @pl.when(pl.program_id(0) == last_step)
    def _():
        for finish in filler:
            finish()


def _mix_call(streams, dmask, qdec, kdec, cdec, rgn_g, rgn_b, wgn_g, wgn_b, rk):
    q, k, v, g, r, km, vr, lwh, lwl, a, kku, gate = streams
    bsz, s, width = q.shape
    n_tiles = width // MXU_DIM
    blk_rows = min(CHUNKS_PER_STEP * CHUNK, s)
    n_steps = s // blk_rows
    cur = pl.BlockSpec((bsz, blk_rows, width), lambda j: (0, jnp.minimum(j, n_steps - 1), 0))
    prev_idx = lambda j: (0, jnp.maximum(j - 1, 0), 0)
    prev = pl.BlockSpec((bsz, blk_rows, width), prev_idx)
    current_streams = [q, k, v, r, km, vr, lwh, lwl, a, kku]
    previous_streams = [g, gate, r, km, vr]
    consts = [dmask, qdec, kdec, cdec, rgn_g, rgn_b, wgn_g, wgn_b, rk]
    n_states = bsz * n_tiles
    n_chains = (blk_rows // CHUNK) * n_states
    tc = bsz * blk_rows
    state = (n_states, MXU_DIM, MXU_DIM)
    tile = (n_chains, MXU_DIM, MXU_DIM)
    flat = (n_chains, CHUNK, MXU_DIM)
    pair = (n_chains, 2 * CHUNK, MXU_DIM)
    return pl.pallas_call(
        _mix_kernel,
        grid=(n_steps + 1,),
        in_specs=([cur] * len(current_streams) + [prev] * len(previous_streams)
                  + [_const_spec(c.shape) for c in consts]),
        out_specs=pl.BlockSpec((bsz, blk_rows, 2 * width), prev_idx),
        out_shape=jax.ShapeDtypeStruct((bsz, s, 2 * width), BF16),
        scratch_shapes=[pltpu.VMEM(state, F32),
                        pltpu.VMEM(state, F32),
                        pltpu.VMEM((tc, width), F32), pltpu.VMEM((tc, width), F32),
                        pltpu.VMEM((tc, width), F32), pltpu.VMEM((tc, width), F32),
                        pltpu.VMEM(flat, BF16), pltpu.VMEM(flat, F32), pltpu.VMEM(flat, BF16),
                        pltpu.VMEM(pair, F32),
                        pltpu.VMEM((n_chains, MXU_DIM, 2 * CHUNK), BF16),
                        pltpu.VMEM((n_chains, MXU_DIM, LANES), F32),
                        pltpu.VMEM(pair, BF16),
                        pltpu.VMEM(flat, F32), pltpu.VMEM(flat, BF16), pltpu.VMEM(tile, F32)],
        compiler_params=pltpu.CompilerParams(dimension_semantics=("arbitrary",),
                                             vmem_limit_bytes=VMEM_LIMIT_BYTES),
        name="mix_recurrences",
    )(*current_streams, *previous_streams, *consts)


def _retention_tables(n_heads):
    h = jnp.arange(n_heads, dtype=F32)
    log_gamma = jnp.log1p(-jnp.exp2(-5.0 - h))
    lg_lane = jnp.repeat(log_gamma, HEAD_DIM)[None, :]
    idx = jnp.arange(CHUNK, dtype=F32)[:, None]
    qdec = jnp.exp((idx + 1.0) * lg_lane)
    kdec = jnp.exp((CHUNK - 1.0 - idx) * lg_lane)
    n_tiles = n_heads // HEADS_PER_TILE
    cdec = jnp.broadcast_to(jnp.exp(CHUNK * lg_lane).reshape(n_tiles, MXU_DIM, 1),
                            (n_tiles, MXU_DIM, MXU_DIM))
    rel = idx - (jnp.arange(MXU_DIM) % HEAD_DIM).astype(F32)[None, :]
    lg_tiles = lg_lane.reshape(n_tiles, 1, MXU_DIM)
    dmask = jnp.where((rel >= 0)[None], jnp.exp(jnp.where(rel >= 0, rel, 0.0)[None] * lg_tiles), 0.0)
    return dmask, qdec, kdec, cdec


def _rotary_tables(s):
    pos = jnp.arange(s, dtype=F32)
    inv_freq = ROPE_BASE ** (-jnp.arange(0, HEAD_DIM, 2, dtype=F32) / HEAD_DIM)
    ang = pos[:, None] * inv_freq[None, :]
    cos, sin = jnp.cos(ang), jnp.sin(ang)
    reps = LANES // HEAD_DIM
    cos_t = jnp.tile(jnp.concatenate([cos, cos], axis=1), (1, reps))
    sin_t = jnp.tile(jnp.concatenate([-sin, sin], axis=1), (1, reps))
    return cos_t, sin_t


def kernel(x, p, ffn1_w_gu, ffn1_w_down, ln1_g, ln1_b, w_in, ret_gn_g, ret_gn_b, rw_mu, rw_w0, rw_w_up, rw_a0, rw_a_up, rw_g_up, rw_k_k, rw_k_a, rw_r_k, rw_gn_g, rw_gn_b, w_out, ln2_g, ln2_b, ffn2_w_gu, ffn2_w_down, ln3_g, ln3_b, ple_w_proj, ple_w_gate, ple_b_gate):
    bsz, s, d = x.shape
    depth = ffn1_w_gu.shape[0]
    alpha = (2.0 * depth) ** 0.25
    width = rw_w0.shape[1]
    n_heads = width // HEAD_DIM
    ret_cols = 4 * width
    lora = DECAY_LORA + AAA_LORA + GATE_LORA
    lora_pad = -(-lora // LANES) * LANES

    dmask, qdec, kdec, cdec = _retention_tables(n_heads)
    cos_t, sin_t = _rotary_tables(s)
    row = lambda v: v.reshape(1, -1)

    h = x.reshape(bsz * s, d)
    for i in range(depth):
        h = _ffn_call(h, ffn1_w_gu[i], ffn1_w_down[i],
                      row(ln1_g[i]), row(ln1_b[i]), alpha)

        mu = row(jnp.pad(rw_mu[i], (0, lora_pad - lora)))
        n_da = DECAY_LORA + AAA_LORA
        wda = jnp.zeros((n_da, 2 * width), BF16)
        wda = wda.at[:DECAY_LORA, :width].set(rw_w_up[i].astype(BF16))
        wda = wda.at[DECAY_LORA:, width:].set(rw_a_up[i].astype(BF16))
        wg = jnp.pad(rw_g_up[i].astype(BF16), ((0, lora_pad - lora), (0, 0)))
        streams = _inproj_call(h.reshape(bsz, s, d), jnp.swapaxes(w_in[i], 0, 1), wda, wg, cos_t, sin_t, mu,
                               row(rw_w0[i]), row(rw_a0[i]), row(rw_k_k[i]), row(rw_k_a[i]))
        mixed = _mix_call(streams, dmask, qdec, kdec, cdec, row(ret_gn_g[i]), row(ret_gn_b[i]),
                          row(rw_gn_g[i]), row(rw_gn_b[i]), row(rw_r_k[i]))
        pre = (mixed.reshape(bsz * s, 2 * width), w_out[i].astype(BF16), row(ln2_g[i]), row(ln2_b[i]))
        ple = (p[i].reshape(bsz * s, -1), ple_w_proj[i].astype(BF16), ple_w_gate[i].astype(BF16),
               row(ple_b_gate[i]))
        h = _ffn_call(h, ffn2_w_gu[i], ffn2_w_down[i],
                      row(ln3_g[i]), row(ln3_b[i]), alpha, pre=pre, ple=ple)
    return h.reshape(bsz, s, d)
```

```python
import functools
import math

import jax
import jax.numpy as jnp
from jax import lax
from jax.experimental import pallas as pl
from jax.experimental.pallas import tpu as pltpu

HEAD_DIM = 64
ROPE_BASE = 10000.0
DECAY_LORA = 64
AAA_LORA = 64
GATE_LORA = 160
LN_EPS = 1e-5
RET_GN_EPS = 1e-5
RWKV_GN_EPS = 64e-5
DECAY_SCALE = math.exp(-0.5)
HEAD_SHIFT = HEAD_DIM.bit_length() - 1

LANES = 128
MXU_DIM = 256
HEADS_PER_TILE = MXU_DIM // HEAD_DIM
CHUNK = 64
CHUNKS_PER_STEP = 2
VMEM_LIMIT_BYTES = 60 * 1024 * 1024

F32 = jnp.float32
BF16 = jnp.bfloat16


def _dot(a, b):
    return jnp.dot(a, b, preferred_element_type=F32)


def _dot_nt(a, b):
    return lax.dot_general(a, b, (((1,), (1,)), ((), ())), preferred_element_type=F32)


def _dot_tn(a, b):
    return lax.dot_general(a, b, (((0,), (0,)), ((), ())), preferred_element_type=F32)


def _layer_norm(y, g, b):
    mu = jnp.mean(y, axis=-1, keepdims=True)
    yc = y - mu
    var = jnp.mean(yc * yc, axis=-1, keepdims=True)
    return yc * lax.rsqrt(var + LN_EPS) * g + b


def _const_spec(shape):
    nd = len(shape)
    return pl.BlockSpec(shape, lambda *_: (0,) * nd, pipeline_mode=pl.Buffered(1))


def _ffn_kernel(*refs, alpha, fc, with_pre, with_ple):
    refs = list(refs)
    x_ref = refs.pop(0)
    if with_pre:
        m_ref, wo_ref, gp_ref, bp_ref = refs[:4]
        refs = refs[4:]
    wgu_ref, wd_ref, g_ref, b_ref = refs[:4]
    refs = refs[4:]
    if with_ple:
        p_ref, wp_ref, wpg_ref, bpg_ref = refs[:4]
        refs = refs[4:]
    o_ref, act_ref = refs[:2]
    d_ff = wd_ref.shape[0]
    pend_ref = refs[2]
    step, last = pl.program_id(0), pl.num_programs(0) - 1

    def finish_previous():
        y = _layer_norm(pend_ref[...], g_ref[...], b_ref[...])
        if with_ple:
            gate = jax.nn.sigmoid(_dot(y.astype(BF16), wpg_ref[...]) + bpg_ref[...])
            y = y + gate * _dot(p_ref[...].astype(BF16), wp_ref[...])
        o_ref[...] = y

    @pl.when(step == 0)
    def _():
        pend_ref[...] = jnp.zeros_like(pend_ref)

    @pl.when(step < last)
    def _():
        if with_pre:
            xs_ref = refs[3]
            xs_ref[...] = _layer_norm(alpha * x_ref[...] + _dot(m_ref[...], wo_ref[...]), gp_ref[...], bp_ref[...])
        else:
            xs_ref = x_ref
        xb = xs_ref[...].astype(BF16)
        for c in range(d_ff // fc):
            hg = _dot(xb, wgu_ref[:, c * fc:(c + 1) * fc].astype(BF16))
            hu = _dot(xb, wgu_ref[:, d_ff + c * fc:d_ff + (c + 1) * fc].astype(BF16))
            act_ref[:, c * fc:(c + 1) * fc] = (hg * jax.nn.sigmoid(hg) * hu).astype(BF16)
            if c == 1:
                finish_previous()
        pend_ref[...] = alpha * xs_ref[...] + 0.5 * _dot(act_ref[...], wd_ref[...].astype(BF16))

    pl.when(step == last)(finish_previous)


def _ffn_call(x2d, wgu, wd, ln_g, ln_b, alpha, pre=None, ple=None, tm=512, fc=MXU_DIM):
    n, d = x2d.shape
    d_ff = wd.shape[0]
    tm = min(tm, n)
    n_tiles = n // tm
    row = lambda i: (jnp.minimum(i, n_tiles - 1), 0)
    lag = lambda i: (jnp.maximum(i - 1, 0), 0)
    in_specs = [pl.BlockSpec((tm, d), row)]
    args = [x2d]
    if pre is not None:
        m2d, wo, gp, bp = pre
        in_specs += [pl.BlockSpec((tm, m2d.shape[1]), row), _const_spec(wo.shape),
                     _const_spec((1, d)), _const_spec((1, d))]
        args += [m2d, wo, gp, bp]
    in_specs += [_const_spec(wgu.shape), _const_spec(wd.shape), _const_spec((1, d)), _const_spec((1, d))]
    args += [wgu, wd, ln_g, ln_b]
    if ple is not None:
        p2d, wp, wpg, bpg = ple
        in_specs += [pl.BlockSpec((tm, p2d.shape[1]), lag), _const_spec(wp.shape),
                     _const_spec(wpg.shape), _const_spec((1, d))]
        args += [p2d, wp, wpg, bpg]
    return pl.pallas_call(
        functools.partial(_ffn_kernel, alpha=alpha, fc=fc, with_pre=pre is not None,
                          with_ple=ple is not None),
        grid=(n_tiles + 1,),
        in_specs=in_specs,
        out_specs=pl.BlockSpec((tm, d), lag),
        out_shape=jax.ShapeDtypeStruct((n, d), F32),
        scratch_shapes=[pltpu.VMEM((tm, d_ff), BF16), pltpu.VMEM((tm, d), F32)]
        + ([pltpu.VMEM((tm, d), F32)] if pre is not None else []),
        compiler_params=pltpu.CompilerParams(dimension_semantics=("arbitrary",),
                                             vmem_limit_bytes=VMEM_LIMIT_BYTES),
        name="ffn_ln",
    )(*args)


def _group_sum(x, ones_blk):
    outs = [_dot(x[:, t * MXU_DIM:(t + 1) * MXU_DIM].astype(BF16), ones_blk)
            for t in range(x.shape[1] // MXU_DIM)]
    return jnp.concatenate(outs, axis=1)


def _inproj_kernel(x_ref, win_ref, wda_ref, wg_ref, cos_ref, sin_ref, mu_ref, w0_ref, a0_ref,
                   kk_ref, ka_ref,
                   q_ref, k_ref, v_ref, g_ref, r_ref, km_ref, vr_ref, lwh_ref, lwl_ref, a_ref, kku_ref,
                   gate_ref, carry_ref, *, width):
    tm = x_ref.shape[0]

    @pl.when(pl.program_id(1) == 0)
    def _():
        carry_ref[...] = jnp.zeros_like(carry_ref)

    xb = x_ref[...].astype(BF16)

    lane = lax.broadcasted_iota(jnp.int32, (tm, width), 1)
    first_half = (lane & (HEAD_DIM // 2)) == 0
    reps = width // cos_ref.shape[1]
    cos = jnp.concatenate([cos_ref[...]] * reps, axis=1)
    sin = jnp.concatenate([sin_ref[...]] * reps, axis=1)

    def rotary(t):
        swapped = jnp.where(first_half, pltpu.roll(t, width - HEAD_DIM // 2, 1),
                            pltpu.roll(t, HEAD_DIM // 2, 1))
        return t * cos + swapped * sin

    def put(ref, val):
        ref[...] = val.astype(ref.dtype)

    ret_cols = 4 * width
    n_lora = win_ref.shape[0] - ret_cols - 3 * width
    lora_pad = mu_ref.shape[1] - 3 * width

    def project(lo_col, hi_col, pad=0):
        w = win_ref[lo_col:hi_col, :].astype(BF16)
        if pad:
            w = jnp.concatenate([w, jnp.zeros((pad, w.shape[1]), BF16)], axis=0)
        return _dot_nt(xb, w)

    def shifted(lo_col, hi_col, z):
        first = carry_ref[0:1, lo_col:hi_col]
        carry_ref[0:1, lo_col:hi_col] = z[tm - 1:tm, :]
        rowid = lax.broadcasted_iota(jnp.int32, z.shape, 0)
        prev = jnp.where(rowid == 0, first, pltpu.roll(z, 1, 0))
        return z + (prev - z) * mu_ref[:, lo_col:hi_col]

    lo = shifted(3 * width, 3 * width + lora_pad,
                 project(ret_cols + 3 * width, ret_cols + 3 * width + n_lora, pad=lora_pad - n_lora))
    n_da = DECAY_LORA + AAA_LORA
    lo_da = lo[:, 0:n_da]
    ll = lax.broadcasted_iota(jnp.int32, lo_da.shape, 1)
    act_da = jnp.where(ll < DECAY_LORA, jnp.tanh(lo_da), lo_da).astype(BF16)
    up = _dot(act_da, wda_ref[...])
    put(gate_ref, _dot(jax.nn.sigmoid(lo[:, n_da:]).astype(BF16), wg_ref[...]))
    lw = -DECAY_SCALE * jax.nn.sigmoid(w0_ref[...] + up[:, 0 * width:1 * width])
    lw_hi, lw_lo = _split_hi_lo(lw)
    lwh_ref[...] = lw_hi
    lwl_ref[...] = lw_lo
    a = jax.nn.sigmoid(a0_ref[...] + up[:, 1 * width:2 * width])
    put(a_ref, a)
    kr = shifted(1 * width, 2 * width, project(ret_cols + 1 * width, ret_cols + 2 * width))
    put(kku_ref, kr * kk_ref[...])
    put(km_ref, kr * (1.0 + (a - 1.0) * ka_ref[...]))
    put(r_ref, shifted(0 * width, 1 * width, project(ret_cols + 0 * width, ret_cols + 1 * width)))
    put(vr_ref, shifted(2 * width, 3 * width, project(ret_cols + 2 * width, ret_cols + 3 * width)))

    put(q_ref, rotary(project(0 * width, 1 * width)))
    put(k_ref, rotary(project(1 * width, 2 * width)) * (HEAD_DIM ** -0.5))
    put(v_ref, project(2 * width, 3 * width))
    put(g_ref, project(3 * width, 4 * width))


def _inproj_call(x3d, w_in_t, wda, wg, cos, sin, mu, w0, a0, k_k, k_a, tm=512):
    bsz, s, d = x3d.shape
    width = w0.shape[1]
    tm = min(tm, s)
    blk = lambda b, j: (b, j, 0)
    out_spec = pl.BlockSpec((None, tm, width), blk)
    n_out = 12
    out_sds = [jax.ShapeDtypeStruct((bsz, s, width), BF16)] * n_out
    return pl.pallas_call(
        functools.partial(_inproj_kernel, width=width),
        grid=(bsz, s // tm),
        in_specs=[pl.BlockSpec((None, tm, d), blk), _const_spec(w_in_t.shape),
                  _const_spec(wda.shape), _const_spec(wg.shape),
                  pl.BlockSpec((tm, cos.shape[1]), lambda b, j: (j, 0)),
                  pl.BlockSpec((tm, sin.shape[1]), lambda b, j: (j, 0)),
                  _const_spec(mu.shape), _const_spec(w0.shape), _const_spec(a0.shape),
                  _const_spec(k_k.shape), _const_spec(k_a.shape)],
        out_specs=[out_spec] * n_out,
        out_shape=out_sds,
        scratch_shapes=[pltpu.VMEM((8, mu.shape[1]), F32)],
        compiler_params=pltpu.CompilerParams(dimension_semantics=("arbitrary", "arbitrary"),
                                             vmem_limit_bytes=VMEM_LIMIT_BYTES),
        name="inproj_prep",
    )(x3d, w_in_t, wda, wg, cos, sin, mu, w0, a0, k_k, k_a)


def _split_hi_lo(x):
    hi = x.astype(BF16)
    lo = (x - hi.astype(F32)).astype(BF16)
    return hi, lo


def _mix_kernel(q_ref, k_ref, v_ref, r_ref, km_ref, vr_ref, lwh_ref, lwl_ref, a_ref, kku_ref,
                g_ref, gate_ref, rl_ref, kml_ref, vrl_ref,
                dmask_ref, qdec_ref, kdec_ref, cdec_ref, rgn_g_ref, rgn_b_ref, wgn_g_ref, wgn_b_ref,
                rk_ref, o_ref,
                sret_ref, srw_ref, kkn_ref, p_ref, yret_ref, yrw_ref,
                pw_ref, tinv_ref, brp_ref, av_ref, pkt_ref, dec_ref, qr_ref, inner_ref, qd_ref, kv_ref):
    bsz, blk_rows, width = q_ref.shape
    n_sub = blk_rows // CHUNK
    n_tiles = width // MXU_DIM
    n_states = bsz * n_tiles
    n_chains = n_sub * n_states
    hpt = HEADS_PER_TILE
    n_rows = bsz * blk_rows

    def full(ref):
        return ref[...].reshape(n_rows, ref.shape[-1]).astype(F32)

    @pl.when(pl.program_id(0) == 0)
    def _():
        sret_ref[...] = jnp.zeros_like(sret_ref)
        srw_ref[...] = jnp.zeros_like(srw_ref)
        yret_ref[...] = jnp.zeros_like(yret_ref)
        yrw_ref[...] = jnp.zeros_like(yrw_ref)

    ri = lax.broadcasted_iota(jnp.int32, (MXU_DIM, MXU_DIM), 0)
    ci = lax.broadcasted_iota(jnp.int32, (MXU_DIM, MXU_DIM), 1)
    same_head = (ri >> HEAD_SHIFT) == (ci >> HEAD_SHIFT)
    ones_blk = same_head.astype(BF16)
    cr = lax.broadcasted_iota(jnp.int32, (CHUNK, MXU_DIM), 0)
    cj = lax.broadcasted_iota(jnp.int32, (CHUNK, MXU_DIM), 1) & (HEAD_DIM - 1)
    strict_lower, incl_lower = cr > cj, cr >= cj
    eye_c = (cr == cj).astype(F32)
    ti = lax.broadcasted_iota(jnp.int32, (2 * CHUNK, CHUNK), 0)
    tj = lax.broadcasted_iota(jnp.int32, (2 * CHUNK, CHUNK), 1)
    tri = (((ti < CHUNK) & (ti >= tj)) | ((ti >= CHUNK) & (ti - CHUNK > tj))).astype(BF16)

    heads_per_lane_tile = LANES // HEAD_DIM
    diag_tiles = [(slice(h * HEAD_DIM, (h + 1) * HEAD_DIM),
                   slice((h // heads_per_lane_tile) * LANES, (h // heads_per_lane_tile + 1) * LANES))
                  for h in range(hpt)]

    def state_bf16(ref, c):
        zeros = jnp.zeros((HEAD_DIM, LANES), BF16)
        rows = []
        for rs, ls in diag_tiles:
            parts = [zeros] * (MXU_DIM // LANES)
            parts[ls.start // LANES] = ref[c, rs, ls].astype(BF16)
            rows.append(jnp.concatenate(parts, axis=1))
        return jnp.concatenate(rows, axis=0)

    def block_diag(x):
        return jnp.where(same_head, jnp.concatenate([x] * hpt, axis=0), jnp.zeros((), x.dtype))

    def group_sum(x):
        return _group_sum(x, ones_blk)

    def normalise_keys():
        kku = full(kku_ref)
        kkn = kku * lax.rsqrt(jnp.maximum(group_sum(kku * kku), 1e-24))
        kkn_ref[...] = kkn
        p_ref[...] = -(kkn * full(a_ref))

    def prepare(sub, b):
        rows = pl.ds(b * blk_rows + sub * CHUNK, CHUNK)
        part = (b, slice(sub * CHUNK, (sub + 1) * CHUNK))
        cums = _dot(tri, lwh_ref[part]) + _dot(tri, lwl_ref[part])
        cum = cums[0:CHUNK]
        e_in = jnp.exp(cum)
        e_ex = jnp.exp(cums[CHUNK:])
        e_neg = jnp.exp(-cum)
        e_last = e_in[CHUNK - 1:CHUNK, :]
        qt_all = kkn_ref[rows, :] * e_ex
        rt_all = r_ref[part].astype(F32) * e_in
        pt_all = p_ref[rows, :] * e_neg
        kt_all = km_ref[part].astype(F32) * e_neg
        ph_all = pt_all * e_last
        kh_all = kt_all * e_last
        vr_all = vr_ref[part].astype(BF16)
        q_all = q_ref[part]
        k_all = k_ref[part]
        v_all = v_ref[part].astype(BF16)
        qd_all = (q_all.astype(F32) * qdec_ref[...]).astype(BF16)
        kd_all = (k_all.astype(F32) * kdec_ref[...]).astype(BF16)
        qt_b, rt_b = qt_all.astype(BF16), rt_all.astype(BF16)
        pt_b, kt_b = pt_all.astype(BF16), kt_all.astype(BF16)
        q_b, k_b = q_all.astype(BF16), k_all.astype(BF16)

        for t in range(n_tiles):
            c = sub * n_states + b * n_tiles + t
            sl = slice(t * MXU_DIM, (t + 1) * MXU_DIM)
            qr = jnp.concatenate([qt_b[:, sl], rt_b[:, sl]], axis=0)
            qr_ref[c] = qr
            g_p = _dot_nt(qr, block_diag(pt_b[:, sl]))
            g_k = _dot_nt(qr, block_diag(kt_b[:, sl]))
            a_qp = jnp.where(strict_lower, g_p[0:CHUNK], 0.0)
            pw_ref[c] = a_qp.astype(BF16)
            tinv_ref[c] = a_qp + eye_c
            brp_ref[c] = jnp.where(incl_lower, g_p[CHUNK:], 0.0).astype(BF16)
            ab = jnp.concatenate([jnp.where(strict_lower, g_k[0:CHUNK], 0.0),
                                  jnp.where(incl_lower, g_k[CHUNK:], 0.0)], axis=0).astype(BF16)
            av_ref[c] = _dot(ab, block_diag(vr_all[:, sl]))
            pkt_ref[c] = jnp.transpose(jnp.concatenate([ph_all[:, sl], kh_all[:, sl]], axis=0)).astype(BF16)
            dec_ref[c] = jnp.transpose(jnp.broadcast_to(e_last[:, sl], (LANES, MXU_DIM)))
            scores = (_dot_nt(q_b[:, sl], block_diag(k_b[:, sl])) * dmask_ref[t]).astype(BF16)
            inner_ref[c] = _dot(scores, block_diag(v_all[:, sl]))
            qd_ref[c] = qd_all[:, sl]
            kv = _dot_tn(kd_all[:, sl], v_all[:, sl])
            for rs, ls in diag_tiles:
                kv_ref[c, rs, ls] = jnp.where(same_head[rs, ls], kv[rs, ls], 0.0)

    inv_hd = 1.0 / HEAD_DIM

    def group_norm(y, eps):
        mu = group_sum(y) * inv_hd
        yc = y - mu
        var = group_sum(yc * yc) * inv_hd
        return yc * lax.rsqrt(var + eps)

    def norm_retention():
        yret_ref[...] = group_norm(yret_ref[...], RET_GN_EPS) * rgn_g_ref[...] + rgn_b_ref[...]

    def norm_rwkv():
        bonus = group_sum(full(rl_ref) * full(kml_ref) * rk_ref[...]) * full(vrl_ref)
        yrw_ref[...] = group_norm(yrw_ref[...], RWKV_GN_EPS) * wgn_g_ref[...] + wgn_b_ref[...] + bonus

    def gate_and_store(b_lo, b_hi):
        for b in range(b_lo, b_hi):
            rows = pl.ds(b * blk_rows, blk_rows)
            g = g_ref[b].astype(F32)
            o_ref[b, :, 0:width] = (g * jax.nn.sigmoid(g) * yret_ref[rows, :]).astype(o_ref.dtype)
            o_ref[b, :, width:2 * width] = (yrw_ref[rows, :] * gate_ref[b].astype(F32)).astype(o_ref.dtype)

    filler = [norm_retention, norm_rwkv, functools.partial(gate_and_store, 0, bsz // 2),
              functools.partial(gate_and_store, bsz // 2, bsz)]

    def invert_and_finish_previous():
        for c in range(n_chains):
            pwb = pw_ref[c]
            pw_ref[c] = _dot(pwb, block_diag(pwb)).astype(BF16)
        for rnd in range(4):
            for c in range(n_chains):
                pwb, t_acc = pw_ref[c], tinv_ref[c]
                res = _dot(jnp.concatenate([pwb, t_acc.astype(BF16)], axis=0), block_diag(pwb))
                pw_ref[c] = res[0:CHUNK].astype(BF16)
                tinv_ref[c] = t_acc + res[CHUNK:]
            filler[rnd]()
        for c in range(n_chains):
            t_acc = tinv_ref[c]
            tinv_ref[c] = t_acc + _dot(t_acc.astype(BF16), block_diag(pw_ref[c]))

    def apply_states(sub):
        chains = [(sub * n_states + st, st, st // n_tiles, (st // n_tiles, slice(sub * CHUNK, (sub + 1) * CHUNK)),
                   pl.ds((st // n_tiles) * blk_rows + sub * CHUNK, CHUNK),
                   slice((st % n_tiles) * MXU_DIM, (st % n_tiles + 1) * MXU_DIM)) for st in range(n_states)]
        qrm = {c: _dot(qr_ref[c], state_bf16(srw_ref, st)) for c, st, _, _, _, _ in chains}
        u = {c: _dot(tinv_ref[c].astype(BF16),
                     block_diag((qrm[c][0:CHUNK] + av_ref[c, 0:CHUNK, :]).astype(BF16)))
             for c, _, _, _, _, _ in chains}
        for c, st, b, part, rows, sl in chains:
            ub = u[c].astype(BF16)
            yrw_ref[rows, sl] = qrm[c][CHUNK:] + av_ref[c, CHUNK:, :] + _dot(brp_ref[c], block_diag(ub))
            dec = dec_ref[c]
            upd = _dot(pkt_ref[c], jnp.concatenate([ub, vr_ref[part][:, sl]], axis=0))
            yret_ref[rows, sl] = inner_ref[c] + _dot(qd_ref[c], state_bf16(sret_ref, st))
            for rs, ls in diag_tiles:
                srw_ref[st, rs, ls] = (srw_ref[st, rs, ls] * dec[rs, :]
                                       + jnp.where(same_head[rs, ls], upd[rs, ls], 0.0))
                sret_ref[st, rs, ls] = (sret_ref[st, rs, ls] * cdec_ref[st % n_tiles, rs, ls]
                                        + kv_ref[c, rs, ls])

    last_step = pl.num_programs(0) - 1

    @pl.when(pl.program_id(0) < last_step)
    def _():
        normalise_keys()
        for sub in range(n_sub):
            for b in range(bsz):
                prepare(sub, b)
        invert_and_finish_previous()
        for sub in range(n_sub):
            apply_states(sub)

    @pl.when(pl.program_id(0) == last_step)
    def _():
        for finish in filler:
            finish()


def _mix_call(streams, dmask, qdec, kdec, cdec, rgn_g, rgn_b, wgn_g, wgn_b, rk):
    q, k, v, g, r, km, vr, lwh, lwl, a, kku, gate = streams
    bsz, s, width = q.shape
    n_tiles = width // MXU_DIM
    blk_rows = min(CHUNKS_PER_STEP * CHUNK, s)
    n_steps = s // blk_rows
    cur = pl.BlockSpec((bsz, blk_rows, width), lambda j: (0, jnp.minimum(j, n_steps - 1), 0))
    prev_idx = lambda j: (0, jnp.maximum(j - 1, 0), 0)
    prev = pl.BlockSpec((bsz, blk_rows, width), prev_idx)
    current_streams = [q, k, v, r, km, vr, lwh, lwl, a, kku]
    previous_streams = [g, gate, r, km, vr]
    consts = [dmask, qdec, kdec, cdec, rgn_g, rgn_b, wgn_g, wgn_b, rk]
    n_states = bsz * n_tiles
    n_chains = (blk_rows // CHUNK) * n_states
    tc = bsz * blk_rows
    state = (n_states, MXU_DIM, MXU_DIM)
    tile = (n_chains, MXU_DIM, MXU_DIM)
    flat = (n_chains, CHUNK, MXU_DIM)
    pair = (n_chains, 2 * CHUNK, MXU_DIM)
    return pl.pallas_call(
        _mix_kernel,
        grid=(n_steps + 1,),
        in_specs=([cur] * len(current_streams) + [prev] * len(previous_streams)
                  + [_const_spec(c.shape) for c in consts]),
        out_specs=pl.BlockSpec((bsz, blk_rows, 2 * width), prev_idx),
        out_shape=jax.ShapeDtypeStruct((bsz, s, 2 * width), BF16),
        scratch_shapes=[pltpu.VMEM(state, F32),
                        pltpu.VMEM(state, F32),
                        pltpu.VMEM((tc, width), F32), pltpu.VMEM((tc, width), F32),
                        pltpu.VMEM((tc, width), F32), pltpu.VMEM((tc, width), F32),
                        pltpu.VMEM(flat, BF16), pltpu.VMEM(flat, F32), pltpu.VMEM(flat, BF16),
                        pltpu.VMEM(pair, F32),
                        pltpu.VMEM((n_chains, MXU_DIM, 2 * CHUNK), BF16),
                        pltpu.VMEM((n_chains, MXU_DIM, LANES), F32),
                        pltpu.VMEM(pair, BF16),
                        pltpu.VMEM(flat, F32), pltpu.VMEM(flat, BF16), pltpu.VMEM(tile, F32)],
        compiler_params=pltpu.CompilerParams(dimension_semantics=("arbitrary",),
                                             vmem_limit_bytes=VMEM_LIMIT_BYTES),
        name="mix_recurrences",
    )(*current_streams, *previous_streams, *consts)


def _retention_tables(n_heads):
    h = jnp.arange(n_heads, dtype=F32)
    log_gamma = jnp.log1p(-jnp.exp2(-5.0 - h))
    lg_lane = jnp.repeat(log_gamma, HEAD_DIM)[None, :]
    idx = jnp.arange(CHUNK, dtype=F32)[:, None]
    qdec = jnp.exp((idx + 1.0) * lg_lane)
    kdec = jnp.exp((CHUNK - 1.0 - idx) * lg_lane)
    n_tiles = n_heads // HEADS_PER_TILE
    cdec = jnp.broadcast_to(jnp.exp(CHUNK * lg_lane).reshape(n_tiles, MXU_DIM, 1),
                            (n_tiles, MXU_DIM, MXU_DIM))
    rel = idx - (jnp.arange(MXU_DIM) % HEAD_DIM).astype(F32)[None, :]
    lg_tiles = lg_lane.reshape(n_tiles, 1, MXU_DIM)
    dmask = jnp.where((rel >= 0)[None], jnp.exp(jnp.where(rel >= 0, rel, 0.0)[None] * lg_tiles), 0.0)
    return dmask, qdec, kdec, cdec


def _rotary_tables(s):
    pos = jnp.arange(s, dtype=F32)
    inv_freq = ROPE_BASE ** (-jnp.arange(0, HEAD_DIM, 2, dtype=F32) / HEAD_DIM)
    ang = pos[:, None] * inv_freq[None, :]
    cos, sin = jnp.cos(ang), jnp.sin(ang)
    reps = LANES // HEAD_DIM
    cos_t = jnp.tile(jnp.concatenate([cos, cos], axis=1), (1, reps))
    sin_t = jnp.tile(jnp.concatenate([-sin, sin], axis=1), (1, reps))
    return cos_t, sin_t


def kernel(x, p, ffn1_w_gu, ffn1_w_down, ln1_g, ln1_b, w_in, ret_gn_g, ret_gn_b, rw_mu, rw_w0, rw_w_up, rw_a0, rw_a_up, rw_g_up, rw_k_k, rw_k_a, rw_r_k, rw_gn_g, rw_gn_b, w_out, ln2_g, ln2_b, ffn2_w_gu, ffn2_w_down, ln3_g, ln3_b, ple_w_proj, ple_w_gate, ple_b_gate):
    bsz, s, d = x.shape
    depth = ffn1_w_gu.shape[0]
    alpha = (2.0 * depth) ** 0.25
    width = rw_w0.shape[1]
    n_heads = width // HEAD_DIM
    ret_cols = 4 * width
    lora = DECAY_LORA + AAA_LORA + GATE_LORA
    lora_pad = -(-lora // LANES) * LANES

    dmask, qdec, kdec, cdec = _retention_tables(n_heads)
    cos_t, sin_t = _rotary_tables(s)
    row = lambda v: v.reshape(1, -1)

    h = x.reshape(bsz * s, d)
    for i in range(depth):
        h = _ffn_call(h, ffn1_w_gu[i], ffn1_w_down[i],
                      row(ln1_g[i]), row(ln1_b[i]), alpha)

        mu = row(jnp.pad(rw_mu[i], (0, lora_pad - lora)))
        n_da = DECAY_LORA + AAA_LORA
        wda = jnp.zeros((n_da, 2 * width), BF16)
        wda = wda.at[:DECAY_LORA, :width].set(rw_w_up[i].astype(BF16))
        wda = wda.at[DECAY_LORA:, width:].set(rw_a_up[i].astype(BF16))
        wg = jnp.pad(rw_g_up[i].astype(BF16), ((0, lora_pad - lora), (0, 0)))
        streams = _inproj_call(h.reshape(bsz, s, d), jnp.swapaxes(w_in[i], 0, 1), wda, wg, cos_t, sin_t, mu,
                               row(rw_w0[i]), row(rw_a0[i]), row(rw_k_k[i]), row(rw_k_a[i]))
        mixed = _mix_call(streams, dmask, qdec, kdec, cdec, row(ret_gn_g[i]), row(ret_gn_b[i]),
                          row(rw_gn_g[i]), row(rw_gn_b[i]), row(rw_r_k[i]))
        pre = (mixed.reshape(bsz * s, 2 * width), w_out[i].astype(BF16), row(ln2_g[i]), row(ln2_b[i]))
        ple = (p[i].reshape(bsz * s, -1), ple_w_proj[i].astype(BF16), ple_w_gate[i].astype(BF16),
               row(ple_b_gate[i]))
        h = _ffn_call(h, ffn2_w_gu[i], ffn2_w_down[i],
                      row(ln3_g[i]), row(ln3_b[i]), alpha, pre=pre, ple=ple)
    return h.reshape(bsz, s, d)
```
